```python
import math
import jax
import jax.numpy as jnp
from jax import lax
import numpy as np

D_MODEL = 1024
BATCH = 8
SEQ = 2048
DEPTH = 2
DEC_BATCH = 32
DEC_SEQ = 1
PAST_LEN = 8192
PAGE_SIZE = 128

HEAD_DIM = 64
H_FOX = 6
H_SB = 6
H_DIFF = 4
DIFF_DQ = 48
BRANCH_W = 384
N_BRANCH = 3
ROPE_DIM = DIFF_DQ // 4
ROPE_THETA = 500000.0
Q_BLOCK = 128
D_FF = ((8 * D_MODEL + 3 * 256 - 1) // (3 * 256)) * 256
N_IN = 9 * BRANCH_W + H_FOX + N_BRANCH * D_MODEL
FORGET_BIAS_MEAN = 2.0
NORM_EPS = 1e-6
SUBLN_EPS = 1e-5

kernel_name = 'hybrid_fox_stickbreak_diff_decode_step'


def rms_norm(x, gain, eps=NORM_EPS):
    xf = x.astype(jnp.float32)
    y = xf * lax.rsqrt(jnp.mean(xf * xf, axis=-1, keepdims=True) + eps)
    return (y * gain.astype(jnp.float32)).astype(x.dtype)


def lambda_init_fn(layer):
    return 0.8 - 0.6 * math.exp(-0.3 * layer)


def partial_rotary(x, pos):
    half = ROPE_DIM // 2
    inv_freq = ROPE_THETA ** (-jnp.arange(0, ROPE_DIM, 2, dtype=jnp.float32) / ROPE_DIM)
    ang = pos.astype(jnp.float32)[:, None] * inv_freq[None, :]
    cos = jnp.cos(ang)[None, :, None, None, :]
    sin = jnp.sin(ang)[None, :, None, None, :]
    xf = x.astype(jnp.float32)
    x1, x2 = xf[..., :half], xf[..., half:ROPE_DIM]
    out = jnp.concatenate([x1 * cos - x2 * sin, x2 * cos + x1 * sin, xf[..., ROPE_DIM:]], axis=-1)
    return out.astype(x.dtype)


def sweep_queries(fn, q_arrays, q_pos):
    t = q_pos.shape[0]
    if t > Q_BLOCK and t % Q_BLOCK == 0:
        nb = t // Q_BLOCK
        blocks = tuple(jnp.swapaxes(a.reshape(a.shape[0], nb, Q_BLOCK, *a.shape[2:]), 0, 1) for a in q_arrays)
        out = lax.map(lambda args: fn(*args), (*blocks, q_pos.reshape(nb, Q_BLOCK)))
        out = jnp.swapaxes(out, 0, 1)
        return out.reshape(out.shape[0], t, *out.shape[3:])
    return fn(*q_arrays, q_pos)


def fox_attention(q, c_q, k, v, c_k, q_pos):
    k_pos = jnp.arange(k.shape[1])
    c_k_t = jnp.swapaxes(c_k, 1, 2)[:, :, None, :]

    def block(qb, cqb, pb):
        s = jnp.einsum('bqhd,bkhd->bhqk', qb, k).astype(jnp.float32) * (HEAD_DIM ** -0.5)
        s = s + jnp.swapaxes(cqb, 1, 2)[:, :, :, None] - c_k_t
        mask = k_pos[None, :] <= pb[:, None]
        p = jax.nn.softmax(jnp.where(mask, s, -jnp.inf), axis=-1)
        return jnp.einsum('bhqk,bkhd->bqhd', p.astype(v.dtype), v)

    return sweep_queries(block, (q, c_q), q_pos)


def stick_breaking_attention(q, k, v, q_pos):
    k_pos = jnp.arange(k.shape[1])

    def block(qb, pb):
        z = jnp.einsum('bqhd,bkhd->bhqk', qb, k).astype(jnp.float32) * (HEAD_DIM ** -0.5)
        strict = k_pos[None, :] < pb[:, None]
        log_fail = jnp.where(strict, jax.nn.log_sigmoid(-z), 0.0)
        later = lax.cumsum(log_fail, axis=3, reverse=True) - log_fail
        w = jnp.where(strict, jnp.exp(jax.nn.log_sigmoid(z) + later), 0.0)
        return jnp.einsum('bhqk,bkhd->bqhd', w.astype(v.dtype), v)

    return sweep_queries(block, (q,), q_pos)


def diff_attention(q, k, v, q_pos, lam):
    k_pos = jnp.arange(k.shape[1])

    def block(qb, pb):
        s = jnp.einsum('bqhcd,bkhcd->bchqk', qb, k).astype(jnp.float32) * (DIFF_DQ ** -0.5)
        mask = k_pos[None, :] <= pb[:, None]
        p = jax.nn.softmax(jnp.where(mask, s, -jnp.inf), axis=-1)
        a = p[:, 0] - lam * p[:, 1]
        return jnp.einsum('bhqk,bkhe->bqhe', a.astype(v.dtype), v)

    return sweep_queries(block, (q,), q_pos)


def token_mixer(h, q_pos, past, w_in, b_forget, diff_lambda, diff_subln, w_branch, w_out, lambda_init):
    b, t, _ = h.shape
    splits = [BRANCH_W * i for i in range(1, 10)] + [9 * BRANCH_W + H_FOX]
    qa, ka, va, qb, kb, vb, qc, kc, vc, f_logit, g = jnp.split(h @ w_in, splits, axis=-1)
    qa = qa.reshape(b, t, H_FOX, HEAD_DIM)
    ka = ka.reshape(b, t, H_FOX, HEAD_DIM)
    va = va.reshape(b, t, H_FOX, HEAD_DIM)
    logf = jax.nn.log_sigmoid((f_logit + b_forget).astype(jnp.float32))
    qb = qb.reshape(b, t, H_SB, HEAD_DIM)
    kb = kb.reshape(b, t, H_SB, HEAD_DIM)
    vb = vb.reshape(b, t, H_SB, HEAD_DIM)
    qc = partial_rotary(qc.reshape(b, t, H_DIFF, 2, DIFF_DQ), q_pos)
    kc = partial_rotary(kc.reshape(b, t, H_DIFF, 2, DIFF_DQ), q_pos)
    vc = vc.reshape(b, t, H_DIFF, 2 * DIFF_DQ)

    new_rows = (jnp.stack([ka, va], axis=2),
                logf,
                jnp.stack([kb, vb], axis=2),
                jnp.stack([kc.reshape(b, t, H_DIFF, 2 * DIFF_DQ), vc], axis=2))
    if past is None:
        fox_kv, fox_logf, sb_kv, diff_kv = new_rows
    else:
        fox_kv, fox_logf, sb_kv, diff_kv = [jnp.concatenate([p_, n_], axis=1) for p_, n_ in zip(past, new_rows)]
    tk = fox_kv.shape[1]

    c_k = jnp.cumsum(fox_logf.astype(jnp.float32), axis=1)
    o_a = fox_attention(qa, c_k[:, tk - t:], fox_kv[:, :, 0], fox_kv[:, :, 1], c_k, q_pos)
    o_b = stick_breaking_attention(qb, sb_kv[:, :, 0], sb_kv[:, :, 1], q_pos)
    lq1, lk1, lq2, lk2 = diff_lambda.astype(jnp.float32)
    lam = jnp.exp(jnp.sum(lq1 * lk1)) - jnp.exp(jnp.sum(lq2 * lk2)) + lambda_init
    k_c_all = diff_kv[:, :, 0].reshape(b, tk, H_DIFF, 2, DIFF_DQ)
    o_c = diff_attention(qc, k_c_all, diff_kv[:, :, 1], q_pos, lam)
    o_c = rms_norm(o_c, diff_subln, SUBLN_EPS) * (1.0 - lambda_init)

    branches = jnp.stack([o_a.reshape(b, t, BRANCH_W), o_b.reshape(b, t, BRANCH_W),
                          o_c.reshape(b, t, BRANCH_W)], axis=2)
    up = jnp.einsum('btnc,ncd->btnd', branches, w_branch)
    gates = jax.nn.sigmoid(g.reshape(b, t, N_BRANCH, D_MODEL).astype(jnp.float32)).astype(up.dtype)
    merged = jnp.sum(gates * up, axis=2)
    return merged @ w_out, new_rows


def swiglu(h, w_gate_up, w_down):
    gate, up = jnp.split(h @ w_gate_up, 2, axis=-1)
    return (jax.nn.silu(gate) * up) @ w_down


def setup_inputs(seed: int = 0) -> dict:
    key = jax.random.key(seed)
    ks = jax.random.split(key, 20)
    n_pages = PAST_LEN // PAGE_SIZE
    n_used = DEC_BATCH * n_pages
    n_phys = n_used + max(1, n_used // 4)
    f32 = jnp.float32
    nrm = lambda k, shape: jax.random.normal(k, shape, dtype=f32)
    page_table = jax.random.permutation(ks[0], n_phys)[:n_used].reshape(DEC_BATCH, n_pages).astype(jnp.int32)
    return {
        'x_prompt': nrm(ks[1], (BATCH, SEQ, D_MODEL)),
        'x_sample': nrm(ks[2], (DEC_BATCH, DEC_SEQ, D_MODEL)),
        'cache_fox_kv': nrm(ks[3], (DEPTH, n_phys, PAGE_SIZE, 2, H_FOX, HEAD_DIM)),
        'cache_fox_logf': jax.nn.log_sigmoid(FORGET_BIAS_MEAN + nrm(ks[4], (DEPTH, n_phys, PAGE_SIZE, H_FOX))),
        'cache_sb_kv': nrm(ks[5], (DEPTH, n_phys, PAGE_SIZE, 2, H_SB, HEAD_DIM)),
        'cache_diff_kv': nrm(ks[6], (DEPTH, n_phys, PAGE_SIZE, 2, H_DIFF, 2 * DIFF_DQ)),
        'page_table': page_table,
        'norm_mix_pre': 1.0 + 0.05 * nrm(ks[7], (DEPTH, D_MODEL)),
        'norm_mix_post': 1.0 + 0.05 * nrm(ks[8], (DEPTH, D_MODEL)),
        'norm_ffn_pre': 1.0 + 0.05 * nrm(ks[9], (DEPTH, D_MODEL)),
        'norm_ffn_post': 1.0 + 0.05 * nrm(ks[10], (DEPTH, D_MODEL)),
        'w_in': nrm(ks[11], (DEPTH, D_MODEL, N_IN)) * D_MODEL ** -0.5,
        'b_forget': FORGET_BIAS_MEAN + 0.5 * nrm(ks[12], (DEPTH, H_FOX)),
        'diff_lambda': 0.1 * nrm(ks[13], (DEPTH, 4, DIFF_DQ)),
        'diff_subln': 1.0 + 0.05 * nrm(ks[14], (DEPTH, 2 * DIFF_DQ)),
        'w_branch': nrm(ks[15], (DEPTH, N_BRANCH, BRANCH_W, D_MODEL)) * BRANCH_W ** -0.5,
        'w_out': nrm(ks[16], (DEPTH, D_MODEL, D_MODEL)) * D_MODEL ** -0.5,
        'w_gate_up': nrm(ks[17], (DEPTH, D_MODEL, 2 * D_FF)) * D_MODEL ** -0.5,
        'w_down': nrm(ks[18], (DEPTH, D_FF, D_MODEL)) * D_FF ** -0.5,
    }


def reference(x_prompt, x_sample, cache_fox_kv, cache_fox_logf, cache_sb_kv, cache_diff_kv, page_table,
              norm_mix_pre, norm_mix_post, norm_ffn_pre, norm_ffn_post, w_in, b_forget, diff_lambda,
              diff_subln, w_branch, w_out, w_gate_up, w_down):
    def layer(x, q_pos, past, l):
        mix, rows = token_mixer(rms_norm(x, norm_mix_pre[l]), q_pos, past, w_in[l], b_forget[l],
                                diff_lambda[l], diff_subln[l], w_branch[l], w_out[l], lambda_init_fn(l))
        x = x + rms_norm(mix, norm_mix_post[l])
        ffn = swiglu(rms_norm(x, norm_ffn_pre[l]), w_gate_up[l], w_down[l])
        return x + rms_norm(ffn, norm_ffn_post[l]), rows

    pos_p = jnp.arange(x_prompt.shape[1])
    h = x_prompt
    rows_p = []
    for l in range(DEPTH):
        h, r = layer(h, pos_p, None, l)
        rows_p.append(r)
    y_prompt = h

    dec_b = x_sample.shape[0]
    past_len = page_table.shape[1] * cache_fox_kv.shape[2]
    pos_s = past_len + jnp.arange(x_sample.shape[1])

    def gather(cache, l):
        g = cache[l][page_table]
        return g.reshape(dec_b, past_len, *g.shape[3:])

    h = x_sample
    rows_s = []
    for l in range(DEPTH):
        past = (gather(cache_fox_kv, l), gather(cache_fox_logf, l),
                gather(cache_sb_kv, l), gather(cache_diff_kv, l))
        h, r = layer(h, pos_s, past, l)
        rows_s.append(r)
    y_sample = h

    def stack(rows, i):
        return jnp.stack([r[i] for r in rows])

    return (y_prompt, y_sample,
            stack(rows_p, 0), stack(rows_p, 1), stack(rows_p, 2), stack(rows_p, 3),
            stack(rows_s, 0), stack(rows_s, 1), stack(rows_s, 2), stack(rows_s, 3))
```

```python
import functools
import math

import jax
import jax.numpy as jnp
from jax import lax
from jax.experimental import pallas as pl
from jax.experimental.pallas import tpu as pltpu

F32 = jnp.float32
BF16 = jnp.bfloat16

HEAD_DIM = 64
H_FOX = 6
H_SB = 6
H_DIFF = 4
DIFF_DQ = 48
DIFF_DV = 2 * DIFF_DQ
BRANCH_W = 384
N_BRANCH = 3
ROPE_DIM = DIFF_DQ // 4
ROPE_THETA = 500000.0
NORM_EPS = 1e-6
SUBLN_EPS = 1e-5

LANES = 128
HEADS_PER_LANE_BLOCK = LANES // HEAD_DIM
F_PAD = LANES
DIFF_PAD_W = H_DIFF * LANES
NEG_BIG = -1e30
VMEM_LIMIT = 56 * 1024 * 1024
DEC_ROWS = 16


def _cparams(sem):
    return pltpu.CompilerParams(dimension_semantics=sem, vmem_limit_bytes=VMEM_LIMIT)


def _rms_norm(x, gain, eps):
    ms = jnp.mean(x * x, axis=-1, keepdims=True)
    return x * lax.rsqrt(ms + eps) * gain


def _softplus_neg_abs(z):
    return jnp.log1p(jnp.exp(-jnp.abs(z)))


def _split_hi_lo(x):
    hi = x.astype(BF16)
    lo = (x - hi.astype(F32)).astype(BF16)
    return hi, lo


def _in_group(index, group, width):
    lo = group * width
    return (index >= lo) & (index < lo + width)


def _dot(a, b):
    return jnp.dot(a, b, preferred_element_type=F32)


def _dot_nt(a, b):
    return lax.dot_general(a, b, (((1,), (1,)), ((), ())), preferred_element_type=F32)


def _inproj_kernel(x_ref, gain_ref, w_ref, bf_ref, cos_ref, sina_ref, sinb_ref,
                   qkv_ab_ref, foxkv_ref, sbkv_ref, diffkv_ref, qkv_c_ref, logf_ref):
    W = BRANCH_W
    h = _rms_norm(x_ref[...], gain_ref[...], NORM_EPS).astype(BF16)

    def proj(c):
        return _dot(h, w_ref[:, c * W:(c + 1) * W])

    def rotary(r):
        return (r * cos_ref[...] + pltpu.roll(r, W - ROPE_DIM // 2, 1) * sina_ref[...]
                + pltpu.roll(r, ROPE_DIM // 2, 1) * sinb_ref[...])

    for c, kv_ref in ((0, foxkv_ref), (3, sbkv_ref)):
        qkv_ab_ref[:, c * W:(c + 1) * W] = proj(c).astype(BF16)
        for j in range(2):
            r = proj(c + 1 + j)
            qkv_ab_ref[:, (c + 1 + j) * W:(c + 2 + j) * W] = r.astype(BF16)
            kv_ref[:, j * W:(j + 1) * W] = r

    qc = rotary(proj(6)) * (DIFF_DQ ** -0.5)
    qkv_c_ref[:, 0:W] = qc.astype(BF16)
    kc = rotary(proj(7))
    qkv_c_ref[:, W:2 * W] = kc.astype(BF16)
    diffkv_ref[:, 0:W] = kc
    vc = proj(8)
    qkv_c_ref[:, 2 * W:3 * W] = vc.astype(BF16)
    diffkv_ref[:, W:2 * W] = vc

    f = _dot(h, w_ref[:, 9 * W:9 * W + F_PAD]) + bf_ref[...]
    logf_ref[...] = jnp.minimum(f, 0.0) - _softplus_neg_abs(f)


def _inproj(x, gain, w, bf, cos_t, sina_t, sinb_t, tm):
    n, d = x.shape
    W = BRANCH_W
    n_tab = cos_t.shape[0] // tm
    row = lambda i: (i, 0)
    fixed = lambda i: (0, 0)
    tab = lambda i: (i % n_tab, 0)
    out_shape = (
        jax.ShapeDtypeStruct((n, 6 * W), BF16),
        jax.ShapeDtypeStruct((n, 2 * W), F32),
        jax.ShapeDtypeStruct((n, 2 * W), F32),
        jax.ShapeDtypeStruct((n, 2 * W), F32),
        jax.ShapeDtypeStruct((n, 3 * W), BF16),
        jax.ShapeDtypeStruct((n, F_PAD), F32),
    )
    return pl.pallas_call(
        _inproj_kernel,
        grid=(n // tm,),
        in_specs=[
            pl.BlockSpec((tm, d), row),
            pl.BlockSpec((1, d), fixed),
            pl.BlockSpec(w.shape, fixed),
            pl.BlockSpec((1, F_PAD), fixed),
            pl.BlockSpec((tm, W), tab),
            pl.BlockSpec((tm, W), tab),
            pl.BlockSpec((tm, W), tab),
        ],
        out_specs=(
            pl.BlockSpec((tm, 6 * W), row),
            pl.BlockSpec((tm, 2 * W), row),
            pl.BlockSpec((tm, 2 * W), row),
            pl.BlockSpec((tm, 2 * W), row),
            pl.BlockSpec((tm, 3 * W), row),
            pl.BlockSpec((tm, F_PAD), row),
        ),
        out_shape=out_shape,
        compiler_params=_cparams(("parallel",)),
        name="inproj",
    )(x, gain, w, bf, cos_t, sina_t, sinb_t)


def _cumsum_kernel(x_ref, o_ref):
    x = x_ref[0]
    t = x.shape[-1]
    lane = lax.broadcasted_iota(jnp.int32, x.shape, 1)
    shift = 1
    while shift < t:
        x = x + jnp.where(lane >= shift, pltpu.roll(x, shift, 1), 0.0)
        shift *= 2
    o_ref[0] = x


def _cumsum_time(logf_t):
    b, r, t = logf_t.shape
    blk = pl.BlockSpec((1, r, t), lambda i: (i, 0, 0))
    return pl.pallas_call(
        _cumsum_kernel, grid=(b,), in_specs=[blk], out_specs=blk,
        out_shape=jax.ShapeDtypeStruct(logf_t.shape, F32),
        compiler_params=_cparams(("parallel",)), name="cumsum_logf",
    )(logf_t)


def _lane_block(ref, rows, blk):
    return ref[0, rows, blk * LANES:(blk + 1) * LANES]


def _online_softmax_step(s, v, m, l, acc):
    m_new = jnp.maximum(m, jnp.max(s, axis=1, keepdims=True))
    alpha = jnp.exp(m - m_new)
    p = jnp.exp(s - m_new)
    l = alpha * l + jnp.sum(p, axis=1, keepdims=True)
    acc = alpha * acc + _dot(p.astype(BF16), v)
    return m_new, l, acc


def _fox_prompt_kernel(q_ref, k_ref, v_ref, cq_ref, ck_ref, o_ref, *, tq):
    i = pl.program_id(1)
    lane = lax.broadcasted_iota(jnp.int32, (tq, LANES), 1)
    row = lax.broadcasted_iota(jnp.int32, (tq, tq), 0)
    col = lax.broadcasted_iota(jnp.int32, (tq, tq), 1)
    causal = col <= row
    q0 = pl.multiple_of(i * tq, tq)
    for blk in range(H_FOX // HEADS_PER_LANE_BLOCK):
        q_pair = _lane_block(q_ref, slice(None), blk)
        outs = []
        for sub in range(HEADS_PER_LANE_BLOCK):
            head = blk * HEADS_PER_LANE_BLOCK + sub
            in_head = _in_group(lane, sub, HEAD_DIM)
            qh = jnp.where(in_head, q_pair, jnp.zeros_like(q_pair))
            cq = cq_ref[0, :, head:head + 1]

            def scores(k0):
                k = _lane_block(k_ref, pl.ds(k0, tq), blk)
                ck = ck_ref[0, head:head + 1, pl.ds(k0, tq)]
                return _dot_nt(qh, k) + (cq - ck)

            def body(kb, carry):
                k0 = pl.multiple_of(kb * tq, tq)
                v = _lane_block(v_ref, pl.ds(k0, tq), blk)
                return _online_softmax_step(scores(k0), v, *carry)

            init = (jnp.full((tq, 1), NEG_BIG, F32), jnp.zeros((tq, 1), F32),
                    jnp.zeros((tq, LANES), F32))
            carry = lax.fori_loop(0, i, body, init)
            s = jnp.where(causal, scores(q0), NEG_BIG)
            v = _lane_block(v_ref, pl.ds(q0, tq), blk)
            _, l, acc = _online_softmax_step(s, v, *carry)
            outs.append((in_head, acc / l))
        o = jnp.where(outs[0][0], outs[0][1], outs[1][1])
        o_ref[0, :, blk * LANES:(blk + 1) * LANES] = o.astype(o_ref.dtype)


def _sb_prompt_kernel(q_ref, k_ref, v_ref, tri_ref, o_ref, *, tq):
    i = pl.program_id(1)
    lane = lax.broadcasted_iota(jnp.int32, (tq, LANES), 1)
    row = lax.broadcasted_iota(jnp.int32, (tq, tq), 0)
    col = lax.broadcasted_iota(jnp.int32, (tq, tq), 1)
    strict = col < row
    tri = tri_ref[...]
    q0 = pl.multiple_of(i * tq, tq)
    for blk in range(H_SB // HEADS_PER_LANE_BLOCK):
        q_pair = _lane_block(q_ref, slice(None), blk)
        outs = []
        for sub in range(HEADS_PER_LANE_BLOCK):
            in_head = _in_group(lane, sub, HEAD_DIM)
            qh = jnp.where(in_head, q_pair, jnp.zeros_like(q_pair))

            def tile(k0, later_blocks, mask):
                k = _lane_block(k_ref, pl.ds(k0, tq), blk)
                v = _lane_block(v_ref, pl.ds(k0, tq), blk)
                z = _dot_nt(qh, k)
                sp = _softplus_neg_abs(z)
                log_hit = jnp.minimum(z, 0.0) - sp
                log_fail = jnp.minimum(-z, 0.0) - sp
                if mask is not None:
                    log_fail = jnp.where(mask, log_fail, 0.0)
                hi, lo = _split_hi_lo(log_fail)
                later = _dot(hi, tri) + _dot(lo, tri) + later_blocks
                w = jnp.exp(log_hit + later)
                if mask is not None:
                    w = jnp.where(mask, w, 0.0)
                return _dot(w.astype(BF16), v), jnp.sum(log_fail, axis=1, keepdims=True)

            acc, later_blocks = tile(q0, jnp.zeros((tq, 1), F32), strict)

            def body(step, carry):
                acc, later_blocks = carry
                k0 = pl.multiple_of((i - 1 - step) * tq, tq)
                d_acc, d_later = tile(k0, later_blocks, None)
                return acc + d_acc, later_blocks + d_later

            acc, _ = lax.fori_loop(0, i, body, (acc, later_blocks))
            outs.append((in_head, acc))
        o = jnp.where(outs[0][0], outs[0][1], outs[1][1])
        o_ref[0, :, blk * LANES:(blk + 1) * LANES] = o.astype(o_ref.dtype)


def _diff_lambda(dl, lambda_init):
    l1 = jnp.sum(dl[0:1] * dl[1:2], axis=1, keepdims=True)
    l2 = jnp.sum(dl[2:3] * dl[3:4], axis=1, keepdims=True)
    return jnp.exp(l1) - jnp.exp(l2) + lambda_init


def _diff_prompt_kernel(q_ref, k_ref, v_ref, dl_ref, subln_ref, o_ref, *, tq, lambda_init):
    i = pl.program_id(1)
    lane = lax.broadcasted_iota(jnp.int32, (tq, LANES), 1)
    row = lax.broadcasted_iota(jnp.int32, (tq, tq), 0)
    col = lax.broadcasted_iota(jnp.int32, (tq, tq), 1)
    causal = col <= row
    lam = _diff_lambda(dl_ref[...], lambda_init)
    q0 = pl.multiple_of(i * tq, tq)
    for head in range(H_DIFF):
        q_head = _lane_block(q_ref, slice(None), head)
        normed = []
        for comp in range(2):
            qc = jnp.where(_in_group(lane, comp, HEAD_DIM), q_head, jnp.zeros_like(q_head))

            def body(kb, carry):
                k0 = pl.multiple_of(kb * tq, tq)
                k = _lane_block(k_ref, pl.ds(k0, tq), head)
                v = _lane_block(v_ref, pl.ds(k0, tq), head)
                return _online_softmax_step(_dot_nt(qc, k), v, *carry)

            init = (jnp.full((tq, 1), NEG_BIG, F32), jnp.zeros((tq, 1), F32),
                    jnp.zeros((tq, LANES), F32))
            carry = lax.fori_loop(0, i, body, init)
            k = _lane_block(k_ref, pl.ds(q0, tq), head)
            v = _lane_block(v_ref, pl.ds(q0, tq), head)
            s = jnp.where(causal, _dot_nt(qc, k), NEG_BIG)
            _, l, acc = _online_softmax_step(s, v, *carry)
            normed.append(acc / l)
        o = normed[0] - lam * normed[1]
        ms = jnp.sum(o * o, axis=1, keepdims=True) * (1.0 / DIFF_DV)
        o = o * lax.rsqrt(ms + SUBLN_EPS) * subln_ref[...] * (1.0 - lambda_init)
        o_ref[0, :, head * LANES:(head + 1) * LANES] = o.astype(o_ref.dtype)


def _prompt_attention_call(kernel, name, q, k, v, extra, extra_specs, out_w, tq):
    b, t = q[0].shape[0], q[0].shape[1]

    def spec(view, rows, per_tile):
        arr, col0, width = view
        assert col0 % width == 0
        cblk = col0 // width
        if per_tile:
            return pl.BlockSpec((1, rows, width), lambda bi, i: (bi, i, cblk))
        return pl.BlockSpec((1, rows, width), lambda bi, i: (bi, 0, cblk))

    return pl.pallas_call(
        functools.partial(kernel, tq=tq),
        grid=(b, t // tq),
        in_specs=[spec(q, tq, True), spec(k, t, False), spec(v, t, False)] + extra_specs,
        out_specs=pl.BlockSpec((1, tq, out_w), lambda bi, i: (bi, i, 0)),
        out_shape=jax.ShapeDtypeStruct((b, t, out_w), BF16),
        compiler_params=_cparams(("parallel", "arbitrary")),
        name=name,
    )(q[0], k[0], v[0], *extra)


def _decode_kernel(pt_ref, qab_ref, qc_ref, lfnew_ref, dl_ref, subln_ref, tri_ref, *rest,
                   pages_per_step, lambda_init):
    del pt_ref
    P = pages_per_step
    fox_refs = rest[0:P]
    lf_refs = rest[P:2 * P]
    sb_refs = rest[2 * P:3 * P]
    diff_refs = rest[3 * P:4 * P]
    o_ref = rest[4 * P]
    (qa_s, qb_s, qd_s, m_a, l_a, acc_a, carry_a, acc_b, carry_b, m_c, l_c, acc_c) = rest[4 * P + 1:]
    W = BRANCH_W
    R = DEC_ROWS
    step = pl.program_id(1)
    n_steps = pl.num_programs(1)
    lane = lax.broadcasted_iota(jnp.int32, (R, W), 1)
    rowi = lax.broadcasted_iota(jnp.int32, (R, W), 0)
    head64 = _in_group(lane, rowi, HEAD_DIM)
    comp48 = _in_group(lane, rowi, DIFF_DQ)
    tri = tri_ref[...]

    def bcast(x):
        return jnp.broadcast_to(x, (R, W))

    @pl.when(step == 0)
    def _init():
        qab = qab_ref[0]
        qc = qc_ref[0]
        zero = jnp.zeros((R, W), F32)
        qa = jnp.where(head64, bcast(qab[:, 0:W].astype(F32)), zero)
        qb = jnp.where(head64, bcast(qab[:, 3 * W:4 * W].astype(F32)), zero)
        qd = jnp.where(comp48, bcast(qc[:, 0:W].astype(F32)), zero)
        qa_s[...] = qa.astype(BF16)
        qb_s[...] = qb.astype(BF16)
        qd_s[...] = qd.astype(BF16)
        ka = bcast(qab[:, W:2 * W].astype(F32))
        m_a[...] = jnp.broadcast_to(jnp.sum(qa * ka, axis=1, keepdims=True), (R, LANES))
        l_a[...] = jnp.ones((R, LANES), F32)
        acc_a[...] = bcast(qab[:, 2 * W:3 * W].astype(F32))
        lane_f = lax.broadcasted_iota(jnp.int32, (R, F_PAD), 1)
        row_f = lax.broadcasted_iota(jnp.int32, (R, F_PAD), 0)
        lf_col = jnp.sum(jnp.where(lane_f == row_f, jnp.broadcast_to(lfnew_ref[0], (R, F_PAD)), 0.0),
                         axis=1, keepdims=True)
        carry_a[...] = jnp.broadcast_to(lf_col, (R, LANES))
        acc_b[...] = jnp.zeros((R, W), F32)
        carry_b[...] = jnp.zeros((R, LANES), F32)
        kd = bcast(qc[:, W:2 * W].astype(F32))
        m_c[...] = jnp.broadcast_to(jnp.sum(qd * kd, axis=1, keepdims=True), (R, LANES))
        l_c[...] = jnp.ones((R, LANES), F32)
        acc_c[...] = bcast(qc[:, 2 * W:3 * W].astype(F32))

    def suffix_sum(x):
        hi, lo = _split_hi_lo(x)
        return _dot(hi, tri) + _dot(lo, tri)

    def softmax_page(s, v, m_ref, l_ref, acc_ref):
        m = m_ref[...]
        m_new = jnp.maximum(m, jnp.max(s, axis=1, keepdims=True))
        alpha = jnp.exp(m - m_new)
        p = jnp.exp(s - m_new[:, 0:1])
        l_ref[...] = alpha * l_ref[...] + jnp.sum(p, axis=1, keepdims=True)
        acc_ref[...] = alpha[:, 0:1] * acc_ref[...] + _dot(p.astype(BF16), v)
        m_ref[...] = m_new

    for j in range(P):
        kv = fox_refs[j][...]
        lf = lf_refs[j][...]
        ca = carry_a[...]
        s = _dot_nt(qa_s[...], kv[:, 0:W].astype(BF16)) + suffix_sum(lf) + ca
        softmax_page(s, kv[:, W:2 * W].astype(BF16), m_a, l_a, acc_a)
        carry_a[...] = ca + jnp.sum(lf, axis=1, keepdims=True)

        kv = sb_refs[j][...]
        z = _dot_nt(qb_s[...], kv[:, 0:W].astype(BF16))
        sp = _softplus_neg_abs(z)
        log_hit = jnp.minimum(z, 0.0) - sp
        log_fail = jnp.minimum(-z, 0.0) - sp
        cb = carry_b[...]
        w = jnp.exp(log_hit + suffix_sum(log_fail) + cb)
        acc_b[...] += _dot(w.astype(BF16), kv[:, W:2 * W].astype(BF16))
        carry_b[...] = cb + jnp.sum(log_fail, axis=1, keepdims=True)

        kv = diff_refs[j][...]
        s = _dot_nt(qd_s[...], kv[:, 0:W].astype(BF16))
        softmax_page(s, kv[:, W:2 * W].astype(BF16), m_c, l_c, acc_c)

    @pl.when(step == n_steps - 1)
    def _finish():
        zero = jnp.zeros((R, W), F32)
        o_a = jnp.sum(jnp.where(head64, acc_a[...] / l_a[:, 0:1], zero), axis=0, keepdims=True)
        o_b = jnp.sum(jnp.where(head64, acc_b[...], zero), axis=0, keepdims=True)
        n_c = acc_c[...] / l_c[:, 0:1]
        odd_row = (rowi & 1) == 1
        head_lo = (rowi - (rowi & 1)) * DIFF_DQ
        own_head = (lane >= head_lo) & (lane < head_lo + DIFF_DV)
        first = jnp.sum(jnp.where(own_head & ~odd_row, n_c, zero), axis=0, keepdims=True)
        second = jnp.sum(jnp.where(own_head & odd_row, n_c, zero), axis=0, keepdims=True)
        o_c = first - _diff_lambda(dl_ref[...], lambda_init) * second
        lane1 = lax.broadcasted_iota(jnp.int32, (1, W), 1)
        inv = jnp.zeros((1, W), F32)
        for head in range(H_DIFF):
            in_head = _in_group(lane1, head, DIFF_DV)
            ms = jnp.sum(jnp.where(in_head, o_c * o_c, 0.0), axis=1, keepdims=True) * (1.0 / DIFF_DV)
            inv = jnp.where(in_head, lax.rsqrt(ms + SUBLN_EPS), inv)
        o_c = o_c * inv * subln_ref[...] * (1.0 - lambda_init)
        o_ref[0, :, 0:W] = o_a.astype(o_ref.dtype)
        o_ref[0, :, W:2 * W] = o_b.astype(o_ref.dtype)
        o_ref[0, :, 2 * W:3 * W] = o_c.astype(o_ref.dtype)


def _decode_attention(page_table, layer, qkv_ab, qkv_c, logf_new, diff_lambda, subln_row, tri,
                      cache_fox, cache_lf_t, cache_sb, cache_diff, lambda_init, pages_per_step):
    nb, n_pages = page_table.shape
    P = pages_per_step
    assert n_pages % P == 0
    n_steps = n_pages // P
    page = cache_fox.shape[2]
    W = BRANCH_W
    R = DEC_ROWS

    def page_spec(shape, j):
        def index(b, s, pt):
            return (layer, pt[b, n_pages - 1 - (s * P + j)], 0, 0)
        return pl.BlockSpec((None, None) + shape, index)

    per_seq = lambda b, s, pt: (b, 0, 0)
    fixed = lambda b, s, pt: (0, 0)
    in_specs = [
        pl.BlockSpec((1, 1, 6 * W), per_seq),
        pl.BlockSpec((1, 1, 3 * W), per_seq),
        pl.BlockSpec((1, 1, F_PAD), per_seq),
        pl.BlockSpec(diff_lambda.shape, fixed),
        pl.BlockSpec((1, W), fixed),
        pl.BlockSpec(tri.shape, fixed),
    ]
    in_specs += [page_spec((page, 2 * W), j) for j in range(P)]
    in_specs += [page_spec((R, page), j) for j in range(P)]
    in_specs += [page_spec((page, 2 * W), j) for j in range(P)]
    in_specs += [page_spec((page, 2 * W), j) for j in range(P)]
    scratch = [pltpu.VMEM((R, W), BF16)] * 3 + [
        pltpu.VMEM((R, LANES), F32), pltpu.VMEM((R, LANES), F32), pltpu.VMEM((R, W), F32),
        pltpu.VMEM((R, LANES), F32),
        pltpu.VMEM((R, W), F32), pltpu.VMEM((R, LANES), F32),
        pltpu.VMEM((R, LANES), F32), pltpu.VMEM((R, LANES), F32), pltpu.VMEM((R, W), F32),
    ]
    grid_spec = pltpu.PrefetchScalarGridSpec(
        num_scalar_prefetch=1, grid=(nb, n_steps), in_specs=in_specs,
        out_specs=pl.BlockSpec((1, 1, 3 * W), per_seq), scratch_shapes=scratch)
    out = pl.pallas_call(
        functools.partial(_decode_kernel, pages_per_step=P, lambda_init=lambda_init),
        grid_spec=grid_spec,
        out_shape=jax.ShapeDtypeStruct((nb, 1, 3 * W), BF16),
        compiler_params=_cparams(("parallel", "arbitrary")),
        name="decode_attention",
    )(page_table, qkv_ab.reshape(nb, 1, 6 * W), qkv_c.reshape(nb, 1, 3 * W),
      logf_new.reshape(nb, 1, F_PAD), diff_lambda, subln_row, tri,
      *([cache_fox] * P), *([cache_lf_t] * P), *([cache_sb] * P), *([cache_diff] * P))
    return out.reshape(nb, 3 * W)


def _merge_kernel(x_ref, oa_ref, ob_ref, oc_ref, gpre_ref, wg_ref, wa_ref, wb_ref, wc_ref,
                  wout_ref, gpost_ref, y_ref):
    x = x_ref[...]
    d = x.shape[1]
    h = _rms_norm(x, gpre_ref[...], NORM_EPS).astype(BF16)
    merged = None
    for n, (o_ref, w_ref) in enumerate(((oa_ref, wa_ref), (ob_ref, wb_ref), (oc_ref, wc_ref))):
        gate = jax.nn.sigmoid(_dot(h, wg_ref[:, n * d:(n + 1) * d]))
        term = gate * _dot(o_ref[...], w_ref[...])
        merged = term if merged is None else merged + term
    mix = _dot(merged.astype(BF16), wout_ref[...])
    y_ref[...] = x + _rms_norm(mix, gpost_ref[...], NORM_EPS)


def _merge(x, o_a, o_b, o_c, gpre, wg, wa, wb, wc, wout, gpost, tm):
    n, d = x.shape
    row = lambda i: (i, 0)
    fixed = lambda i: (0, 0)
    full = lambda a: pl.BlockSpec(a.shape, fixed)
    return pl.pallas_call(
        _merge_kernel,
        grid=(n // tm,),
        in_specs=[pl.BlockSpec((tm, d), row), pl.BlockSpec((tm, o_a.shape[1]), row),
                  pl.BlockSpec((tm, o_b.shape[1]), row), pl.BlockSpec((tm, o_c.shape[1]), row),
                  full(gpre), full(wg), full(wa), full(wb), full(wc), full(wout), full(gpost)],
        out_specs=pl.BlockSpec((tm, d), row),
        out_shape=jax.ShapeDtypeStruct((n, d), F32),
        compiler_params=_cparams(("parallel",)),
        name="merge",
    )(x, o_a, o_b, o_c, gpre, wg, wa, wb, wc, wout, gpost)


def _ffn_kernel(x_ref, gpre_ref, wgu_ref, wdown_ref, gpost_ref, y_ref, act_ref, *, chunk):
    x = x_ref[...]
    d_ff = wdown_ref.shape[0]
    h = _rms_norm(x, gpre_ref[...], NORM_EPS).astype(BF16)
    for c0 in range(0, d_ff, chunk):
        gate = _dot(h, wgu_ref[:, c0:c0 + chunk])
        up = _dot(h, wgu_ref[:, d_ff + c0:d_ff + c0 + chunk])
        act_ref[:, c0:c0 + chunk] = (gate * jax.nn.sigmoid(gate) * up).astype(BF16)
    ffn = _dot(act_ref[...], wdown_ref[...])
    y_ref[...] = x + _rms_norm(ffn, gpost_ref[...], NORM_EPS)


def _ffn(x, gpre, wgu, wdown, gpost, tm, chunk):
    n, d = x.shape
    d_ff = wdown.shape[0]
    assert d_ff % chunk == 0
    row = lambda i: (i, 0)
    fixed = lambda i: (0, 0)
    full = lambda a: pl.BlockSpec(a.shape, fixed)
    return pl.pallas_call(
        functools.partial(_ffn_kernel, chunk=chunk),
        grid=(n // tm,),
        in_specs=[pl.BlockSpec((tm, d), row), full(gpre), full(wgu), full(wdown), full(gpost)],
        out_specs=pl.BlockSpec((tm, d), row),
        out_shape=jax.ShapeDtypeStruct((n, d), F32),
        scratch_shapes=[pltpu.VMEM((tm, d_ff), BF16)],
        compiler_params=_cparams(("parallel",)),
        name="ffn",
    )(x, gpre, wgu, wdown, gpost)


def _lambda_init(layer):
    return 0.8 - 0.6 * math.exp(-0.3 * layer)


def _rotary_tables(pos):
    half = ROPE_DIM // 2
    inv_freq = ROPE_THETA ** (-jnp.arange(0, ROPE_DIM, 2, dtype=F32) / ROPE_DIM)
    ang = pos.astype(F32)[:, None] * inv_freq[None, :]
    cos, sin = jnp.cos(ang), jnp.sin(ang)
    n = pos.shape[0]
    rest = DIFF_DQ - ROPE_DIM
    ones, zeros = jnp.ones((n, rest), F32), jnp.zeros((n, rest), F32)
    zh = jnp.zeros((n, half), F32)
    reps = BRANCH_W // DIFF_DQ
    cos_t = jnp.tile(jnp.concatenate([cos, cos, ones], axis=1), (1, reps))
    sina_t = jnp.tile(jnp.concatenate([-sin, zh, zeros], axis=1), (1, reps))
    sinb_t = jnp.tile(jnp.concatenate([zh, sin, zeros], axis=1), (1, reps))
    return cos_t, sina_t, sinb_t


def _layer_params(l, norm_mix_pre, norm_mix_post, norm_ffn_pre, norm_ffn_post, w_in, b_forget,
                  diff_lambda, diff_subln, w_branch, w_out, w_gate_up, w_down):
    W = BRANCH_W
    d = w_in.shape[1]
    w = w_in[l]
    q_scale = HEAD_DIM ** -0.5
    cols = [w[:, 0:W] * q_scale, w[:, W:3 * W], w[:, 3 * W:4 * W] * q_scale, w[:, 4 * W:9 * W],
            jnp.pad(w[:, 9 * W:9 * W + H_FOX], ((0, 0), (0, F_PAD - H_FOX)))]
    w_c = w_branch[l, 2]
    w_c_pad = jnp.pad(w_c.reshape(H_DIFF, DIFF_DV, d), ((0, 0), (0, LANES - DIFF_DV), (0, 0)))
    row = lambda v: v.reshape(1, -1).astype(F32)
    return dict(
        w_qkv=jnp.concatenate(cols, axis=1).astype(BF16),
        b_f=jnp.pad(b_forget[l], (0, F_PAD - H_FOX)).reshape(1, F_PAD).astype(F32),
        w_gate=w[:, 9 * W + H_FOX:].astype(BF16),
        w_a=w_branch[l, 0].astype(BF16), w_b=w_branch[l, 1].astype(BF16),
        w_c=w_c.astype(BF16), w_c_pad=w_c_pad.reshape(DIFF_PAD_W, d).astype(BF16),
        w_out=w_out[l].astype(BF16), w_gu=w_gate_up[l].astype(BF16), w_down=w_down[l].astype(BF16),
        g_mix_pre=row(norm_mix_pre[l]), g_mix_post=row(norm_mix_post[l]),
        g_ffn_pre=row(norm_ffn_pre[l]), g_ffn_post=row(norm_ffn_post[l]),
        diff_lambda=diff_lambda[l].astype(F32),
        subln=row(jnp.tile(diff_subln[l], H_DIFF)),
        subln_pad=row(jnp.pad(diff_subln[l], (0, LANES - DIFF_DV))),
        lambda_init=_lambda_init(l),
    )


def _strict_lower_ones(n):
    j = lax.broadcasted_iota(jnp.int32, (n, n), 0)
    s = lax.broadcasted_iota(jnp.int32, (n, n), 1)
    return (j > s).astype(BF16)


def _pad_heads(x, groups, width, padded):
    lead = x.shape[:-1]
    x = x.reshape(*lead, groups, width)
    x = jnp.pad(x, [(0, 0)] * len(lead) + [(0, 0), (0, padded - width)])
    return x.reshape(*lead, groups * padded)


def _ffn_chunk(d_ff):
    for chunk in (512, 256, 128):
        if d_ff % chunk == 0:
            return chunk
    return d_ff


def kernel(x_prompt, x_sample, cache_fox_kv, cache_fox_logf, cache_sb_kv, cache_diff_kv, page_table,
           norm_mix_pre, norm_mix_post, norm_ffn_pre, norm_ffn_post, w_in, b_forget, diff_lambda,
           diff_subln, w_branch, w_out, w_gate_up, w_down):
    depth = w_in.shape[0]
    b, t, d = x_prompt.shape
    nb, dec_t, _ = x_sample.shape
    assert dec_t == 1
    n_phys, page = cache_fox_kv.shape[1], cache_fox_kv.shape[2]
    n_pages = page_table.shape[1]
    past_len = n_pages * page
    W = BRANCH_W
    d_ff = w_down.shape[1]

    tq = min(256, t)
    tm = min(512, t)
    assert t % tq == 0 and t % tm == 0 and page == LANES
    pages_per_step = 4 if n_pages % 4 == 0 else 1
    chunk = _ffn_chunk(d_ff)

    params = [_layer_params(l, norm_mix_pre, norm_mix_post, norm_ffn_pre, norm_ffn_post, w_in,
                            b_forget, diff_lambda, diff_subln, w_branch, w_out, w_gate_up, w_down)
              for l in range(depth)]
    tab_p = _rotary_tables(jnp.arange(t))
    tab_s = tuple(jnp.broadcast_to(a, (nb, W)) for a in _rotary_tables(jnp.full((1,), past_len)))
    tri_q = _strict_lower_ones(tq)
    tri_page = _strict_lower_ones(page)

    cache_fox = cache_fox_kv.reshape(depth, n_phys, page, 2 * W)
    cache_sb = cache_sb_kv.reshape(depth, n_phys, page, 2 * W)
    cache_diff = cache_diff_kv.reshape(depth, n_phys, page, 2 * W)
    cache_lf_t = jnp.pad(jnp.swapaxes(cache_fox_logf, 2, 3),
                         ((0, 0), (0, 0), (0, DEC_ROWS - H_FOX), (0, 0)))

    def tail(x, o_a, o_b, o_c, w_c, p, rows):
        x = _merge(x, o_a, o_b, o_c, p["g_mix_pre"], p["w_gate"], p["w_a"], p["w_b"], w_c,
                   p["w_out"], p["g_mix_post"], rows)
        return _ffn(x, p["g_ffn_pre"], p["w_gu"], p["w_down"], p["g_ffn_post"], rows, chunk)

    x = x_prompt.reshape(b * t, d)
    rows_p = []
    for p in params:
        qkv_ab, fox_kv, sb_kv, diff_kv, qkv_c, logf = _inproj(
            x, p["g_mix_pre"], p["w_qkv"], p["b_f"], *tab_p, tm)
        logf6 = logf[:, :H_FOX].reshape(b, t, H_FOX)
        rows_p.append((fox_kv.reshape(b, t, 2, H_FOX, HEAD_DIM), logf6,
                       sb_kv.reshape(b, t, 2, H_SB, HEAD_DIM),
                       diff_kv.reshape(b, t, 2, H_DIFF, DIFF_DV)))
        c_k = _cumsum_time(jnp.pad(jnp.swapaxes(logf6, 1, 2), ((0, 0), (0, 8 - H_FOX), (0, 0))))
        c_q = jnp.swapaxes(c_k, 1, 2)
        qkv_ab3 = qkv_ab.reshape(b, t, 6 * W)
        qkv_c3 = qkv_c.reshape(b, t, 3 * W)
        o_a = _prompt_attention_call(
            _fox_prompt_kernel, "fox_prompt",
            (qkv_ab3, 0, W), (qkv_ab3, W, W), (qkv_ab3, 2 * W, W), (c_q, c_k),
            [pl.BlockSpec((1, tq, 8), lambda bi, i: (bi, i, 0)),
             pl.BlockSpec((1, 8, t), lambda bi, i: (bi, 0, 0))], W, tq)
        o_b = _prompt_attention_call(
            _sb_prompt_kernel, "sb_prompt",
            (qkv_ab3, 3 * W, W), (qkv_ab3, 4 * W, W), (qkv_ab3, 5 * W, W), (tri_q,),
            [pl.BlockSpec(tri_q.shape, lambda bi, i: (0, 0))], W, tq)
        q_c = _pad_heads(qkv_c3[..., 0:W], 2 * H_DIFF, DIFF_DQ, HEAD_DIM)
        k_c = _pad_heads(qkv_c3[..., W:2 * W], 2 * H_DIFF, DIFF_DQ, HEAD_DIM)
        v_c = _pad_heads(qkv_c3[..., 2 * W:3 * W], H_DIFF, DIFF_DV, LANES)
        o_c = _prompt_attention_call(
            functools.partial(_diff_prompt_kernel, lambda_init=p["lambda_init"]), "diff_prompt",
            (q_c, 0, DIFF_PAD_W), (k_c, 0, DIFF_PAD_W), (v_c, 0, DIFF_PAD_W),
            (p["diff_lambda"], p["subln_pad"]),
            [pl.BlockSpec(p["diff_lambda"].shape, lambda bi, i: (0, 0)),
             pl.BlockSpec((1, LANES), lambda bi, i: (0, 0))], DIFF_PAD_W, tq)
        x = tail(x, o_a.reshape(b * t, W), o_b.reshape(b * t, W), o_c.reshape(b * t, DIFF_PAD_W),
                 p["w_c_pad"], p, tm)
    y_prompt = x.reshape(b, t, d)

    x = x_sample.reshape(nb, d)
    rows_s = []
    for l, p in enumerate(params):
        qkv_ab, fox_kv, sb_kv, diff_kv, qkv_c, logf = _inproj(
            x, p["g_mix_pre"], p["w_qkv"], p["b_f"], *tab_s, nb)
        rows_s.append((fox_kv.reshape(nb, 1, 2, H_FOX, HEAD_DIM),
                       logf[:, :H_FOX].reshape(nb, 1, H_FOX),
                       sb_kv.reshape(nb, 1, 2, H_SB, HEAD_DIM),
                       diff_kv.reshape(nb, 1, 2, H_DIFF, DIFF_DV)))
        o = _decode_attention(page_table, l, qkv_ab, qkv_c, logf, p["diff_lambda"], p["subln"],
                              tri_page, cache_fox, cache_lf_t, cache_sb, cache_diff,
                              p["lambda_init"], pages_per_step)
        x = tail(x, o[:, 0:W], o[:, W:2 * W], o[:, 2 * W:3 * W], p["w_c"], p, nb)
    y_sample = x.reshape(nb, 1, d)

    def stack(rows, i):
        return jnp.stack([r[i] for r in rows])

    return (y_prompt, y_sample,
            stack(rows_p, 0), stack(rows_p, 1), stack(rows_p, 2), stack(rows_p, 3),
            stack(rows_s, 0), stack(rows_s, 1), stack(rows_s, 2), stack(rows_s, 3))
```

```python
import functools
import math

import jax
import jax.numpy as jnp
from jax import lax
from jax.experimental import pallas as pl
from jax.experimental.pallas import tpu as pltpu

F32 = jnp.float32
BF16 = jnp.bfloat16

HEAD_DIM = 64
H_FOX = 6
H_SB = 6
H_DIFF = 4
DIFF_DQ = 48
DIFF_DV = 2 * DIFF_DQ
BRANCH_W = 384
N_BRANCH = 3
ROPE_DIM = DIFF_DQ // 4
ROPE_THETA = 500000.0
NORM_EPS = 1e-6
SUBLN_EPS = 1e-5

LANES = 128
HEADS_PER_LANE_BLOCK = LANES // HEAD_DIM
F_PAD = LANES
DIFF_PAD_W = H_DIFF * LANES
NEG_BIG = -1e30
VMEM_LIMIT = 56 * 1024 * 1024
DEC_ROWS = 16
STAGE_LAG = 2


def _cparams(sem):
    return pltpu.CompilerParams(dimension_semantics=sem, vmem_limit_bytes=VMEM_LIMIT)


def _rms_norm(x, gain, eps):
    ms = jnp.mean(x * x, axis=-1, keepdims=True)
    return x * lax.rsqrt(ms + eps) * gain


def _softplus_neg_abs(z):
    return jnp.log1p(jnp.exp(-jnp.abs(z)))


def _bf16_prefix(x):
    bits = lax.bitcast_convert_type(x, jnp.uint32) & jnp.uint32(0xFFFF0000)
    return lax.bitcast_convert_type(bits, F32)


def _split_hi_lo(x):
    hi = _bf16_prefix(x)
    return hi.astype(BF16), (x - hi).astype(BF16)


def _in_group(index, group, width):
    lo = group * width
    return (index >= lo) & (index < lo + width)


def _dot(a, b):
    return jnp.dot(a, b, preferred_element_type=F32)


def _dot_nt(a, b):
    return lax.dot_general(a, b, (((1,), (1,)), ((), ())), preferred_element_type=F32)


def _inproj_kernel(x_ref, gain_ref, w_ref, bf_ref, cos_ref, sina_ref, sinb_ref,
                   qkv_ab_ref, foxkv_ref, sbkv_ref, diffkv_ref, qkv_c_ref, logf_ref):
    W = BRANCH_W
    h = _rms_norm(x_ref[...], gain_ref[...], NORM_EPS).astype(BF16)

    def proj(c):
        return _dot(h, w_ref[:, c * W:(c + 1) * W])

    def rotary(r):
        return (r * cos_ref[...] + pltpu.roll(r, W - ROPE_DIM // 2, 1) * sina_ref[...]
                + pltpu.roll(r, ROPE_DIM // 2, 1) * sinb_ref[...])

    for c, kv_ref in ((0, foxkv_ref), (3, sbkv_ref)):
        qkv_ab_ref[:, c * W:(c + 1) * W] = proj(c).astype(BF16)
        for j in range(2):
            r = proj(c + 1 + j)
            qkv_ab_ref[:, (c + 1 + j) * W:(c + 2 + j) * W] = r.astype(BF16)
            kv_ref[:, j * W:(j + 1) * W] = r

    qc = rotary(proj(6)) * (DIFF_DQ ** -0.5)
    qkv_c_ref[:, 0:W] = qc.astype(BF16)
    kc = rotary(proj(7))
    qkv_c_ref[:, W:2 * W] = kc.astype(BF16)
    diffkv_ref[:, 0:W] = kc
    vc = proj(8)
    qkv_c_ref[:, 2 * W:3 * W] = vc.astype(BF16)
    diffkv_ref[:, W:2 * W] = vc

    f = _dot(h, w_ref[:, 9 * W:9 * W + F_PAD]) + bf_ref[...]
    logf_ref[...] = jnp.minimum(f, 0.0) - _softplus_neg_abs(f)


def _inproj(x, gain, w, bf, cos_t, sina_t, sinb_t, tm):
    n, d = x.shape
    W = BRANCH_W
    n_tab = cos_t.shape[0] // tm
    row = lambda i: (i, 0)
    fixed = lambda i: (0, 0)
    tab = lambda i: (i % n_tab, 0)
    out_shape = (
        jax.ShapeDtypeStruct((n, 6 * W), BF16),
        jax.ShapeDtypeStruct((n, 2 * W), F32),
        jax.ShapeDtypeStruct((n, 2 * W), F32),
        jax.ShapeDtypeStruct((n, 2 * W), F32),
        jax.ShapeDtypeStruct((n, 3 * W), BF16),
        jax.ShapeDtypeStruct((n, F_PAD), F32),
    )
    return pl.pallas_call(
        _inproj_kernel,
        grid=(n // tm,),
        in_specs=[
            pl.BlockSpec((tm, d), row),
            pl.BlockSpec((1, d), fixed),
            pl.BlockSpec(w.shape, fixed),
            pl.BlockSpec((1, F_PAD), fixed),
            pl.BlockSpec((tm, W), tab),
            pl.BlockSpec((tm, W), tab),
            pl.BlockSpec((tm, W), tab),
        ],
        out_specs=(
            pl.BlockSpec((tm, 6 * W), row),
            pl.BlockSpec((tm, 2 * W), row),
            pl.BlockSpec((tm, 2 * W), row),
            pl.BlockSpec((tm, 2 * W), row),
            pl.BlockSpec((tm, 3 * W), row),
            pl.BlockSpec((tm, F_PAD), row),
        ),
        out_shape=out_shape,
        compiler_params=_cparams(("parallel",)),
        name="inproj",
    )(x, gain, w, bf, cos_t, sina_t, sinb_t)


def _cumsum_kernel(x_ref, o_ref):
    x = x_ref[0]
    t = x.shape[-1]
    lane = lax.broadcasted_iota(jnp.int32, x.shape, 1)
    shift = 1
    while shift < t:
        x = x + jnp.where(lane >= shift, pltpu.roll(x, shift, 1), 0.0)
        shift *= 2
    o_ref[0] = x


def _cumsum_time(logf_t):
    b, r, t = logf_t.shape
    blk = pl.BlockSpec((1, r, t), lambda i: (i, 0, 0))
    return pl.pallas_call(
        _cumsum_kernel, grid=(b,), in_specs=[blk], out_specs=blk,
        out_shape=jax.ShapeDtypeStruct(logf_t.shape, F32),
        compiler_params=_cparams(("parallel",)), name="cumsum_logf",
    )(logf_t)


def _lane_block(ref, rows, blk):
    return ref[0, rows, blk * LANES:(blk + 1) * LANES]


def _softmax_update(s, vt_ext, m_ref, acc_ref):
    m = m_ref[...]
    m_new = jnp.maximum(m, jnp.max(s, axis=0, keepdims=True))
    p = jnp.exp(s - m_new)
    acc_ref[...] = jnp.exp(m - m_new) * acc_ref[...] + _dot(vt_ext, p.astype(BF16))
    m_ref[...] = m_new


def _staggered(n_chains, stages, lag):
    state = [None] * n_chains
    for step in range(n_chains + (len(stages) - 1) * lag):
        for k, stage in enumerate(stages):
            c = step - k * lag
            if 0 <= c < n_chains:
                state[c] = stage(c, state[c])


def _key_query_iota(tq):
    key = lax.broadcasted_iota(jnp.int32, (tq, tq), 0)
    query = lax.broadcasted_iota(jnp.int32, (tq, tq), 1)
    return key, query


def _for_key_blocks(i, tq, block, newest_first):
    def body(step, carry):
        kb = (i - 1 - step) if newest_first else step
        block(pl.multiple_of(kb * tq, tq), False)
        return carry

    q0 = pl.multiple_of(i * tq, tq)
    if newest_first:
        block(q0, True)
        lax.fori_loop(0, i, body, 0)
    else:
        lax.fori_loop(0, i, body, 0)
        block(q0, True)


def _fox_prompt_kernel(q_ref, k_ref, vt_ref, o_ref, m_s, acc_s, *, tq):
    i = pl.program_id(1)
    key, query = _key_query_iota(tq)
    causal = key <= query
    for head in range(H_FOX):
        m_s[head] = jnp.full((1, tq), NEG_BIG, F32)
        acc_s[head] = jnp.zeros((LANES, tq), F32)

    def block(k0, diagonal):
        def scores(head, _):
            k = _lane_block(k_ref, pl.ds(k0, tq), head)
            s = _dot_nt(k, _lane_block(q_ref, slice(None), head))
            return jnp.where(causal, s, NEG_BIG) if diagonal else s

        def update(head, s):
            vt = vt_ref[0, head * LANES:(head + 1) * LANES, pl.ds(k0, tq)]
            _softmax_update(s, vt, m_s.at[head], acc_s.at[head])

        _staggered(H_FOX, (scores, update), STAGE_LAG)

    _for_key_blocks(i, tq, block, newest_first=False)
    for blk in range(H_FOX // HEADS_PER_LANE_BLOCK):
        halves = []
        for sub in range(HEADS_PER_LANE_BLOCK):
            a = acc_s[blk * HEADS_PER_LANE_BLOCK + sub]
            halves.append(a[0:HEAD_DIM] / a[HEAD_DIM:HEAD_DIM + 1])
        o_t = jnp.concatenate(halves, axis=0)
        o_ref[0, :, blk * LANES:(blk + 1) * LANES] = o_t.T.astype(o_ref.dtype)


def _sb_prompt_kernel(q_ref, k_ref, vt_ref, tri_ref, o_ref, qh_s, later_s, acc_s, *, tq):
    i = pl.program_id(1)
    lane = lax.broadcasted_iota(jnp.int32, (tq, LANES), 1)
    key, query = _key_query_iota(tq)
    strict = key < query
    for head in range(H_SB):
        blk, sub = divmod(head, HEADS_PER_LANE_BLOCK)
        q_pair = _lane_block(q_ref, slice(None), blk)
        qh_s[head] = jnp.where(_in_group(lane, sub, HEAD_DIM), q_pair, jnp.zeros_like(q_pair))
        later_s[head] = jnp.zeros((1, tq), F32)
        acc_s[head] = jnp.zeros((HEAD_DIM, tq), F32)

    def block(k0, diagonal):
        def logits(head, _):
            k = _lane_block(k_ref, pl.ds(k0, tq), head // HEADS_PER_LANE_BLOCK)
            z = _dot_nt(k, qh_s[head])
            sp = _softplus_neg_abs(z)
            log_hit = jnp.minimum(z, 0.0) - sp
            log_fail = jnp.minimum(-z, 0.0) - sp
            if diagonal:
                log_fail = jnp.where(strict, log_fail, 0.0)
            return log_hit, _split_hi_lo(log_fail), jnp.sum(log_fail, axis=0, keepdims=True)

        def suffix_sums(head, state):
            log_hit, (hi, lo), fail_sum = state
            tri = tri_ref[...]
            return log_hit + (_dot(tri, hi) + _dot(tri, lo)), fail_sum

        def accumulate(head, state):
            log_w, fail_sum = state
            vt = vt_ref[0, head * HEAD_DIM:(head + 1) * HEAD_DIM, pl.ds(k0, tq)]
            w = jnp.exp(log_w + later_s[head])
            if diagonal:
                w = jnp.where(strict, w, 0.0)
            acc_s[head] += _dot(vt, w.astype(BF16))
            later_s[head] += fail_sum

        _staggered(H_SB, (logits, suffix_sums, accumulate), STAGE_LAG)

    _for_key_blocks(i, tq, block, newest_first=True)
    for blk in range(H_SB // HEADS_PER_LANE_BLOCK):
        o_t = jnp.concatenate([acc_s[blk * HEADS_PER_LANE_BLOCK + sub]
                               for sub in range(HEADS_PER_LANE_BLOCK)], axis=0)
        o_ref[0, :, blk * LANES:(blk + 1) * LANES] = o_t.T.astype(o_ref.dtype)


def _diff_lambda(dl, lambda_init):
    l1 = jnp.sum(dl[0:1] * dl[1:2], axis=1, keepdims=True)
    l2 = jnp.sum(dl[2:3] * dl[3:4], axis=1, keepdims=True)
    return jnp.exp(l1) - jnp.exp(l2) + lambda_init


def _diff_prompt_kernel(q_ref, k_ref, vt_ref, dl_ref, subln_ref, o_ref, qh_s, m_s, acc_s,
                        *, tq, lambda_init):
    i = pl.program_id(1)
    lane = lax.broadcasted_iota(jnp.int32, (tq, LANES), 1)
    key, query = _key_query_iota(tq)
    causal = key <= query
    for head in range(H_DIFF):
        q_head = _lane_block(q_ref, slice(None), head)
        for comp in range(2):
            c = 2 * head + comp
            qh_s[c] = jnp.where(_in_group(lane, comp, HEAD_DIM), q_head, jnp.zeros_like(q_head))
            m_s[c] = jnp.full((1, tq), NEG_BIG, F32)
            acc_s[c] = jnp.zeros((LANES, tq), F32)

    def block(k0, diagonal):
        def scores(c, _):
            s = _dot_nt(_lane_block(k_ref, pl.ds(k0, tq), c // 2), qh_s[c])
            return jnp.where(causal, s, NEG_BIG) if diagonal else s

        def update(c, s):
            head = c // 2
            vt = vt_ref[0, head * LANES:(head + 1) * LANES, pl.ds(k0, tq)]
            _softmax_update(s, vt, m_s.at[c], acc_s.at[c])

        _staggered(2 * H_DIFF, (scores, update), STAGE_LAG)

    _for_key_blocks(i, tq, block, newest_first=False)
    lam = _diff_lambda(dl_ref[...], lambda_init)
    gain = subln_ref[0:DIFF_DV, :] * (1.0 - lambda_init)
    for head in range(H_DIFF):
        a0 = acc_s[2 * head]
        a1 = acc_s[2 * head + 1]
        o = (a0[0:DIFF_DV] / a0[DIFF_DV:DIFF_DV + 1]
             - lam * (a1[0:DIFF_DV] / a1[DIFF_DV:DIFF_DV + 1]))
        ms = jnp.sum(o * o, axis=0, keepdims=True) * (1.0 / DIFF_DV)
        o = o * lax.rsqrt(ms + SUBLN_EPS) * gain
        o_t = jnp.concatenate([o, jnp.zeros((LANES - DIFF_DV, tq), F32)], axis=0)
        o_ref[0, :, head * LANES:(head + 1) * LANES] = o_t.T.astype(o_ref.dtype)


def _prompt_attention_call(kernel, name, args, in_specs, out_w, tq, scratch):
    b, t = args[0].shape[0], args[0].shape[1]
    return pl.pallas_call(
        functools.partial(kernel, tq=tq),
        grid=(b, t // tq),
        in_specs=in_specs,
        out_specs=pl.BlockSpec((1, tq, out_w), lambda bi, i: (bi, i, 0)),
        out_shape=jax.ShapeDtypeStruct((b, t, out_w), BF16),
        scratch_shapes=scratch,
        compiler_params=_cparams(("parallel", "arbitrary")),
        name=name,
    )(*args)


def _q_tile_spec(tq, width, col_block=0):
    return pl.BlockSpec((1, tq, width), lambda bi, i: (bi, i, col_block))


def _per_batch_spec(rows, width, col_block=0):
    return pl.BlockSpec((1, rows, width), lambda bi, i: (bi, 0, col_block))


def _const_spec(shape):
    return pl.BlockSpec(shape, lambda bi, i: (0,) * len(shape))


def _fox_prompt(q_ext, k_ext, vt_ext, tq):
    b, t, w = q_ext.shape
    scratch = [pltpu.VMEM((H_FOX, 1, tq), F32), pltpu.VMEM((H_FOX, LANES, tq), F32)]
    return _prompt_attention_call(
        _fox_prompt_kernel, "fox_prompt", (q_ext, k_ext, vt_ext),
        [_q_tile_spec(tq, w), _per_batch_spec(t, w), _per_batch_spec(w, t)], BRANCH_W, tq, scratch)


def _sb_prompt(qkv, vt, tri, tq):
    b, t, _ = qkv.shape
    W = BRANCH_W
    scratch = [pltpu.VMEM((H_SB, tq, LANES), BF16), pltpu.VMEM((H_SB, 1, tq), F32),
               pltpu.VMEM((H_SB, HEAD_DIM, tq), F32)]
    return _prompt_attention_call(
        _sb_prompt_kernel, "sb_prompt", (qkv, qkv, vt, tri),
        [_q_tile_spec(tq, W, 3), _per_batch_spec(t, W, 4), _per_batch_spec(W, t),
         _const_spec(tri.shape)], W, tq, scratch)


def _diff_prompt(q_pad, k_pad, vt_ext, diff_lambda, subln_col, lambda_init, tq):
    b, t, w = q_pad.shape
    chains = 2 * H_DIFF
    scratch = [pltpu.VMEM((chains, tq, LANES), BF16), pltpu.VMEM((chains, 1, tq), F32),
               pltpu.VMEM((chains, LANES, tq), F32)]
    return _prompt_attention_call(
        functools.partial(_diff_prompt_kernel, lambda_init=lambda_init), "diff_prompt",
        (q_pad, k_pad, vt_ext, diff_lambda, subln_col),
        [_q_tile_spec(tq, w), _per_batch_spec(t, w), _per_batch_spec(w, t),
         _const_spec(diff_lambda.shape), _const_spec(subln_col.shape)], w, tq, scratch)


def _decode_kernel(pt_ref, qab_ref, qc_ref, lfnew_ref, dl_ref, subln_ref, tri_ref, *rest,
                   pages_per_step, lambda_init):
    del pt_ref
    P = pages_per_step
    fox_refs = rest[0:P]
    lf_refs = rest[P:2 * P]
    sb_refs = rest[2 * P:3 * P]
    diff_refs = rest[3 * P:4 * P]
    o_ref = rest[4 * P]
    (qa_s, qb_s, qd_s, m_a, l_a, acc_a, carry_a, acc_b, carry_b, m_c, l_c, acc_c) = rest[4 * P + 1:]
    W = BRANCH_W
    R = DEC_ROWS
    page = LANES
    step = pl.program_id(1)
    n_steps = pl.num_programs(1)
    lane = lax.broadcasted_iota(jnp.int32, (R, W), 1)
    rowi = lax.broadcasted_iota(jnp.int32, (R, W), 0)
    head64 = _in_group(lane, rowi, HEAD_DIM)
    comp48 = _in_group(lane, rowi, DIFF_DQ)
    tri_ones = tri_ref[...]

    def bcast(x):
        return jnp.broadcast_to(x, (R, W))

    @pl.when(step == 0)
    def _init():
        qab = qab_ref[0]
        qc = qc_ref[0]
        zero = jnp.zeros((R, W), F32)
        qa = jnp.where(head64, bcast(qab[:, 0:W].astype(F32)), zero)
        qb = jnp.where(head64, bcast(qab[:, 3 * W:4 * W].astype(F32)), zero)
        qd = jnp.where(comp48, bcast(qc[:, 0:W].astype(F32)), zero)
        qa_s[...] = qa.astype(BF16)
        qb_s[...] = qb.astype(BF16)
        qd_s[...] = qd.astype(BF16)
        ka = bcast(qab[:, W:2 * W].astype(F32))
        m_a[...] = jnp.broadcast_to(jnp.sum(qa * ka, axis=1, keepdims=True), (R, LANES))
        l_a[...] = jnp.ones((R, LANES), F32)
        acc_a[...] = bcast(qab[:, 2 * W:3 * W].astype(F32))
        lane_f = lax.broadcasted_iota(jnp.int32, (R, F_PAD), 1)
        row_f = lax.broadcasted_iota(jnp.int32, (R, F_PAD), 0)
        lf_col = jnp.sum(jnp.where(lane_f == row_f, jnp.broadcast_to(lfnew_ref[0], (R, F_PAD)), 0.0),
                         axis=1, keepdims=True)
        carry_a[...] = jnp.broadcast_to(lf_col, (R, LANES))
        acc_b[...] = jnp.zeros((R, W), F32)
        carry_b[...] = jnp.zeros((R, LANES), F32)
        kd = bcast(qc[:, W:2 * W].astype(F32))
        m_c[...] = jnp.broadcast_to(jnp.sum(qd * kd, axis=1, keepdims=True), (R, LANES))
        l_c[...] = jnp.ones((R, LANES), F32)
        acc_c[...] = bcast(qc[:, 2 * W:3 * W].astype(F32))

    def keys(ref):
        return ref[0:W, :].astype(BF16)

    def values(ref):
        return ref[W:2 * W, :].astype(BF16)

    def rows(x, j):
        return x[j * R:(j + 1) * R]

    def softmax_weights(s_parts, m_ref, l_ref):
        m = m_ref[...]
        block_max = s_parts[0]
        for s in s_parts[1:]:
            block_max = jnp.maximum(block_max, s)
        m_new = jnp.maximum(m, jnp.max(block_max, axis=1, keepdims=True))
        alpha = jnp.exp(m - m_new)
        p_parts = [jnp.exp(s - m_new) for s in s_parts]
        p_sum = p_parts[0]
        for p in p_parts[1:]:
            p_sum = p_sum + p
        l_ref[...] = alpha * l_ref[...] + jnp.sum(p_sum, axis=1, keepdims=True)
        m_ref[...] = m_new
        return alpha, p_parts

    def weighted_values(w_parts, v_refs):
        pv = None
        for j in range(P):
            term = _dot_nt(w_parts[j].astype(BF16), values(v_refs[j]))
            pv = term if pv is None else pv + term
        return pv

    z_a = [_dot(qa_s[...], keys(fox_refs[j])) for j in range(P)]
    z_b = jnp.concatenate([_dot(qb_s[...], keys(sb_refs[j])) for j in range(P)], axis=0)
    z_c = [_dot(qd_s[...], keys(diff_refs[j])) for j in range(P)]

    sp = _softplus_neg_abs(z_b)
    log_hit = jnp.minimum(z_b, 0.0) - sp
    log_fail = jnp.minimum(-z_b, 0.0) - sp
    log_forget = jnp.concatenate([lf_refs[j][...] for j in range(P)], axis=0)
    hi, lo = _split_hi_lo(jnp.concatenate([log_fail, log_forget], axis=0))
    sums = _dot(hi, tri_ones) + _dot(lo, tri_ones)
    suffix_b, total_b = sums[0:P * R, 0:page], sums[0:P * R, page:2 * page]
    suffix_a, total_a = sums[P * R:2 * P * R, 0:page], sums[P * R:2 * P * R, page:2 * page]

    later_a = carry_a[...]
    later_b = carry_b[...]
    s_a, w_b = [], []
    for j in range(P):
        s_a.append(z_a[j] + rows(suffix_a, j) + later_a)
        w_b.append(jnp.exp(rows(log_hit, j) + rows(suffix_b, j) + later_b))
        later_a = later_a + rows(total_a, j)
        later_b = later_b + rows(total_b, j)
    carry_a[...] = later_a
    carry_b[...] = later_b

    alpha_a, p_a = softmax_weights(s_a, m_a, l_a)
    alpha_c, p_c = softmax_weights(z_c, m_c, l_c)
    widen = lambda a: jnp.tile(a, (1, W // LANES))
    acc_a[...] = widen(alpha_a) * acc_a[...] + weighted_values(p_a, fox_refs)
    acc_b[...] += weighted_values(w_b, sb_refs)
    acc_c[...] = widen(alpha_c) * acc_c[...] + weighted_values(p_c, diff_refs)

    @pl.when(step == n_steps - 1)
    def _finish():
        zero = jnp.zeros((R, W), F32)
        o_a = jnp.sum(jnp.where(head64, acc_a[...] / l_a[:, 0:1], zero), axis=0, keepdims=True)
        o_b = jnp.sum(jnp.where(head64, acc_b[...], zero), axis=0, keepdims=True)
        n_c = acc_c[...] / l_c[:, 0:1]
        odd_row = (rowi & 1) == 1
        head_lo = (rowi - (rowi & 1)) * DIFF_DQ
        own_head = (lane >= head_lo) & (lane < head_lo + DIFF_DV)
        first = jnp.sum(jnp.where(own_head & ~odd_row, n_c, zero), axis=0, keepdims=True)
        second = jnp.sum(jnp.where(own_head & odd_row, n_c, zero), axis=0, keepdims=True)
        o_c = first - _diff_lambda(dl_ref[...], lambda_init) * second
        lane1 = lax.broadcasted_iota(jnp.int32, (1, W), 1)
        inv = jnp.zeros((1, W), F32)
        for head in range(H_DIFF):
            in_head = _in_group(lane1, head, DIFF_DV)
            ms = jnp.sum(jnp.where(in_head, o_c * o_c, 0.0), axis=1, keepdims=True) * (1.0 / DIFF_DV)
            inv = jnp.where(in_head, lax.rsqrt(ms + SUBLN_EPS), inv)
        o_c = o_c * inv * subln_ref[...] * (1.0 - lambda_init)
        o_ref[0, :, 0:W] = o_a.astype(o_ref.dtype)
        o_ref[0, :, W:2 * W] = o_b.astype(o_ref.dtype)
        o_ref[0, :, 2 * W:3 * W] = o_c.astype(o_ref.dtype)


def _decode_attention(page_table, layer, qkv_ab, qkv_c, logf_new, diff_lambda, subln_row, tri_ones,
                      cache_fox, cache_lf_t, cache_sb, cache_diff, lambda_init, pages_per_step):
    nb, n_pages = page_table.shape
    P = pages_per_step
    assert n_pages % P == 0
    n_steps = n_pages // P
    page = cache_fox.shape[3]
    W = BRANCH_W
    R = DEC_ROWS

    def page_spec(shape, j):
        def index(b, s, pt):
            return (layer, pt[b, n_pages - 1 - (s * P + j)], 0, 0)
        return pl.BlockSpec((None, None) + shape, index)

    per_seq = lambda b, s, pt: (b, 0, 0)
    fixed = lambda b, s, pt: (0, 0)
    in_specs = [
        pl.BlockSpec((1, 1, 6 * W), per_seq),
        pl.BlockSpec((1, 1, 3 * W), per_seq),
        pl.BlockSpec((1, 1, F_PAD), per_seq),
        pl.BlockSpec(diff_lambda.shape, fixed),
        pl.BlockSpec((1, W), fixed),
        pl.BlockSpec(tri_ones.shape, fixed),
    ]
    in_specs += [page_spec((2 * W, page), j) for j in range(P)]
    in_specs += [page_spec((R, page), j) for j in range(P)]
    in_specs += [page_spec((2 * W, page), j) for j in range(P)]
    in_specs += [page_spec((2 * W, page), j) for j in range(P)]
    scratch = [pltpu.VMEM((R, W), BF16)] * 3 + [
        pltpu.VMEM((R, LANES), F32), pltpu.VMEM((R, LANES), F32), pltpu.VMEM((R, W), F32),
        pltpu.VMEM((R, LANES), F32),
        pltpu.VMEM((R, W), F32), pltpu.VMEM((R, LANES), F32),
        pltpu.VMEM((R, LANES), F32), pltpu.VMEM((R, LANES), F32), pltpu.VMEM((R, W), F32),
    ]
    grid_spec = pltpu.PrefetchScalarGridSpec(
        num_scalar_prefetch=1, grid=(nb, n_steps), in_specs=in_specs,
        out_specs=pl.BlockSpec((1, 1, 3 * W), per_seq), scratch_shapes=scratch)
    out = pl.pallas_call(
        functools.partial(_decode_kernel, pages_per_step=P, lambda_init=lambda_init),
        grid_spec=grid_spec,
        out_shape=jax.ShapeDtypeStruct((nb, 1, 3 * W), BF16),
        compiler_params=_cparams(("parallel", "arbitrary")),
        name="decode_attention",
    )(page_table, qkv_ab.reshape(nb, 1, 6 * W), qkv_c.reshape(nb, 1, 3 * W),
      logf_new.reshape(nb, 1, F_PAD), diff_lambda, subln_row, tri_ones,
      *([cache_fox] * P), *([cache_lf_t] * P), *([cache_sb] * P), *([cache_diff] * P))
    return out.reshape(nb, 3 * W)


def _merge_kernel(x_ref, oa_ref, ob_ref, oc_ref, gpre_ref, wg_ref, wa_ref, wb_ref, wc_ref,
                  wout_ref, gpost_ref, y_ref):
    x = x_ref[...]
    d = x.shape[1]
    h = _rms_norm(x, gpre_ref[...], NORM_EPS).astype(BF16)
    merged = None
    for n, (o_ref, w_ref) in enumerate(((oa_ref, wa_ref), (ob_ref, wb_ref), (oc_ref, wc_ref))):
        gate = jax.nn.sigmoid(_dot(h, wg_ref[:, n * d:(n + 1) * d]))
        term = gate * _dot(o_ref[...], w_ref[...])
        merged = term if merged is None else merged + term
    mix = _dot(merged.astype(BF16), wout_ref[...])
    y_ref[...] = x + _rms_norm(mix, gpost_ref[...], NORM_EPS)


def _merge(x, o_a, o_b, o_c, gpre, wg, wa, wb, wc, wout, gpost, tm):
    n, d = x.shape
    row = lambda i: (i, 0)
    fixed = lambda i: (0, 0)
    full = lambda a: pl.BlockSpec(a.shape, fixed)
    return pl.pallas_call(
        _merge_kernel,
        grid=(n // tm,),
        in_specs=[pl.BlockSpec((tm, d), row), pl.BlockSpec((tm, o_a.shape[1]), row),
                  pl.BlockSpec((tm, o_b.shape[1]), row), pl.BlockSpec((tm, o_c.shape[1]), row),
                  full(gpre), full(wg), full(wa), full(wb), full(wc), full(wout), full(gpost)],
        out_specs=pl.BlockSpec((tm, d), row),
        out_shape=jax.ShapeDtypeStruct((n, d), F32),
        compiler_params=_cparams(("parallel",)),
        name="merge",
    )(x, o_a, o_b, o_c, gpre, wg, wa, wb, wc, wout, gpost)


def _ffn_kernel(x_ref, gpre_ref, wgu_ref, wdown_ref, gpost_ref, y_ref, act_ref, *, chunk):
    x = x_ref[...]
    d_ff = wdown_ref.shape[0]
    h = _rms_norm(x, gpre_ref[...], NORM_EPS).astype(BF16)
    for c0 in range(0, d_ff, chunk):
        gate = _dot(h, wgu_ref[:, c0:c0 + chunk])
        up = _dot(h, wgu_ref[:, d_ff + c0:d_ff + c0 + chunk])
        act_ref[:, c0:c0 + chunk] = (gate * jax.nn.sigmoid(gate) * up).astype(BF16)
    ffn = _dot(act_ref[...], wdown_ref[...])
    y_ref[...] = x + _rms_norm(ffn, gpost_ref[...], NORM_EPS)


def _ffn(x, gpre, wgu, wdown, gpost, tm, chunk):
    n, d = x.shape
    d_ff = wdown.shape[0]
    assert d_ff % chunk == 0
    row = lambda i: (i, 0)
    fixed = lambda i: (0, 0)
    full = lambda a: pl.BlockSpec(a.shape, fixed)
    return pl.pallas_call(
        functools.partial(_ffn_kernel, chunk=chunk),
        grid=(n // tm,),
        in_specs=[pl.BlockSpec((tm, d), row), full(gpre), full(wgu), full(wdown), full(gpost)],
        out_specs=pl.BlockSpec((tm, d), row),
        out_shape=jax.ShapeDtypeStruct((n, d), F32),
        scratch_shapes=[pltpu.VMEM((tm, d_ff), BF16)],
        compiler_params=_cparams(("parallel",)),
        name="ffn",
    )(x, gpre, wgu, wdown, gpost)


def _lambda_init(layer):
    return 0.8 - 0.6 * math.exp(-0.3 * layer)


def _rotary_tables(pos):
    half = ROPE_DIM // 2
    inv_freq = ROPE_THETA ** (-jnp.arange(0, ROPE_DIM, 2, dtype=F32) / ROPE_DIM)
    ang = pos.astype(F32)[:, None] * inv_freq[None, :]
    cos, sin = jnp.cos(ang), jnp.sin(ang)
    n = pos.shape[0]
    rest = DIFF_DQ - ROPE_DIM
    ones, zeros = jnp.ones((n, rest), F32), jnp.zeros((n, rest), F32)
    zh = jnp.zeros((n, half), F32)
    reps = BRANCH_W // DIFF_DQ
    cos_t = jnp.tile(jnp.concatenate([cos, cos, ones], axis=1), (1, reps))
    sina_t = jnp.tile(jnp.concatenate([-sin, zh, zeros], axis=1), (1, reps))
    sinb_t = jnp.tile(jnp.concatenate([zh, sin, zeros], axis=1), (1, reps))
    return cos_t, sina_t, sinb_t


def _layer_params(l, norm_mix_pre, norm_mix_post, norm_ffn_pre, norm_ffn_post, w_in, b_forget,
                  diff_lambda, diff_subln, w_branch, w_out, w_gate_up, w_down):
    W = BRANCH_W
    d = w_in.shape[1]
    w = w_in[l]
    q_scale = HEAD_DIM ** -0.5
    cols = [w[:, 0:W] * q_scale, w[:, W:3 * W], w[:, 3 * W:4 * W] * q_scale, w[:, 4 * W:9 * W],
            jnp.pad(w[:, 9 * W:9 * W + H_FOX], ((0, 0), (0, F_PAD - H_FOX)))]
    w_c = w_branch[l, 2]
    w_c_pad = jnp.pad(w_c.reshape(H_DIFF, DIFF_DV, d), ((0, 0), (0, LANES - DIFF_DV), (0, 0)))
    row = lambda v: v.reshape(1, -1).astype(F32)
    return dict(
        w_qkv=jnp.concatenate(cols, axis=1).astype(BF16),
        b_f=jnp.pad(b_forget[l], (0, F_PAD - H_FOX)).reshape(1, F_PAD).astype(F32),
        w_gate=w[:, 9 * W + H_FOX:].astype(BF16),
        w_a=w_branch[l, 0].astype(BF16), w_b=w_branch[l, 1].astype(BF16),
        w_c=w_c.astype(BF16), w_c_pad=w_c_pad.reshape(DIFF_PAD_W, d).astype(BF16),
        w_out=w_out[l].astype(BF16), w_gu=w_gate_up[l].astype(BF16), w_down=w_down[l].astype(BF16),
        g_mix_pre=row(norm_mix_pre[l]), g_mix_post=row(norm_mix_post[l]),
        g_ffn_pre=row(norm_ffn_pre[l]), g_ffn_post=row(norm_ffn_post[l]),
        diff_lambda=diff_lambda[l].astype(F32),
        subln=row(jnp.tile(diff_subln[l], H_DIFF)),
        subln_col=jnp.pad(diff_subln[l], (0, LANES - DIFF_DV)).reshape(LANES, 1).astype(F32),
        lambda_init=_lambda_init(l),
    )


def _strict_lower_ones(n):
    j = lax.broadcasted_iota(jnp.int32, (n, n), 0)
    s = lax.broadcasted_iota(jnp.int32, (n, n), 1)
    return (j > s).astype(BF16)


def _pad_heads(x, groups, width, padded, fill=0.0):
    lead = x.shape[:-1]
    x = x.reshape(*lead, groups, width)
    x = jnp.pad(x, [(0, 0)] * len(lead) + [(0, 0), (0, padded - width)], constant_values=fill)
    return x.reshape(*lead, groups * padded)


def _fox_decay_operands(q, k, c):
    b, t, _ = q.shape
    hi = _bf16_prefix(c)
    mid = _bf16_prefix(c - hi)
    lo = c - hi - mid
    pieces = jnp.stack([hi, mid, lo], axis=-1).astype(BF16)
    ones = jnp.ones_like(pieces)
    fill = jnp.zeros((b, t, H_FOX, LANES - HEAD_DIM - 6), BF16)
    q_ext = jnp.concatenate([q.reshape(b, t, H_FOX, HEAD_DIM), pieces, -ones, fill], axis=-1)
    k_ext = jnp.concatenate([k.reshape(b, t, H_FOX, HEAD_DIM), ones, pieces, fill], axis=-1)
    return q_ext.reshape(b, t, H_FOX * LANES), k_ext.reshape(b, t, H_FOX * LANES)


def _page_major_cache(cache):
    depth, n_phys, page = cache.shape[:3]
    return jnp.transpose(cache, (0, 1, 3, 4, 5, 2)).reshape(depth, n_phys, -1, page)


def _ffn_chunk(d_ff):
    for chunk in (512, 256, 128):
        if d_ff % chunk == 0:
            return chunk
    return d_ff


def _pages_per_step(n_pages):
    for p in (8, 4, 2):
        if n_pages % p == 0:
            return p
    return 1


def kernel(x_prompt, x_sample, cache_fox_kv, cache_fox_logf, cache_sb_kv, cache_diff_kv, page_table,
           norm_mix_pre, norm_mix_post, norm_ffn_pre, norm_ffn_post, w_in, b_forget, diff_lambda,
           diff_subln, w_branch, w_out, w_gate_up, w_down):
    depth = w_in.shape[0]
    b, t, d = x_prompt.shape
    nb, dec_t, _ = x_sample.shape
    assert dec_t == 1
    page = cache_fox_kv.shape[2]
    n_pages = page_table.shape[1]
    past_len = n_pages * page
    W = BRANCH_W
    d_ff = w_down.shape[1]

    tq = min(256, t)
    tm = min(512, t)
    assert t % tq == 0 and t % tm == 0 and page == LANES
    pages_per_step = _pages_per_step(n_pages)
    chunk = _ffn_chunk(d_ff)

    params = [_layer_params(l, norm_mix_pre, norm_mix_post, norm_ffn_pre, norm_ffn_post, w_in,
                            b_forget, diff_lambda, diff_subln, w_branch, w_out, w_gate_up, w_down)
              for l in range(depth)]
    tab_p = _rotary_tables(jnp.arange(t))
    tab_s = tuple(jnp.broadcast_to(a, (nb, W)) for a in _rotary_tables(jnp.full((1,), past_len)))
    tri_q = _strict_lower_ones(tq).T
    tri_ones_page = jnp.concatenate([_strict_lower_ones(page), jnp.ones((page, page), BF16)], axis=1)

    cache_fox = _page_major_cache(cache_fox_kv)
    cache_sb = _page_major_cache(cache_sb_kv)
    cache_diff = _page_major_cache(cache_diff_kv)
    cache_lf_t = jnp.pad(jnp.swapaxes(cache_fox_logf, 2, 3),
                         ((0, 0), (0, 0), (0, DEC_ROWS - H_FOX), (0, 0)))

    def tail(x, o_a, o_b, o_c, w_c, p, rows):
        x = _merge(x, o_a, o_b, o_c, p["g_mix_pre"], p["w_gate"], p["w_a"], p["w_b"], w_c,
                   p["w_out"], p["g_mix_post"], rows)
        return _ffn(x, p["g_ffn_pre"], p["w_gu"], p["w_down"], p["g_ffn_post"], rows, chunk)

    x = x_prompt.reshape(b * t, d)
    rows_p = []
    for p in params:
        qkv_ab, fox_kv, sb_kv, diff_kv, qkv_c, logf = _inproj(
            x, p["g_mix_pre"], p["w_qkv"], p["b_f"], *tab_p, tm)
        logf6 = logf[:, :H_FOX].reshape(b, t, H_FOX)
        rows_p.append((fox_kv.reshape(b, t, 2, H_FOX, HEAD_DIM), logf6,
                       sb_kv.reshape(b, t, 2, H_SB, HEAD_DIM),
                       diff_kv.reshape(b, t, 2, H_DIFF, DIFF_DV)))
        c_t = _cumsum_time(jnp.pad(jnp.swapaxes(logf6, 1, 2), ((0, 0), (0, 8 - H_FOX), (0, 0))))
        c = jnp.swapaxes(c_t, 1, 2)[..., :H_FOX]
        qkv_ab3 = qkv_ab.reshape(b, t, 6 * W)
        qkv_c3 = qkv_c.reshape(b, t, 3 * W)
        q_ext, k_ext = _fox_decay_operands(qkv_ab3[..., 0:W], qkv_ab3[..., W:2 * W], c)
        vt_a = jnp.swapaxes(_pad_heads(qkv_ab3[..., 2 * W:3 * W], H_FOX, HEAD_DIM, LANES, 1.0), 1, 2)
        o_a = _fox_prompt(q_ext, k_ext, vt_a, tq)
        vt_b = jnp.swapaxes(qkv_ab3[..., 5 * W:6 * W], 1, 2)
        o_b = _sb_prompt(qkv_ab3, vt_b, tri_q, tq)
        q_c = _pad_heads(qkv_c3[..., 0:W], 2 * H_DIFF, DIFF_DQ, HEAD_DIM)
        k_c = _pad_heads(qkv_c3[..., W:2 * W], 2 * H_DIFF, DIFF_DQ, HEAD_DIM)
        vt_c = jnp.swapaxes(_pad_heads(qkv_c3[..., 2 * W:3 * W], H_DIFF, DIFF_DV, LANES, 1.0), 1, 2)
        o_c = _diff_prompt(q_c, k_c, vt_c, p["diff_lambda"], p["subln_col"], p["lambda_init"], tq)
        x = tail(x, o_a.reshape(b * t, W), o_b.reshape(b * t, W), o_c.reshape(b * t, DIFF_PAD_W),
                 p["w_c_pad"], p, tm)
    y_prompt = x.reshape(b, t, d)

    x = x_sample.reshape(nb, d)
    rows_s = []
    for l, p in enumerate(params):
        qkv_ab, fox_kv, sb_kv, diff_kv, qkv_c, logf = _inproj(
            x, p["g_mix_pre"], p["w_qkv"], p["b_f"], *tab_s, nb)
        rows_s.append((fox_kv.reshape(nb, 1, 2, H_FOX, HEAD_DIM),
                       logf[:, :H_FOX].reshape(nb, 1, H_FOX),
                       sb_kv.reshape(nb, 1, 2, H_SB, HEAD_DIM),
                       diff_kv.reshape(nb, 1, 2, H_DIFF, DIFF_DV)))
        o = _decode_attention(page_table, l, qkv_ab, qkv_c, logf, p["diff_lambda"], p["subln"],
                              tri_ones_page, cache_fox, cache_lf_t, cache_sb, cache_diff,
                              p["lambda_init"], pages_per_step)
        x = tail(x, o[:, 0:W], o[:, W:2 * W], o[:, 2 * W:3 * W], p["w_c"], p, nb)
    y_sample = x.reshape(nb, 1, d)

    def stack(rows, i):
        return jnp.stack([r[i] for r in rows])

    return (y_prompt, y_sample,
            stack(rows_p, 0), stack(rows_p, 1), stack(rows_p, 2), stack(rows_p, 3),
            stack(rows_s, 0), stack(rows_s, 1), stack(rows_s, 2), stack(rows_s, 3))
```

```python
import functools
import math

import jax
import jax.numpy as jnp
import numpy as np
from jax import lax
from jax.experimental import pallas as pl
from jax.experimental.pallas import tpu as pltpu

F32 = jnp.float32
BF16 = jnp.bfloat16

HEAD_DIM = 64
H_FOX = 6
H_SB = 6
H_DIFF = 4
DIFF_DQ = 48
DIFF_DV = 2 * DIFF_DQ
BRANCH_W = 384
N_BRANCH = 3
ROPE_DIM = DIFF_DQ // 4
ROPE_THETA = 500000.0
NORM_EPS = 1e-6
SUBLN_EPS = 1e-5

LANES = 128
HEADS_PER_LANE_BLOCK = LANES // HEAD_DIM
F_PAD = LANES
DIFF_PAD_W = H_DIFF * LANES
NEG_BIG = -1e30
VMEM_LIMIT = 56 * 1024 * 1024
DEC_ROWS = 16
STAGE_LAG = 2
PREP_ROWS = 512


def _cparams(sem):
    return pltpu.CompilerParams(dimension_semantics=sem, vmem_limit_bytes=VMEM_LIMIT)


def _rms_norm(x, gain, eps):
    ms = jnp.mean(x * x, axis=-1, keepdims=True)
    return x * lax.rsqrt(ms + eps) * gain


def _softplus_neg_abs(z):
    return jnp.log1p(jnp.exp(-jnp.abs(z)))


def _log_hit_and_fail(z):
    log_fail = jnp.minimum(-z, 0.0) - jnp.log(1.0 + jnp.exp(-jnp.abs(z)))
    return log_fail + z, log_fail


def _bf16_prefix(x):
    bits = lax.bitcast_convert_type(x, jnp.uint32) & jnp.uint32(0xFFFF0000)
    return lax.bitcast_convert_type(bits, F32)


def _split_hi_lo(x):
    hi = _bf16_prefix(x)
    return hi.astype(BF16), (x - hi).astype(BF16)


def _in_group(index, group, width):
    lo = group * width
    return (index >= lo) & (index < lo + width)


def _dot(a, b):
    return jnp.dot(a, b, preferred_element_type=F32)


def _dot_nt(a, b):
    return lax.dot_general(a, b, (((1,), (1,)), ((), ())), preferred_element_type=F32)


def _inproj_kernel(x_ref, gain_ref, w_ref, bf_ref, cos_ref, sina_ref, sinb_ref, *rest):
    qkv_ab_ref, foxkv_ref, sbkv_ref, diffkv_ref, qkv_c_ref, logf_ref = rest[-6:]
    W = BRANCH_W
    h = _rms_norm(x_ref[...], gain_ref[...], NORM_EPS).astype(BF16)

    def proj(c):
        return _dot(h, w_ref[:, c * W:(c + 1) * W])

    def rotary(r):
        return (r * cos_ref[...] + pltpu.roll(r, W - ROPE_DIM // 2, 1) * sina_ref[...]
                + pltpu.roll(r, ROPE_DIM // 2, 1) * sinb_ref[...])

    for c, kv_ref in ((0, foxkv_ref), (3, sbkv_ref)):
        qkv_ab_ref[:, c * W:(c + 1) * W] = proj(c).astype(BF16)
        for j in range(2):
            r = proj(c + 1 + j)
            qkv_ab_ref[:, (c + 1 + j) * W:(c + 2 + j) * W] = r.astype(BF16)
            kv_ref[:, j * W:(j + 1) * W] = r

    qc = rotary(proj(6)) * (DIFF_DQ ** -0.5)
    qkv_c_ref[:, 0:W] = qc.astype(BF16)
    kc = rotary(proj(7))
    qkv_c_ref[:, W:2 * W] = kc.astype(BF16)
    diffkv_ref[:, 0:W] = kc
    vc = proj(8)
    qkv_c_ref[:, 2 * W:3 * W] = vc.astype(BF16)
    diffkv_ref[:, W:2 * W] = vc

    f = _dot(h, w_ref[:, 9 * W:9 * W + F_PAD]) + bf_ref[...]
    logf_ref[...] = jnp.minimum(f, 0.0) - _softplus_neg_abs(f)


def _inproj(x, gain, w, bf, cos_t, sina_t, sinb_t, tm, layer, depth, rows_so_far):
    n, d = x.shape
    W = BRANCH_W
    n_tab = cos_t.shape[0] // tm
    row = lambda i: (i, 0)
    fixed = lambda i: (0, 0)
    tab = lambda i: (i % n_tab, 0)
    slab = pl.BlockSpec((None, tm, 2 * W), lambda i: (layer, i, 0))
    rows_shape = jax.ShapeDtypeStruct((depth, n, 2 * W), F32)
    out_shape = (
        jax.ShapeDtypeStruct((n, 6 * W), BF16), rows_shape, rows_shape, rows_shape,
        jax.ShapeDtypeStruct((n, 3 * W), BF16),
        jax.ShapeDtypeStruct((n, F_PAD), F32),
    )
    in_specs = [
        pl.BlockSpec((tm, d), row),
        pl.BlockSpec((1, d), fixed),
        pl.BlockSpec(w.shape, fixed),
        pl.BlockSpec((1, F_PAD), fixed),
        pl.BlockSpec((tm, W), tab),
        pl.BlockSpec((tm, W), tab),
        pl.BlockSpec((tm, W), tab),
    ]
    args = [x, gain, w, bf, cos_t, sina_t, sinb_t]
    aliases = {}
    if rows_so_far is not None:
        for j, buf in enumerate(rows_so_far):
            aliases[len(args)] = 1 + j
            args.append(buf)
            in_specs.append(pl.BlockSpec(memory_space=pl.ANY))
    qkv_ab, fox_rows, sb_rows, diff_rows, qkv_c, logf = pl.pallas_call(
        _inproj_kernel,
        grid=(n // tm,),
        in_specs=in_specs,
        out_specs=(pl.BlockSpec((tm, 6 * W), row), slab, slab, slab,
                   pl.BlockSpec((tm, 3 * W), row), pl.BlockSpec((tm, F_PAD), row)),
        out_shape=out_shape,
        input_output_aliases=aliases,
        compiler_params=_cparams(("parallel",)),
        name="inproj",
    )(*args)
    return qkv_ab, (fox_rows, sb_rows, diff_rows), qkv_c, logf


def _cumsum_kernel(x_ref, o_ref):
    x = x_ref[0]
    t = x.shape[-1]
    lane = lax.broadcasted_iota(jnp.int32, x.shape, 1)
    shift = 1
    while shift < t:
        x = x + jnp.where(lane >= shift, pltpu.roll(x, shift, 1), 0.0)
        shift *= 2
    o_ref[0] = x


def _cumsum_time(logf_t):
    b, r, t = logf_t.shape
    blk = pl.BlockSpec((1, r, t), lambda i: (i, 0, 0))
    return pl.pallas_call(
        _cumsum_kernel, grid=(b,), in_specs=[blk], out_specs=blk,
        out_shape=jax.ShapeDtypeStruct(logf_t.shape, F32),
        compiler_params=_cparams(("parallel",)), name="cumsum_logf",
    )(logf_t)


def _lane_block(ref, rows, blk):
    return ref[0, rows, blk * LANES:(blk + 1) * LANES]


def _softmax_update(s, vt_ext, m_ref, acc_ref):
    m = m_ref[...]
    m_new = jnp.maximum(m, jnp.max(s, axis=0, keepdims=True))
    p = jnp.exp(s - m_new)
    acc_ref[...] = jnp.exp(m - m_new) * acc_ref[...] + _dot(vt_ext, p.astype(BF16))
    m_ref[...] = m_new


def _staggered(n_chains, stages, lag):
    state = [None] * n_chains
    for step in range(n_chains + (len(stages) - 1) * lag):
        for k, stage in enumerate(stages):
            c = step - k * lag
            if 0 <= c < n_chains:
                state[c] = stage(c, state[c])


def _key_query_iota(tq):
    key = lax.broadcasted_iota(jnp.int32, (tq, tq), 0)
    query = lax.broadcasted_iota(jnp.int32, (tq, tq), 1)
    return key, query


def _for_key_blocks(i, tq, block, newest_first):
    def body(step, carry):
        kb = (i - 1 - step) if newest_first else step
        block(pl.multiple_of(kb * tq, tq), False)
        return carry

    q0 = pl.multiple_of(i * tq, tq)
    if newest_first:
        block(q0, True)
        lax.fori_loop(0, i, body, 0)
    else:
        lax.fori_loop(0, i, body, 0)
        block(q0, True)


def _fox_prompt_kernel(q_ref, k_ref, v_ref, cq_ref, ck_ref, place_q_ref, place_k_ref, place_vt_ref,
                       o_ref, qx_s, kx_s, vt_s, m_s, acc_s, *, tq):
    i = pl.program_id(1)
    t = k_ref.shape[1]
    key, query = _key_query_iota(tq)
    causal = key <= query

    @pl.when(i == 0)
    def _per_batch_row():
        for r0 in range(0, t, PREP_ROWS):
            rows = slice(r0, min(r0 + PREP_ROWS, t))
            ck = ck_ref[0, rows, :]
            kx = _dot(jnp.concatenate([k_ref[0, rows, :], ck], axis=1), place_k_ref[...])
            kx_s[rows, :] = kx.astype(BF16)
            vt = _dot_nt(place_vt_ref[...], jnp.concatenate([v_ref[0, rows, :], ck], axis=1))
            vt_s[:, rows] = vt.astype(BF16)

    qx = _dot(jnp.concatenate([q_ref[0], cq_ref[0]], axis=1), place_q_ref[...])
    qx_s[...] = qx.astype(BF16)
    for head in range(H_FOX):
        m_s[head] = jnp.full((1, tq), NEG_BIG, F32)
        acc_s[head] = jnp.zeros((LANES, tq), F32)

    def block(k0, diagonal):
        def scores(head, _):
            k = kx_s[pl.ds(k0, tq), head * LANES:(head + 1) * LANES]
            s = _dot_nt(k, qx_s[:, head * LANES:(head + 1) * LANES])
            return jnp.where(causal, s, NEG_BIG) if diagonal else s

        def update(head, s):
            vt = vt_s[head * LANES:(head + 1) * LANES, pl.ds(k0, tq)]
            _softmax_update(s, vt, m_s.at[head], acc_s.at[head])

        _staggered(H_FOX, (scores, update), STAGE_LAG)

    _for_key_blocks(i, tq, block, newest_first=False)
    for blk in range(H_FOX // HEADS_PER_LANE_BLOCK):
        halves = []
        for sub in range(HEADS_PER_LANE_BLOCK):
            a = acc_s[blk * HEADS_PER_LANE_BLOCK + sub]
            halves.append(a[0:HEAD_DIM] / a[HEAD_DIM:HEAD_DIM + 1])
        o_t = jnp.concatenate(halves, axis=0)
        o_ref[0, :, blk * LANES:(blk + 1) * LANES] = o_t.T.astype(o_ref.dtype)


def _sb_prompt_kernel(q_ref, k_ref, v_ref, eye_ref, tri_ref, o_ref, vt_s, qh_s, later_s, acc_s,
                      *, tq):
    i = pl.program_id(1)
    t = k_ref.shape[1]
    lane = lax.broadcasted_iota(jnp.int32, (tq, LANES), 1)
    key, query = _key_query_iota(tq)
    strict = key < query

    @pl.when(i == 0)
    def _per_batch_row():
        for r0 in range(0, t, PREP_ROWS):
            rows = slice(r0, min(r0 + PREP_ROWS, t))
            vt_s[:, rows] = _dot_nt(eye_ref[...], v_ref[0, rows, :]).astype(BF16)

    for head in range(H_SB):
        blk, sub = divmod(head, HEADS_PER_LANE_BLOCK)
        q_pair = _lane_block(q_ref, slice(None), blk)
        qh_s[head] = jnp.where(_in_group(lane, sub, HEAD_DIM), q_pair, jnp.zeros_like(q_pair))
        later_s[head] = jnp.zeros((1, tq), F32)
        acc_s[head] = jnp.zeros((HEAD_DIM, tq), F32)

    def block(k0, diagonal):
        def logits(head, _):
            k = _lane_block(k_ref, pl.ds(k0, tq), head // HEADS_PER_LANE_BLOCK)
            log_hit, log_fail = _log_hit_and_fail(_dot_nt(k, qh_s[head]))
            if diagonal:
                log_fail = jnp.where(strict, log_fail, 0.0)
            return log_hit, _split_hi_lo(log_fail), jnp.sum(log_fail, axis=0, keepdims=True)

        def suffix_sums(head, state):
            log_hit, (hi, lo), fail_sum = state
            tri = tri_ref[...]
            return log_hit + (_dot(tri, hi) + _dot(tri, lo)), fail_sum

        def accumulate(head, state):
            log_w, fail_sum = state
            vt = vt_s[head * HEAD_DIM:(head + 1) * HEAD_DIM, pl.ds(k0, tq)]
            w = jnp.exp(log_w + later_s[head])
            if diagonal:
                w = jnp.where(strict, w, 0.0)
            acc_s[head] += _dot(vt, w.astype(BF16))
            later_s[head] += fail_sum

        _staggered(H_SB, (logits, suffix_sums, accumulate), STAGE_LAG)

    _for_key_blocks(i, tq, block, newest_first=True)
    for blk in range(H_SB // HEADS_PER_LANE_BLOCK):
        o_t = jnp.concatenate([acc_s[blk * HEADS_PER_LANE_BLOCK + sub]
                               for sub in range(HEADS_PER_LANE_BLOCK)], axis=0)
        o_ref[0, :, blk * LANES:(blk + 1) * LANES] = o_t.T.astype(o_ref.dtype)


def _diff_lambda(dl, lambda_init):
    l1 = jnp.sum(dl[0:1] * dl[1:2], axis=1, keepdims=True)
    l2 = jnp.sum(dl[2:3] * dl[3:4], axis=1, keepdims=True)
    return jnp.exp(l1) - jnp.exp(l2) + lambda_init


def _diff_prompt_kernel(q_ref, k_ref, v_ref, pad_qk_ref, pad_vt_ref, dl_ref, subln_ref, o_ref,
                        kp_s, vt_s, qh_s, m_s, acc_s, *, tq, lambda_init):
    i = pl.program_id(1)
    t = k_ref.shape[1]
    lane = lax.broadcasted_iota(jnp.int32, (tq, LANES), 1)
    key, query = _key_query_iota(tq)
    causal = key <= query

    @pl.when(i == 0)
    def _per_batch_row():
        for r0 in range(0, t, PREP_ROWS):
            rows = slice(r0, min(r0 + PREP_ROWS, t))
            kp_s[rows, :] = _dot(k_ref[0, rows, :], pad_qk_ref[...]).astype(BF16)
            vt = _dot_nt(pad_vt_ref[...], v_ref[0, rows, :])
            row = lax.broadcasted_iota(jnp.int32, vt.shape, 0)
            vt_s[:, rows] = jnp.where((row & (LANES - 1)) < DIFF_DV, vt, 1.0).astype(BF16)

    q_pad = _dot(q_ref[0], pad_qk_ref[...]).astype(BF16)
    for head in range(H_DIFF):
        q_head = q_pad[:, head * LANES:(head + 1) * LANES]
        for comp in range(2):
            c = 2 * head + comp
            qh_s[c] = jnp.where(_in_group(lane, comp, HEAD_DIM), q_head, jnp.zeros_like(q_head))
            m_s[c] = jnp.full((1, tq), NEG_BIG, F32)
            acc_s[c] = jnp.zeros((LANES, tq), F32)

    def block(k0, diagonal):
        def scores(c, _):
            head = c // 2
            s = _dot_nt(kp_s[pl.ds(k0, tq), head * LANES:(head + 1) * LANES], qh_s[c])
            return jnp.where(causal, s, NEG_BIG) if diagonal else s

        def update(c, s):
            head = c // 2
            vt = vt_s[head * LANES:(head + 1) * LANES, pl.ds(k0, tq)]
            _softmax_update(s, vt, m_s.at[c], acc_s.at[c])

        _staggered(2 * H_DIFF, (scores, update), STAGE_LAG)

    _for_key_blocks(i, tq, block, newest_first=False)
    lam = _diff_lambda(dl_ref[...], lambda_init)
    gain = subln_ref[0:DIFF_DV, :] * (1.0 - lambda_init)
    for head in range(H_DIFF):
        a0 = acc_s[2 * head]
        a1 = acc_s[2 * head + 1]
        o = (a0[0:DIFF_DV] / a0[DIFF_DV:DIFF_DV + 1]
             - lam * (a1[0:DIFF_DV] / a1[DIFF_DV:DIFF_DV + 1]))
        ms = jnp.sum(o * o, axis=0, keepdims=True) * (1.0 / DIFF_DV)
        o = o * lax.rsqrt(ms + SUBLN_EPS) * gain
        o_t = jnp.concatenate([o, jnp.zeros((LANES - DIFF_DV, tq), F32)], axis=0)
        o_ref[0, :, head * LANES:(head + 1) * LANES] = o_t.T.astype(o_ref.dtype)


def _prompt_attention_call(kernel, name, args, in_specs, out_w, tq, scratch):
    b, t = args[0].shape[0], args[0].shape[1]
    return pl.pallas_call(
        functools.partial(kernel, tq=tq),
        grid=(b, t // tq),
        in_specs=in_specs,
        out_specs=pl.BlockSpec((1, tq, out_w), lambda bi, i: (bi, i, 0)),
        out_shape=jax.ShapeDtypeStruct((b, t, out_w), BF16),
        scratch_shapes=scratch,
        compiler_params=_cparams(("parallel", "arbitrary")),
        name=name,
    )(*args)


def _q_tile_spec(tq, width, col_block=0):
    return pl.BlockSpec((1, tq, width), lambda bi, i: (bi, i, col_block))


def _per_batch_spec(rows, width, col_block=0):
    return pl.BlockSpec((1, rows, width), lambda bi, i: (bi, 0, col_block))


def _const_spec(shape):
    return pl.BlockSpec(shape, lambda bi, i: (0,) * len(shape))


def _fox_prompt(qkv, decay, consts, tq):
    b, t, _ = qkv.shape
    W = BRANCH_W
    wx = H_FOX * LANES
    place_q, place_k, place_vt = consts
    scratch = [pltpu.VMEM((tq, wx), BF16), pltpu.VMEM((t, wx), BF16), pltpu.VMEM((wx, t), BF16),
               pltpu.VMEM((H_FOX, 1, tq), F32), pltpu.VMEM((H_FOX, LANES, tq), F32)]
    return _prompt_attention_call(
        _fox_prompt_kernel, "fox_prompt", (qkv, qkv, qkv, decay, decay, place_q, place_k, place_vt),
        [_q_tile_spec(tq, W, 0), _per_batch_spec(t, W, 1), _per_batch_spec(t, W, 2),
         _q_tile_spec(tq, LANES), _per_batch_spec(t, LANES),
         _const_spec(place_q.shape), _const_spec(place_k.shape), _const_spec(place_vt.shape)],
        W, tq, scratch)


def _sb_prompt(qkv, eye, tri, tq):
    b, t, _ = qkv.shape
    W = BRANCH_W
    scratch = [pltpu.VMEM((W, t), BF16), pltpu.VMEM((H_SB, tq, LANES), BF16),
               pltpu.VMEM((H_SB, 1, tq), F32), pltpu.VMEM((H_SB, HEAD_DIM, tq), F32)]
    return _prompt_attention_call(
        _sb_prompt_kernel, "sb_prompt", (qkv, qkv, qkv, eye, tri),
        [_q_tile_spec(tq, W, 3), _per_batch_spec(t, W, 4), _per_batch_spec(t, W, 5),
         _const_spec(eye.shape), _const_spec(tri.shape)], W, tq, scratch)


def _diff_prompt(qkv, consts, diff_lambda, subln_col, lambda_init, tq):
    b, t, _ = qkv.shape
    W = BRANCH_W
    pad_qk, pad_vt = consts
    chains = 2 * H_DIFF
    scratch = [pltpu.VMEM((t, DIFF_PAD_W), BF16), pltpu.VMEM((DIFF_PAD_W, t), BF16),
               pltpu.VMEM((chains, tq, LANES), BF16), pltpu.VMEM((chains, 1, tq), F32),
               pltpu.VMEM((chains, LANES, tq), F32)]
    return _prompt_attention_call(
        functools.partial(_diff_prompt_kernel, lambda_init=lambda_init), "diff_prompt",
        (qkv, qkv, qkv, pad_qk, pad_vt, diff_lambda, subln_col),
        [_q_tile_spec(tq, W, 0), _per_batch_spec(t, W, 1), _per_batch_spec(t, W, 2),
         _const_spec(pad_qk.shape), _const_spec(pad_vt.shape),
         _const_spec(diff_lambda.shape), _const_spec(subln_col.shape)], DIFF_PAD_W, tq, scratch)


def _decode_kernel(pt_ref, qab_ref, qc_ref, lfnew_ref, dl_ref, subln_ref, tri_ref, *rest,
                   pages_per_step, lambda_init):
    del pt_ref
    P = pages_per_step
    fox_refs = rest[0:P]
    lf_refs = rest[P:2 * P]
    sb_refs = rest[2 * P:3 * P]
    diff_refs = rest[3 * P:4 * P]
    o_ref = rest[4 * P]
    (qa_s, qb_s, qd_s, m_a, l_a, acc_a, carry_a, acc_b, carry_b, m_c, l_c, acc_c) = rest[4 * P + 1:]
    W = BRANCH_W
    R = DEC_ROWS
    page = LANES
    step = pl.program_id(1)
    n_steps = pl.num_programs(1)
    lane = lax.broadcasted_iota(jnp.int32, (R, W), 1)
    rowi = lax.broadcasted_iota(jnp.int32, (R, W), 0)
    head64 = _in_group(lane, rowi, HEAD_DIM)
    comp48 = _in_group(lane, rowi, DIFF_DQ)
    tri_ones = tri_ref[...]

    def bcast(x):
        return jnp.broadcast_to(x, (R, W))

    @pl.when(step == 0)
    def _init():
        qab = qab_ref[0]
        qc = qc_ref[0]
        zero = jnp.zeros((R, W), F32)
        qa = jnp.where(head64, bcast(qab[:, 0:W].astype(F32)), zero)
        qb = jnp.where(head64, bcast(qab[:, 3 * W:4 * W].astype(F32)), zero)
        qd = jnp.where(comp48, bcast(qc[:, 0:W].astype(F32)), zero)
        qa_s[...] = qa.astype(BF16)
        qb_s[...] = qb.astype(BF16)
        qd_s[...] = qd.astype(BF16)
        ka = bcast(qab[:, W:2 * W].astype(F32))
        m_a[...] = jnp.broadcast_to(jnp.sum(qa * ka, axis=1, keepdims=True), (R, LANES))
        l_a[...] = jnp.ones((R, LANES), F32)
        acc_a[...] = bcast(qab[:, 2 * W:3 * W].astype(F32))
        lane_f = lax.broadcasted_iota(jnp.int32, (R, F_PAD), 1)
        row_f = lax.broadcasted_iota(jnp.int32, (R, F_PAD), 0)
        lf_col = jnp.sum(jnp.where(lane_f == row_f, jnp.broadcast_to(lfnew_ref[0], (R, F_PAD)), 0.0),
                         axis=1, keepdims=True)
        carry_a[...] = jnp.broadcast_to(lf_col, (R, LANES))
        acc_b[...] = jnp.zeros((R, W), F32)
        carry_b[...] = jnp.zeros((R, LANES), F32)
        kd = bcast(qc[:, W:2 * W].astype(F32))
        m_c[...] = jnp.broadcast_to(jnp.sum(qd * kd, axis=1, keepdims=True), (R, LANES))
        l_c[...] = jnp.ones((R, LANES), F32)
        acc_c[...] = bcast(qc[:, 2 * W:3 * W].astype(F32))

    def keys(ref):
        return ref[0:W, :].astype(BF16)

    def values(ref):
        return ref[W:2 * W, :].astype(BF16)

    def rows(x, j):
        return x[j * R:(j + 1) * R]

    def softmax_weights(s_parts, m_ref, l_ref):
        m = m_ref[...]
        block_max = s_parts[0]
        for s in s_parts[1:]:
            block_max = jnp.maximum(block_max, s)
        m_new = jnp.maximum(m, jnp.max(block_max, axis=1, keepdims=True))
        alpha = jnp.exp(m - m_new)
        p_parts = [jnp.exp(s - m_new) for s in s_parts]
        p_sum = p_parts[0]
        for p in p_parts[1:]:
            p_sum = p_sum + p
        l_ref[...] = alpha * l_ref[...] + jnp.sum(p_sum, axis=1, keepdims=True)
        m_ref[...] = m_new
        return alpha, p_parts

    def weighted_values(w_parts, v_refs):
        pv = None
        for j in range(P):
            term = _dot_nt(w_parts[j].astype(BF16), values(v_refs[j]))
            pv = term if pv is None else pv + term
        return pv

    z_a = [_dot(qa_s[...], keys(fox_refs[j])) for j in range(P)]
    z_b = jnp.concatenate([_dot(qb_s[...], keys(sb_refs[j])) for j in range(P)], axis=0)
    z_c = [_dot(qd_s[...], keys(diff_refs[j])) for j in range(P)]

    log_hit, log_fail = _log_hit_and_fail(z_b)
    log_forget = jnp.concatenate([lf_refs[j][...] for j in range(P)], axis=0)
    hi, lo = _split_hi_lo(jnp.concatenate([log_fail, log_forget], axis=0))
    sums = _dot(hi, tri_ones) + _dot(lo, tri_ones)
    suffix_b, total_b = sums[0:P * R, 0:page], sums[0:P * R, page:2 * page]
    suffix_a, total_a = sums[P * R:2 * P * R, 0:page], sums[P * R:2 * P * R, page:2 * page]

    later_a = carry_a[...]
    later_b = carry_b[...]
    s_a, w_b = [], []
    for j in range(P):
        s_a.append(z_a[j] + rows(suffix_a, j) + later_a)
        w_b.append(jnp.exp(rows(log_hit, j) + rows(suffix_b, j) + later_b))
        later_a = later_a + rows(total_a, j)
        later_b = later_b + rows(total_b, j)
    carry_a[...] = later_a
    carry_b[...] = later_b

    alpha_a, p_a = softmax_weights(s_a, m_a, l_a)
    alpha_c, p_c = softmax_weights(z_c, m_c, l_c)
    widen = lambda a: jnp.tile(a, (1, W // LANES))
    acc_a[...] = widen(alpha_a) * acc_a[...] + weighted_values(p_a, fox_refs)
    acc_b[...] += weighted_values(w_b, sb_refs)
    acc_c[...] = widen(alpha_c) * acc_c[...] + weighted_values(p_c, diff_refs)

    @pl.when(step == n_steps - 1)
    def _finish():
        zero = jnp.zeros((R, W), F32)
        o_a = jnp.sum(jnp.where(head64, acc_a[...] / l_a[:, 0:1], zero), axis=0, keepdims=True)
        o_b = jnp.sum(jnp.where(head64, acc_b[...], zero), axis=0, keepdims=True)
        n_c = acc_c[...] / l_c[:, 0:1]
        odd_row = (rowi & 1) == 1
        head_lo = (rowi - (rowi & 1)) * DIFF_DQ
        own_head = (lane >= head_lo) & (lane < head_lo + DIFF_DV)
        first = jnp.sum(jnp.where(own_head & ~odd_row, n_c, zero), axis=0, keepdims=True)
        second = jnp.sum(jnp.where(own_head & odd_row, n_c, zero), axis=0, keepdims=True)
        o_c = first - _diff_lambda(dl_ref[...], lambda_init) * second
        lane1 = lax.broadcasted_iota(jnp.int32, (1, W), 1)
        inv = jnp.zeros((1, W), F32)
        for head in range(H_DIFF):
            in_head = _in_group(lane1, head, DIFF_DV)
            ms = jnp.sum(jnp.where(in_head, o_c * o_c, 0.0), axis=1, keepdims=True) * (1.0 / DIFF_DV)
            inv = jnp.where(in_head, lax.rsqrt(ms + SUBLN_EPS), inv)
        o_c = o_c * inv * subln_ref[...] * (1.0 - lambda_init)
        o_ref[0, :, 0:W] = o_a.astype(o_ref.dtype)
        o_ref[0, :, W:2 * W] = o_b.astype(o_ref.dtype)
        o_ref[0, :, 2 * W:3 * W] = o_c.astype(o_ref.dtype)


def _decode_attention(page_table, layer, qkv_ab, qkv_c, logf_new, diff_lambda, subln_row, tri_ones,
                      cache_fox, cache_lf_t, cache_sb, cache_diff, lambda_init, pages_per_step):
    nb, n_pages = page_table.shape
    P = pages_per_step
    assert n_pages % P == 0
    n_steps = n_pages // P
    page = cache_fox.shape[3]
    W = BRANCH_W
    R = DEC_ROWS

    def page_spec(shape, j):
        def index(b, s, pt):
            return (layer, pt[b, n_pages - 1 - (s * P + j)], 0, 0)
        return pl.BlockSpec((None, None) + shape, index)

    per_seq = lambda b, s, pt: (b, 0, 0)
    fixed = lambda b, s, pt: (0, 0)
    in_specs = [
        pl.BlockSpec((1, 1, 6 * W), per_seq),
        pl.BlockSpec((1, 1, 3 * W), per_seq),
        pl.BlockSpec((1, 1, F_PAD), per_seq),
        pl.BlockSpec(diff_lambda.shape, fixed),
        pl.BlockSpec((1, W), fixed),
        pl.BlockSpec(tri_ones.shape, fixed),
    ]
    in_specs += [page_spec((2 * W, page), j) for j in range(P)]
    in_specs += [page_spec((R, page), j) for j in range(P)]
    in_specs += [page_spec((2 * W, page), j) for j in range(P)]
    in_specs += [page_spec((2 * W, page), j) for j in range(P)]
    scratch = [pltpu.VMEM((R, W), BF16)] * 3 + [
        pltpu.VMEM((R, LANES), F32), pltpu.VMEM((R, LANES), F32), pltpu.VMEM((R, W), F32),
        pltpu.VMEM((R, LANES), F32),
        pltpu.VMEM((R, W), F32), pltpu.VMEM((R, LANES), F32),
        pltpu.VMEM((R, LANES), F32), pltpu.VMEM((R, LANES), F32), pltpu.VMEM((R, W), F32),
    ]
    grid_spec = pltpu.PrefetchScalarGridSpec(
        num_scalar_prefetch=1, grid=(nb, n_steps), in_specs=in_specs,
        out_specs=pl.BlockSpec((1, 1, 3 * W), per_seq), scratch_shapes=scratch)
    out = pl.pallas_call(
        functools.partial(_decode_kernel, pages_per_step=P, lambda_init=lambda_init),
        grid_spec=grid_spec,
        out_shape=jax.ShapeDtypeStruct((nb, 1, 3 * W), BF16),
        compiler_params=_cparams(("parallel", "arbitrary")),
        name="decode_attention",
    )(page_table, qkv_ab.reshape(nb, 1, 6 * W), qkv_c.reshape(nb, 1, 3 * W),
      logf_new.reshape(nb, 1, F_PAD), diff_lambda, subln_row, tri_ones,
      *([cache_fox] * P), *([cache_lf_t] * P), *([cache_sb] * P), *([cache_diff] * P))
    return out.reshape(nb, 3 * W)


def _merge_kernel(x_ref, oa_ref, ob_ref, oc_ref, gpre_ref, wg_ref, wa_ref, wb_ref, wc_ref,
                  wout_ref, gpost_ref, y_ref):
    x = x_ref[...]
    d = x.shape[1]
    h = _rms_norm(x, gpre_ref[...], NORM_EPS).astype(BF16)
    merged = None
    for n, (o_ref, w_ref) in enumerate(((oa_ref, wa_ref), (ob_ref, wb_ref), (oc_ref, wc_ref))):
        gate = jax.nn.sigmoid(_dot(h, wg_ref[:, n * d:(n + 1) * d]))
        term = gate * _dot(o_ref[...], w_ref[...])
        merged = term if merged is None else merged + term
    mix = _dot(merged.astype(BF16), wout_ref[...])
    y_ref[...] = x + _rms_norm(mix, gpost_ref[...], NORM_EPS)


def _merge(x, o_a, o_b, o_c, gpre, wg, wa, wb, wc, wout, gpost, tm):
    n, d = x.shape
    row = lambda i: (i, 0)
    fixed = lambda i: (0, 0)
    full = lambda a: pl.BlockSpec(a.shape, fixed)
    return pl.pallas_call(
        _merge_kernel,
        grid=(n // tm,),
        in_specs=[pl.BlockSpec((tm, d), row), pl.BlockSpec((tm, o_a.shape[1]), row),
                  pl.BlockSpec((tm, o_b.shape[1]), row), pl.BlockSpec((tm, o_c.shape[1]), row),
                  full(gpre), full(wg), full(wa), full(wb), full(wc), full(wout), full(gpost)],
        out_specs=pl.BlockSpec((tm, d), row),
        out_shape=jax.ShapeDtypeStruct((n, d), F32),
        compiler_params=_cparams(("parallel",)),
        name="merge",
    )(x, o_a, o_b, o_c, gpre, wg, wa, wb, wc, wout, gpost)


def _ffn_kernel(x_ref, gpre_ref, wgu_ref, wdown_ref, gpost_ref, y_ref, act_ref, *, chunk):
    x = x_ref[...]
    d_ff = wdown_ref.shape[0]
    h = _rms_norm(x, gpre_ref[...], NORM_EPS).astype(BF16)
    for c0 in range(0, d_ff, chunk):
        gate = _dot(h, wgu_ref[:, c0:c0 + chunk])
        up = _dot(h, wgu_ref[:, d_ff + c0:d_ff + c0 + chunk])
        act_ref[:, c0:c0 + chunk] = (gate * jax.nn.sigmoid(gate) * up).astype(BF16)
    ffn = _dot(act_ref[...], wdown_ref[...])
    y_ref[...] = x + _rms_norm(ffn, gpost_ref[...], NORM_EPS)


def _ffn(x, gpre, wgu, wdown, gpost, tm, chunk):
    n, d = x.shape
    d_ff = wdown.shape[0]
    assert d_ff % chunk == 0
    row = lambda i: (i, 0)
    fixed = lambda i: (0, 0)
    full = lambda a: pl.BlockSpec(a.shape, fixed)
    return pl.pallas_call(
        functools.partial(_ffn_kernel, chunk=chunk),
        grid=(n // tm,),
        in_specs=[pl.BlockSpec((tm, d), row), full(gpre), full(wgu), full(wdown), full(gpost)],
        out_specs=pl.BlockSpec((tm, d), row),
        out_shape=jax.ShapeDtypeStruct((n, d), F32),
        scratch_shapes=[pltpu.VMEM((tm, d_ff), BF16)],
        compiler_params=_cparams(("parallel",)),
        name="ffn",
    )(x, gpre, wgu, wdown, gpost)


def _lambda_init(layer):
    return 0.8 - 0.6 * math.exp(-0.3 * layer)


def _rotary_tables(pos):
    half = ROPE_DIM // 2
    inv_freq = ROPE_THETA ** (-jnp.arange(0, ROPE_DIM, 2, dtype=F32) / ROPE_DIM)
    ang = pos.astype(F32)[:, None] * inv_freq[None, :]
    cos, sin = jnp.cos(ang), jnp.sin(ang)
    n = pos.shape[0]
    rest = DIFF_DQ - ROPE_DIM
    ones, zeros = jnp.ones((n, rest), F32), jnp.zeros((n, rest), F32)
    zh = jnp.zeros((n, half), F32)
    reps = BRANCH_W // DIFF_DQ
    cos_t = jnp.tile(jnp.concatenate([cos, cos, ones], axis=1), (1, reps))
    sina_t = jnp.tile(jnp.concatenate([-sin, zh, zeros], axis=1), (1, reps))
    sinb_t = jnp.tile(jnp.concatenate([zh, sin, zeros], axis=1), (1, reps))
    return cos_t, sina_t, sinb_t


def _layer_params(l, norm_mix_pre, norm_mix_post, norm_ffn_pre, norm_ffn_post, w_in, b_forget,
                  diff_lambda, diff_subln, w_branch, w_out, w_gate_up, w_down):
    W = BRANCH_W
    d = w_in.shape[1]
    w = w_in[l]
    q_scale = HEAD_DIM ** -0.5
    cols = [w[:, 0:W] * q_scale, w[:, W:3 * W], w[:, 3 * W:4 * W] * q_scale, w[:, 4 * W:9 * W],
            jnp.pad(w[:, 9 * W:9 * W + H_FOX], ((0, 0), (0, F_PAD - H_FOX)))]
    w_c = w_branch[l, 2]
    w_c_pad = jnp.pad(w_c.reshape(H_DIFF, DIFF_DV, d), ((0, 0), (0, LANES - DIFF_DV), (0, 0)))
    row = lambda v: v.reshape(1, -1).astype(F32)
    return dict(
        w_qkv=jnp.concatenate(cols, axis=1).astype(BF16),
        b_f=jnp.pad(b_forget[l], (0, F_PAD - H_FOX)).reshape(1, F_PAD).astype(F32),
        w_gate=w[:, 9 * W + H_FOX:].astype(BF16),
        w_a=w_branch[l, 0].astype(BF16), w_b=w_branch[l, 1].astype(BF16),
        w_c=w_c.astype(BF16), w_c_pad=w_c_pad.reshape(DIFF_PAD_W, d).astype(BF16),
        w_out=w_out[l].astype(BF16), w_gu=w_gate_up[l].astype(BF16), w_down=w_down[l].astype(BF16),
        g_mix_pre=row(norm_mix_pre[l]), g_mix_post=row(norm_mix_post[l]),
        g_ffn_pre=row(norm_ffn_pre[l]), g_ffn_post=row(norm_ffn_post[l]),
        diff_lambda=diff_lambda[l].astype(F32),
        subln=row(jnp.tile(diff_subln[l], H_DIFF)),
        subln_col=jnp.pad(diff_subln[l], (0, LANES - DIFF_DV)).reshape(LANES, 1).astype(F32),
        lambda_init=_lambda_init(l),
    )


def _strict_lower_ones(n):
    j = lax.broadcasted_iota(jnp.int32, (n, n), 0)
    s = lax.broadcasted_iota(jnp.int32, (n, n), 1)
    return (j > s).astype(BF16)


N_PIECES = 3
ONES_LANE = H_FOX * N_PIECES


def _fox_decay_pieces(c):
    b, t, _ = c.shape
    hi = _bf16_prefix(c)
    mid = _bf16_prefix(c - hi)
    lo = c - hi - mid
    pieces = jnp.stack([hi, mid, lo], axis=-1).reshape(b, t, ONES_LANE)
    out = jnp.concatenate([pieces, jnp.ones((b, t, 1), F32),
                           jnp.zeros((b, t, LANES - ONES_LANE - 1), F32)], axis=-1)
    return out.astype(BF16)


def _fox_placements():
    n_in = H_FOX * HEAD_DIM + LANES
    place_q = np.zeros((n_in, H_FOX * LANES), np.float32)
    place_k = np.zeros((n_in, H_FOX * LANES), np.float32)
    place_vt = np.zeros((H_FOX * LANES, n_in), np.float32)
    pieces0 = H_FOX * HEAD_DIM
    for h in range(H_FOX):
        for d in range(HEAD_DIM):
            place_q[h * HEAD_DIM + d, h * LANES + d] = 1.0
            place_k[h * HEAD_DIM + d, h * LANES + d] = 1.0
            place_vt[h * LANES + d, h * HEAD_DIM + d] = 1.0
            place_vt[h * LANES + HEAD_DIM + d, pieces0 + ONES_LANE] = 1.0
        for j in range(N_PIECES):
            place_q[pieces0 + N_PIECES * h + j, h * LANES + HEAD_DIM + j] = 1.0
            place_q[pieces0 + ONES_LANE, h * LANES + HEAD_DIM + N_PIECES + j] = -1.0
            place_k[pieces0 + ONES_LANE, h * LANES + HEAD_DIM + j] = 1.0
            place_k[pieces0 + N_PIECES * h + j, h * LANES + HEAD_DIM + N_PIECES + j] = 1.0
    return tuple(jnp.asarray(a, BF16) for a in (place_q, place_k, place_vt))


def _diff_placements():
    pad_qk = np.zeros((BRANCH_W, DIFF_PAD_W), np.float32)
    pad_vt = np.zeros((DIFF_PAD_W, BRANCH_W), np.float32)
    for h in range(H_DIFF):
        for c in range(2):
            for d in range(DIFF_DQ):
                pad_qk[h * DIFF_DV + c * DIFF_DQ + d, h * LANES + c * HEAD_DIM + d] = 1.0
        for e in range(DIFF_DV):
            pad_vt[h * LANES + e, h * DIFF_DV + e] = 1.0
    return jnp.asarray(pad_qk, BF16), jnp.asarray(pad_vt, BF16)


def _page_major_cache(cache):
    depth, n_phys, page = cache.shape[:3]
    return jnp.transpose(cache, (0, 1, 3, 4, 5, 2)).reshape(depth, n_phys, -1, page)


def _ffn_chunk(d_ff):
    for chunk in (512, 256, 128):
        if d_ff % chunk == 0:
            return chunk
    return d_ff


def _pages_per_step(n_pages):
    for p in (8, 4, 2):
        if n_pages % p == 0:
            return p
    return 1


def kernel(x_prompt, x_sample, cache_fox_kv, cache_fox_logf, cache_sb_kv, cache_diff_kv, page_table,
           norm_mix_pre, norm_mix_post, norm_ffn_pre, norm_ffn_post, w_in, b_forget, diff_lambda,
           diff_subln, w_branch, w_out, w_gate_up, w_down):
    depth = w_in.shape[0]
    b, t, d = x_prompt.shape
    nb, dec_t, _ = x_sample.shape
    assert dec_t == 1
    page = cache_fox_kv.shape[2]
    n_pages = page_table.shape[1]
    past_len = n_pages * page
    W = BRANCH_W
    d_ff = w_down.shape[1]

    tq = min(256, t)
    tm = min(512, t)
    assert t % tq == 0 and t % tm == 0 and page == LANES
    pages_per_step = _pages_per_step(n_pages)
    chunk = _ffn_chunk(d_ff)

    params = [_layer_params(l, norm_mix_pre, norm_mix_post, norm_ffn_pre, norm_ffn_post, w_in,
                            b_forget, diff_lambda, diff_subln, w_branch, w_out, w_gate_up, w_down)
              for l in range(depth)]
    tab_p = _rotary_tables(jnp.arange(t))
    tab_s = tuple(jnp.broadcast_to(a, (nb, W)) for a in _rotary_tables(jnp.full((1,), past_len)))
    tri_q = _strict_lower_ones(tq).T
    fox_consts = _fox_placements()
    diff_consts = _diff_placements()
    eye_w = jnp.eye(W, dtype=BF16)
    tri_ones_page = jnp.concatenate([_strict_lower_ones(page), jnp.ones((page, page), BF16)], axis=1)

    cache_fox = _page_major_cache(cache_fox_kv)
    cache_sb = _page_major_cache(cache_sb_kv)
    cache_diff = _page_major_cache(cache_diff_kv)
    cache_lf_t = jnp.pad(jnp.swapaxes(cache_fox_logf, 2, 3),
                         ((0, 0), (0, 0), (0, DEC_ROWS - H_FOX), (0, 0)))

    def tail(x, o_a, o_b, o_c, w_c, p, rows):
        x = _merge(x, o_a, o_b, o_c, p["g_mix_pre"], p["w_gate"], p["w_a"], p["w_b"], w_c,
                   p["w_out"], p["g_mix_post"], rows)
        return _ffn(x, p["g_ffn_pre"], p["w_gu"], p["w_down"], p["g_ffn_post"], rows, chunk)

    x = x_prompt.reshape(b * t, d)
    rows_p, logf_p = None, []
    for l, p in enumerate(params):
        qkv_ab, rows_p, qkv_c, logf = _inproj(
            x, p["g_mix_pre"], p["w_qkv"], p["b_f"], *tab_p, tm, l, depth, rows_p)
        logf6 = logf[:, :H_FOX].reshape(b, t, H_FOX)
        logf_p.append(logf6)
        c_t = _cumsum_time(jnp.pad(jnp.swapaxes(logf6, 1, 2), ((0, 0), (0, 8 - H_FOX), (0, 0))))
        c = jnp.swapaxes(c_t, 1, 2)[..., :H_FOX]
        qkv_ab3 = qkv_ab.reshape(b, t, 6 * W)
        qkv_c3 = qkv_c.reshape(b, t, 3 * W)
        o_a = _fox_prompt(qkv_ab3, _fox_decay_pieces(c), fox_consts, tq)
        o_b = _sb_prompt(qkv_ab3, eye_w, tri_q, tq)
        o_c = _diff_prompt(qkv_c3, diff_consts, p["diff_lambda"], p["subln_col"], p["lambda_init"], tq)
        x = tail(x, o_a.reshape(b * t, W), o_b.reshape(b * t, W), o_c.reshape(b * t, DIFF_PAD_W),
                 p["w_c_pad"], p, tm)
    y_prompt = x.reshape(b, t, d)

    x = x_sample.reshape(nb, d)
    rows_s, logf_s = None, []
    for l, p in enumerate(params):
        qkv_ab, rows_s, qkv_c, logf = _inproj(
            x, p["g_mix_pre"], p["w_qkv"], p["b_f"], *tab_s, nb, l, depth, rows_s)
        logf_s.append(logf[:, :H_FOX].reshape(nb, 1, H_FOX))
        o = _decode_attention(page_table, l, qkv_ab, qkv_c, logf, p["diff_lambda"], p["subln"],
                              tri_ones_page, cache_fox, cache_lf_t, cache_sb, cache_diff,
                              p["lambda_init"], pages_per_step)
        x = tail(x, o[:, 0:W], o[:, W:2 * W], o[:, 2 * W:3 * W], p["w_c"], p, nb)
    y_sample = x.reshape(nb, 1, d)

    def rows_out(rows, lead):
        fox, sb, diff = rows
        return (fox.reshape(depth, *lead, 2, H_FOX, HEAD_DIM), sb.reshape(depth, *lead, 2, H_SB, HEAD_DIM),
                diff.reshape(depth, *lead, 2, H_DIFF, DIFF_DV))

    fox_p, sb_p, diff_p = rows_out(rows_p, (b, t))
    fox_s, sb_s, diff_s = rows_out(rows_s, (nb, 1))
    return (y_prompt, y_sample, fox_p, jnp.stack(logf_p), sb_p, diff_p,
            fox_s, jnp.stack(logf_s), sb_s, diff_s)
```

```python
import functools
import math

import jax
import jax.numpy as jnp
import numpy as np
from jax import lax
from jax.experimental import pallas as pl
from jax.experimental.pallas import tpu as pltpu

F32 = jnp.float32
BF16 = jnp.bfloat16

HEAD_DIM = 64
H_FOX = 6
H_SB = 6
H_DIFF = 4
DIFF_DQ = 48
DIFF_DV = 2 * DIFF_DQ
BRANCH_W = 384
N_BRANCH = 3
ROPE_DIM = DIFF_DQ // 4
ROPE_THETA = 500000.0
NORM_EPS = 1e-6
SUBLN_EPS = 1e-5

LANES = 128
HEADS_PER_LANE_BLOCK = LANES // HEAD_DIM
F_PAD = LANES
DIFF_PAD_W = H_DIFF * LANES
NEG_BIG = -1e30
VMEM_LIMIT = 56 * 1024 * 1024
DEC_ROWS = 16
LF_ROWS = 8
STAGE_LAG = 4
PREP_ROWS = 512


def _cparams(sem):
    return pltpu.CompilerParams(dimension_semantics=sem, vmem_limit_bytes=VMEM_LIMIT)


def _rms_norm(x, gain, eps):
    ms = jnp.mean(x * x, axis=-1, keepdims=True)
    return x * lax.rsqrt(ms + eps) * gain


def _softplus_neg_abs(z):
    return jnp.log1p(jnp.exp(-jnp.abs(z)))


def _log_hit_and_fail(z):
    log_fail = jnp.minimum(-z, 0.0) - jnp.log(1.0 + jnp.exp(-jnp.abs(z)))
    return log_fail + z, log_fail


def _bf16_prefix(x):
    bits = lax.bitcast_convert_type(x, jnp.uint32) & jnp.uint32(0xFFFF0000)
    return lax.bitcast_convert_type(bits, F32)


def _split_hi_lo(x):
    hi = _bf16_prefix(x)
    return hi.astype(BF16), (x - hi).astype(BF16)


def _in_group(index, group, width):
    lo = group * width
    return (index >= lo) & (index < lo + width)


def _dot(a, b):
    return jnp.dot(a, b, preferred_element_type=F32)


def _dot_nt(a, b):
    return lax.dot_general(a, b, (((1,), (1,)), ((), ())), preferred_element_type=F32)


def _inproj_kernel(x_ref, gain_ref, w_ref, bf_ref, cos_ref, sina_ref, sinb_ref, *rest, rows_transposed):
    qkv_ab_ref, foxkv_ref, sbkv_ref, diffkv_ref, qkv_c_ref, logf_ref = rest[-6:]
    W = BRANCH_W
    h = _rms_norm(x_ref[...], gain_ref[...], NORM_EPS).astype(BF16)

    def proj(c):
        return _dot(h, w_ref[:, c * W:(c + 1) * W])

    def rotary(r):
        return (r * cos_ref[...] + pltpu.roll(r, W - ROPE_DIM // 2, 1) * sina_ref[...]
                + pltpu.roll(r, ROPE_DIM // 2, 1) * sinb_ref[...])

    def store_rows(kv_ref, j, r):
        if rows_transposed:
            kv_ref[j * W:(j + 1) * W, :] = r.T
        else:
            kv_ref[:, j * W:(j + 1) * W] = r

    for c, kv_ref in ((0, foxkv_ref), (3, sbkv_ref)):
        qkv_ab_ref[:, c * W:(c + 1) * W] = (proj(c) * (HEAD_DIM ** -0.5)).astype(BF16)
        for j in range(2):
            r = proj(c + 1 + j)
            qkv_ab_ref[:, (c + 1 + j) * W:(c + 2 + j) * W] = r.astype(BF16)
            store_rows(kv_ref, j, r)

    qc = rotary(proj(6)) * (DIFF_DQ ** -0.5)
    qkv_c_ref[:, 0:W] = qc.astype(BF16)
    kc = rotary(proj(7))
    qkv_c_ref[:, W:2 * W] = kc.astype(BF16)
    store_rows(diffkv_ref, 0, kc)
    vc = proj(8)
    qkv_c_ref[:, 2 * W:3 * W] = vc.astype(BF16)
    store_rows(diffkv_ref, 1, vc)

    f = _dot(h, w_ref[:, 9 * W:9 * W + F_PAD]) + bf_ref[...]
    logf_ref[...] = jnp.minimum(f, 0.0) - _softplus_neg_abs(f)


def _inproj(x, gain, w, bf, cos_t, sina_t, sinb_t, tm, layer, depth, rows_so_far, seq_len=None):
    n, d = x.shape
    W = BRANCH_W
    n_tab = cos_t.shape[0] // tm
    row = lambda i: (i, 0)
    fixed = lambda i: (0, 0)
    tab = lambda i: (i % n_tab, 0)
    if seq_len is None:
        slab = pl.BlockSpec((None, tm, 2 * W), lambda i: (layer, i, 0))
        rows_shape = jax.ShapeDtypeStruct((depth, n, 2 * W), F32)
    else:
        assert seq_len == cos_t.shape[0]
        slab = pl.BlockSpec((None, None, 2 * W, tm), lambda i: (layer, i // n_tab, 0, i % n_tab))
        rows_shape = jax.ShapeDtypeStruct((depth, n // seq_len, 2 * W, seq_len), F32)
    out_shape = (
        jax.ShapeDtypeStruct((n, 6 * W), BF16), rows_shape, rows_shape, rows_shape,
        jax.ShapeDtypeStruct((n, 3 * W), BF16),
        jax.ShapeDtypeStruct((n, F_PAD), F32),
    )
    in_specs = [
        pl.BlockSpec((tm, d), row),
        pl.BlockSpec((1, d), fixed),
        pl.BlockSpec(w.shape, fixed),
        pl.BlockSpec((1, F_PAD), fixed),
        pl.BlockSpec((tm, W), tab),
        pl.BlockSpec((tm, W), tab),
        pl.BlockSpec((tm, W), tab),
    ]
    args = [x, gain, w, bf, cos_t, sina_t, sinb_t]
    aliases = {}
    if rows_so_far is not None:
        for j, buf in enumerate(rows_so_far):
            aliases[len(args)] = 1 + j
            args.append(buf)
            in_specs.append(pl.BlockSpec(memory_space=pl.ANY))
    qkv_ab, fox_rows, sb_rows, diff_rows, qkv_c, logf = pl.pallas_call(
        functools.partial(_inproj_kernel, rows_transposed=seq_len is not None),
        grid=(n // tm,),
        in_specs=in_specs,
        out_specs=(pl.BlockSpec((tm, 6 * W), row), slab, slab, slab,
                   pl.BlockSpec((tm, 3 * W), row), pl.BlockSpec((tm, F_PAD), row)),
        out_shape=out_shape,
        input_output_aliases=aliases,
        compiler_params=_cparams(("parallel",)),
        name="inproj",
    )(*args)
    return qkv_ab, (fox_rows, sb_rows, diff_rows), qkv_c, logf


def _cumsum_kernel(x_ref, o_ref):
    x = x_ref[0]
    t = x.shape[-1]
    lane = lax.broadcasted_iota(jnp.int32, x.shape, 1)
    shift = 1
    while shift < t:
        x = x + jnp.where(lane >= shift, pltpu.roll(x, shift, 1), 0.0)
        shift *= 2
    o_ref[0] = x


def _cumsum_time(logf_t):
    b, r, t = logf_t.shape
    blk = pl.BlockSpec((1, r, t), lambda i: (i, 0, 0))
    return pl.pallas_call(
        _cumsum_kernel, grid=(b,), in_specs=[blk], out_specs=blk,
        out_shape=jax.ShapeDtypeStruct(logf_t.shape, F32),
        compiler_params=_cparams(("parallel",)), name="cumsum_logf",
    )(logf_t)


def _lane_block(ref, rows, blk):
    return ref[0, rows, blk * LANES:(blk + 1) * LANES]


def _softmax_update(s, vt_ext, m_ref, acc_ref):
    m = m_ref[...]
    m_new = jnp.maximum(m, jnp.max(s, axis=0, keepdims=True))
    p = jnp.exp(s - m_new)
    acc_ref[...] = jnp.exp(m - m_new) * acc_ref[...] + _dot(vt_ext, p.astype(BF16))
    m_ref[...] = m_new


def _staggered(n_chains, stages, lag):
    state = [None] * n_chains
    for step in range(n_chains + (len(stages) - 1) * lag):
        for k, stage in enumerate(stages):
            c = step - k * lag
            if 0 <= c < n_chains:
                state[c] = stage(c, state[c])


def _key_query_iota(tq):
    key = lax.broadcasted_iota(jnp.int32, (tq, tq), 0)
    query = lax.broadcasted_iota(jnp.int32, (tq, tq), 1)
    return key, query


def _for_key_blocks(i, tq, block, newest_first):
    def body(step, carry):
        kb = (i - 1 - step) if newest_first else step
        block(pl.multiple_of(kb * tq, tq), False)
        return carry

    q0 = pl.multiple_of(i * tq, tq)
    if newest_first:
        block(q0, True)
        lax.fori_loop(0, i, body, 0)
    else:
        lax.fori_loop(0, i, body, 0)
        block(q0, True)


def _fox_prompt_kernel(q_ref, k_ref, vt_ref, cq_ref, ck_ref, place_q_ref, place_k_ref,
                       o_ref, qx_s, kx_s, vt_s, m_s, acc_s, *, tq):
    i = pl.program_id(1)
    t = k_ref.shape[1]
    key, query = _key_query_iota(tq)
    causal = key <= query

    @pl.when(i == 0)
    def _per_batch_row():
        for r0 in range(0, t, PREP_ROWS):
            rows = slice(r0, min(r0 + PREP_ROWS, t))
            kx = _dot(jnp.concatenate([k_ref[0, rows, :], ck_ref[0, rows, :]], axis=1), place_k_ref[...])
            kx_s[rows, :] = kx.astype(BF16)
        for head in range(H_FOX):
            vt_s[head * LANES:head * LANES + HEAD_DIM, :] = (
                vt_ref[head * HEAD_DIM:(head + 1) * HEAD_DIM, :].astype(BF16))
            vt_s[head * LANES + HEAD_DIM:(head + 1) * LANES, :] = jnp.ones((LANES - HEAD_DIM, t), BF16)

    qx = _dot(jnp.concatenate([q_ref[0], cq_ref[0]], axis=1), place_q_ref[...])
    qx_s[...] = qx.astype(BF16)
    for head in range(H_FOX):
        m_s[head] = jnp.full((1, tq), NEG_BIG, F32)
        acc_s[head] = jnp.zeros((LANES, tq), F32)

    def block(k0, diagonal):
        def scores(head, _):
            k = kx_s[pl.ds(k0, tq), head * LANES:(head + 1) * LANES]
            s = _dot_nt(k, qx_s[:, head * LANES:(head + 1) * LANES])
            return jnp.where(causal, s, NEG_BIG) if diagonal else s

        def update(head, s):
            vt = vt_s[head * LANES:(head + 1) * LANES, pl.ds(k0, tq)]
            _softmax_update(s, vt, m_s.at[head], acc_s.at[head])

        _staggered(H_FOX, (scores, update), STAGE_LAG)

    _for_key_blocks(i, tq, block, newest_first=False)
    for blk in range(H_FOX // HEADS_PER_LANE_BLOCK):
        halves = []
        for sub in range(HEADS_PER_LANE_BLOCK):
            a = acc_s[blk * HEADS_PER_LANE_BLOCK + sub]
            halves.append(a[0:HEAD_DIM] / a[HEAD_DIM:HEAD_DIM + 1])
        o_t = jnp.concatenate(halves, axis=0)
        o_ref[0, :, blk * LANES:(blk + 1) * LANES] = o_t.T.astype(o_ref.dtype)


def _sb_prompt_kernel(q_ref, k_ref, vt_ref, tri_ref, o_ref, qh_s, later_s, acc_s, *, tq):
    i = pl.program_id(1)
    lane = lax.broadcasted_iota(jnp.int32, (tq, LANES), 1)
    key, query = _key_query_iota(tq)
    strict = key < query
    for head in range(H_SB):
        blk, sub = divmod(head, HEADS_PER_LANE_BLOCK)
        q_pair = _lane_block(q_ref, slice(None), blk)
        qh_s[head] = jnp.where(_in_group(lane, sub, HEAD_DIM), q_pair, jnp.zeros_like(q_pair))
        later_s[head] = jnp.zeros((1, tq), F32)
        acc_s[head] = jnp.zeros((HEAD_DIM, tq), F32)

    def block(k0, diagonal):
        def logits(head, _):
            k = _lane_block(k_ref, pl.ds(k0, tq), head // HEADS_PER_LANE_BLOCK)
            log_hit, log_fail = _log_hit_and_fail(_dot_nt(k, qh_s[head]))
            if diagonal:
                log_fail = jnp.where(strict, log_fail, 0.0)
            return log_hit, _split_hi_lo(log_fail), jnp.sum(log_fail, axis=0, keepdims=True)

        def suffix_sums(head, state):
            log_hit, (hi, lo), fail_sum = state
            tri = tri_ref[...]
            return log_hit + (_dot(tri, hi) + _dot(tri, lo)), fail_sum

        def accumulate(head, state):
            log_w, fail_sum = state
            vt = vt_ref[head * HEAD_DIM:(head + 1) * HEAD_DIM, pl.ds(k0, tq)].astype(BF16)
            w = jnp.exp(log_w + later_s[head])
            if diagonal:
                w = jnp.where(strict, w, 0.0)
            acc_s[head] += _dot(vt, w.astype(BF16))
            later_s[head] += fail_sum

        _staggered(H_SB, (logits, suffix_sums, accumulate), STAGE_LAG)

    _for_key_blocks(i, tq, block, newest_first=True)
    for blk in range(H_SB // HEADS_PER_LANE_BLOCK):
        o_t = jnp.concatenate([acc_s[blk * HEADS_PER_LANE_BLOCK + sub]
                               for sub in range(HEADS_PER_LANE_BLOCK)], axis=0)
        o_ref[0, :, blk * LANES:(blk + 1) * LANES] = o_t.T.astype(o_ref.dtype)


def _diff_lambda(dl, lambda_init):
    l1 = jnp.sum(dl[0:1] * dl[1:2], axis=1, keepdims=True)
    l2 = jnp.sum(dl[2:3] * dl[3:4], axis=1, keepdims=True)
    return jnp.exp(l1) - jnp.exp(l2) + lambda_init


def _diff_prompt_kernel(q_ref, k_ref, vt_ref, pad_qk_ref, dl_ref, subln_ref, o_ref,
                        kp_s, vt_s, qh_s, m_s, acc_s, *, tq, lambda_init):
    i = pl.program_id(1)
    t = k_ref.shape[1]
    lane = lax.broadcasted_iota(jnp.int32, (tq, LANES), 1)
    key, query = _key_query_iota(tq)
    causal = key <= query

    @pl.when(i == 0)
    def _per_batch_row():
        for r0 in range(0, t, PREP_ROWS):
            rows = slice(r0, min(r0 + PREP_ROWS, t))
            kp_s[rows, :] = _dot(k_ref[0, rows, :], pad_qk_ref[...]).astype(BF16)
        for head in range(H_DIFF):
            vt_s[head * LANES:head * LANES + DIFF_DV, :] = (
                vt_ref[head * DIFF_DV:(head + 1) * DIFF_DV, :].astype(BF16))
            vt_s[head * LANES + DIFF_DV:(head + 1) * LANES, :] = jnp.ones((LANES - DIFF_DV, t), BF16)

    q_pad = _dot(q_ref[0], pad_qk_ref[...]).astype(BF16)
    for head in range(H_DIFF):
        q_head = q_pad[:, head * LANES:(head + 1) * LANES]
        for comp in range(2):
            c = 2 * head + comp
            qh_s[c] = jnp.where(_in_group(lane, comp, HEAD_DIM), q_head, jnp.zeros_like(q_head))
            m_s[c] = jnp.full((1, tq), NEG_BIG, F32)
            acc_s[c] = jnp.zeros((LANES, tq), F32)

    def block(k0, diagonal):
        def scores(c, _):
            head = c // 2
            s = _dot_nt(kp_s[pl.ds(k0, tq), head * LANES:(head + 1) * LANES], qh_s[c])
            return jnp.where(causal, s, NEG_BIG) if diagonal else s

        def update(c, s):
            head = c // 2
            vt = vt_s[head * LANES:(head + 1) * LANES, pl.ds(k0, tq)]
            _softmax_update(s, vt, m_s.at[c], acc_s.at[c])

        _staggered(2 * H_DIFF, (scores, update), STAGE_LAG)

    _for_key_blocks(i, tq, block, newest_first=False)
    lam = _diff_lambda(dl_ref[...], lambda_init)
    gain = subln_ref[0:DIFF_DV, :] * (1.0 - lambda_init)
    for head in range(H_DIFF):
        a0 = acc_s[2 * head]
        a1 = acc_s[2 * head + 1]
        o = (a0[0:DIFF_DV] / a0[DIFF_DV:DIFF_DV + 1]
             - lam * (a1[0:DIFF_DV] / a1[DIFF_DV:DIFF_DV + 1]))
        ms = jnp.sum(o * o, axis=0, keepdims=True) * (1.0 / DIFF_DV)
        o = o * lax.rsqrt(ms + SUBLN_EPS) * gain
        o_t = jnp.concatenate([o, jnp.zeros((LANES - DIFF_DV, tq), F32)], axis=0)
        o_ref[0, :, head * LANES:(head + 1) * LANES] = o_t.T.astype(o_ref.dtype)


def _prompt_attention_call(kernel, name, args, in_specs, out_w, tq, scratch):
    b, t = args[0].shape[0], args[0].shape[1]
    return pl.pallas_call(
        functools.partial(kernel, tq=tq),
        grid=(b, t // tq),
        in_specs=in_specs,
        out_specs=pl.BlockSpec((1, tq, out_w), lambda bi, i: (bi, i, 0)),
        out_shape=jax.ShapeDtypeStruct((b, t, out_w), BF16),
        scratch_shapes=scratch,
        compiler_params=_cparams(("parallel", "arbitrary")),
        name=name,
    )(*args)


def _q_tile_spec(tq, width, col_block=0):
    return pl.BlockSpec((1, tq, width), lambda bi, i: (bi, i, col_block))


def _per_batch_spec(rows, width, col_block=0):
    return pl.BlockSpec((1, rows, width), lambda bi, i: (bi, 0, col_block))


def _const_spec(shape):
    return pl.BlockSpec(shape, lambda bi, i: (0,) * len(shape))


def _values_spec(layer, t):
    return pl.BlockSpec((None, None, BRANCH_W, t), lambda bi, i: (layer, bi, 1, 0))


def _fox_prompt(qkv, rows_t, layer, decay, consts, tq):
    b, t, _ = qkv.shape
    W = BRANCH_W
    wx = H_FOX * LANES
    place_q, place_k = consts
    scratch = [pltpu.VMEM((tq, wx), BF16), pltpu.VMEM((t, wx), BF16), pltpu.VMEM((wx, t), BF16),
               pltpu.VMEM((H_FOX, 1, tq), F32), pltpu.VMEM((H_FOX, LANES, tq), F32)]
    return _prompt_attention_call(
        _fox_prompt_kernel, "fox_prompt", (qkv, qkv, rows_t, decay, decay, place_q, place_k),
        [_q_tile_spec(tq, W, 0), _per_batch_spec(t, W, 1), _values_spec(layer, t),
         _q_tile_spec(tq, LANES), _per_batch_spec(t, LANES),
         _const_spec(place_q.shape), _const_spec(place_k.shape)], W, tq, scratch)


def _sb_prompt(qkv, rows_t, layer, tri, tq):
    b, t, _ = qkv.shape
    W = BRANCH_W
    scratch = [pltpu.VMEM((H_SB, tq, LANES), BF16), pltpu.VMEM((H_SB, 1, tq), F32),
               pltpu.VMEM((H_SB, HEAD_DIM, tq), F32)]
    return _prompt_attention_call(
        _sb_prompt_kernel, "sb_prompt", (qkv, qkv, rows_t, tri),
        [_q_tile_spec(tq, W, 3), _per_batch_spec(t, W, 4), _values_spec(layer, t),
         _const_spec(tri.shape)], W, tq, scratch)


def _diff_prompt(qkv, rows_t, layer, pad_qk, diff_lambda, subln_col, lambda_init, tq):
    b, t, _ = qkv.shape
    W = BRANCH_W
    chains = 2 * H_DIFF
    scratch = [pltpu.VMEM((t, DIFF_PAD_W), BF16), pltpu.VMEM((DIFF_PAD_W, t), BF16),
               pltpu.VMEM((chains, tq, LANES), BF16), pltpu.VMEM((chains, 1, tq), F32),
               pltpu.VMEM((chains, LANES, tq), F32)]
    return _prompt_attention_call(
        functools.partial(_diff_prompt_kernel, lambda_init=lambda_init), "diff_prompt",
        (qkv, qkv, rows_t, pad_qk, diff_lambda, subln_col),
        [_q_tile_spec(tq, W, 0), _per_batch_spec(t, W, 1), _values_spec(layer, t),
         _const_spec(pad_qk.shape), _const_spec(diff_lambda.shape), _const_spec(subln_col.shape)],
        DIFF_PAD_W, tq, scratch)


def _decode_kernel(pt_ref, qab_ref, qc_ref, lfnew_ref, dl_ref, subln_ref, tri_ref, *rest,
                   pages_per_step, lambda_init):
    del pt_ref
    P = pages_per_step
    fox_refs = rest[0:P]
    lf_refs = rest[P:2 * P]
    sb_refs = rest[2 * P:3 * P]
    diff_refs = rest[3 * P:4 * P]
    o_ref = rest[4 * P]
    (qa_s, qb_s, qd_s, m_a, l_a, acc_a, carry_a, acc_b, carry_b, m_c, l_c, acc_c) = rest[4 * P + 1:]
    W = BRANCH_W
    R = DEC_ROWS
    page = LANES
    step = pl.program_id(1)
    n_steps = pl.num_programs(1)
    lane = lax.broadcasted_iota(jnp.int32, (R, W), 1)
    rowi = lax.broadcasted_iota(jnp.int32, (R, W), 0)
    head64 = _in_group(lane, rowi, HEAD_DIM)
    comp48 = _in_group(lane, rowi, DIFF_DQ)
    tri_ones = tri_ref[...]

    def bcast(x):
        return jnp.broadcast_to(x, (R, W))

    @pl.when(step == 0)
    def _init():
        qab = qab_ref[0]
        qc = qc_ref[0]
        zero = jnp.zeros((R, W), F32)
        qa = jnp.where(head64, bcast(qab[:, 0:W].astype(F32)), zero)
        qb = jnp.where(head64, bcast(qab[:, 3 * W:4 * W].astype(F32)), zero)
        qd = jnp.where(comp48, bcast(qc[:, 0:W].astype(F32)), zero)
        qa_s[...] = qa.astype(BF16)
        qb_s[...] = qb.astype(BF16)
        qd_s[...] = qd.astype(BF16)
        ka = bcast(qab[:, W:2 * W].astype(F32))
        m_a[...] = jnp.broadcast_to(jnp.sum(qa * ka, axis=1, keepdims=True), (R, LANES))
        l_a[...] = jnp.ones((R, LANES), F32)
        acc_a[...] = bcast(qab[:, 2 * W:3 * W].astype(F32))
        lane_f = lax.broadcasted_iota(jnp.int32, (R, F_PAD), 1)
        row_f = lax.broadcasted_iota(jnp.int32, (R, F_PAD), 0)
        lf_col = jnp.sum(jnp.where(lane_f == row_f, jnp.broadcast_to(lfnew_ref[0], (R, F_PAD)), 0.0),
                         axis=1, keepdims=True)
        carry_a[...] = jnp.broadcast_to(lf_col, (R, LANES))
        acc_b[...] = jnp.zeros((R, W), F32)
        carry_b[...] = jnp.zeros((R, LANES), F32)
        kd = bcast(qc[:, W:2 * W].astype(F32))
        m_c[...] = jnp.broadcast_to(jnp.sum(qd * kd, axis=1, keepdims=True), (R, LANES))
        l_c[...] = jnp.ones((R, LANES), F32)
        acc_c[...] = bcast(qc[:, 2 * W:3 * W].astype(F32))

    def keys(ref):
        return ref[0:W, :].astype(BF16)

    def values(ref):
        return ref[W:2 * W, :].astype(BF16)

    def rows(x, j):
        return x[j * R:(j + 1) * R]

    def softmax_weights(s_parts, m_ref, l_ref):
        m = m_ref[...]
        block_max = s_parts[0]
        for s in s_parts[1:]:
            block_max = jnp.maximum(block_max, s)
        m_new = jnp.maximum(m, jnp.max(block_max, axis=1, keepdims=True))
        alpha = jnp.exp(m - m_new)
        p_parts = [jnp.exp(s - m_new) for s in s_parts]
        p_sum = p_parts[0]
        for p in p_parts[1:]:
            p_sum = p_sum + p
        l_ref[...] = alpha * l_ref[...] + jnp.sum(p_sum, axis=1, keepdims=True)
        m_ref[...] = m_new
        return alpha, p_parts

    def weighted_values(w_parts, v_refs):
        pv = None
        for j in range(P):
            term = _dot_nt(w_parts[j].astype(BF16), values(v_refs[j]))
            pv = term if pv is None else pv + term
        return pv

    z_a = [_dot(qa_s[...], keys(fox_refs[j])) for j in range(P)]
    z_b = jnp.concatenate([_dot(qb_s[...], keys(sb_refs[j])) for j in range(P)], axis=0)
    z_c = [_dot(qd_s[...], keys(diff_refs[j])) for j in range(P)]

    log_hit, log_fail = _log_hit_and_fail(z_b)
    no_head = jnp.zeros((R - LF_ROWS, page), F32)
    log_forget = jnp.concatenate(
        [part for j in range(P) for part in (lf_refs[j][...], no_head)], axis=0)
    hi, lo = _split_hi_lo(jnp.concatenate([log_fail, log_forget], axis=0))
    sums = _dot(hi, tri_ones) + _dot(lo, tri_ones)
    suffix_b, total_b = sums[0:P * R, 0:page], sums[0:P * R, page:2 * page]
    suffix_a, total_a = sums[P * R:2 * P * R, 0:page], sums[P * R:2 * P * R, page:2 * page]

    later_a = carry_a[...]
    later_b = carry_b[...]
    s_a, w_b = [], []
    for j in range(P):
        s_a.append(z_a[j] + rows(suffix_a, j) + later_a)
        w_b.append(jnp.exp(rows(log_hit, j) + rows(suffix_b, j) + later_b))
        later_a = later_a + rows(total_a, j)
        later_b = later_b + rows(total_b, j)
    carry_a[...] = later_a
    carry_b[...] = later_b

    alpha_a, p_a = softmax_weights(s_a, m_a, l_a)
    alpha_c, p_c = softmax_weights(z_c, m_c, l_c)
    widen = lambda a: jnp.tile(a, (1, W // LANES))
    acc_a[...] = widen(alpha_a) * acc_a[...] + weighted_values(p_a, fox_refs)
    acc_b[...] += weighted_values(w_b, sb_refs)
    acc_c[...] = widen(alpha_c) * acc_c[...] + weighted_values(p_c, diff_refs)

    @pl.when(step == n_steps - 1)
    def _finish():
        zero = jnp.zeros((R, W), F32)
        o_a = jnp.sum(jnp.where(head64, acc_a[...] / l_a[:, 0:1], zero), axis=0, keepdims=True)
        o_b = jnp.sum(jnp.where(head64, acc_b[...], zero), axis=0, keepdims=True)
        n_c = acc_c[...] / l_c[:, 0:1]
        odd_row = (rowi & 1) == 1
        head_lo = (rowi - (rowi & 1)) * DIFF_DQ
        own_head = (lane >= head_lo) & (lane < head_lo + DIFF_DV)
        first = jnp.sum(jnp.where(own_head & ~odd_row, n_c, zero), axis=0, keepdims=True)
        second = jnp.sum(jnp.where(own_head & odd_row, n_c, zero), axis=0, keepdims=True)
        o_c = first - _diff_lambda(dl_ref[...], lambda_init) * second
        lane1 = lax.broadcasted_iota(jnp.int32, (1, W), 1)
        inv = jnp.zeros((1, W), F32)
        for head in range(H_DIFF):
            in_head = _in_group(lane1, head, DIFF_DV)
            ms = jnp.sum(jnp.where(in_head, o_c * o_c, 0.0), axis=1, keepdims=True) * (1.0 / DIFF_DV)
            inv = jnp.where(in_head, lax.rsqrt(ms + SUBLN_EPS), inv)
        o_c = o_c * inv * subln_ref[...] * (1.0 - lambda_init)
        o_ref[0, :, 0:W] = o_a.astype(o_ref.dtype)
        o_ref[0, :, W:2 * W] = o_b.astype(o_ref.dtype)
        o_ref[0, :, 2 * W:3 * W] = o_c.astype(o_ref.dtype)


def _decode_attention(page_table, layer, qkv_ab, qkv_c, logf_new, diff_lambda, subln_row, tri_ones,
                      cache_fox, cache_lf_t, cache_sb, cache_diff, lambda_init, pages_per_step):
    nb, n_pages = page_table.shape
    P = pages_per_step
    assert n_pages % P == 0
    n_steps = n_pages // P
    page = cache_fox.shape[3]
    W = BRANCH_W
    R = DEC_ROWS

    def page_spec(shape, j):
        def index(b, s, pt):
            return (layer, pt[b, n_pages - 1 - (s * P + j)], 0, 0)
        return pl.BlockSpec((None, None) + shape, index)

    per_seq = lambda b, s, pt: (b, 0, 0)
    fixed = lambda b, s, pt: (0, 0)
    in_specs = [
        pl.BlockSpec((1, 1, 6 * W), per_seq),
        pl.BlockSpec((1, 1, 3 * W), per_seq),
        pl.BlockSpec((1, 1, F_PAD), per_seq),
        pl.BlockSpec(diff_lambda.shape, fixed),
        pl.BlockSpec((1, W), fixed),
        pl.BlockSpec(tri_ones.shape, fixed),
    ]
    in_specs += [page_spec((2 * W, page), j) for j in range(P)]
    in_specs += [page_spec((LF_ROWS, page), j) for j in range(P)]
    in_specs += [page_spec((2 * W, page), j) for j in range(P)]
    in_specs += [page_spec((2 * W, page), j) for j in range(P)]
    scratch = [pltpu.VMEM((R, W), BF16)] * 3 + [
        pltpu.VMEM((R, LANES), F32), pltpu.VMEM((R, LANES), F32), pltpu.VMEM((R, W), F32),
        pltpu.VMEM((R, LANES), F32),
        pltpu.VMEM((R, W), F32), pltpu.VMEM((R, LANES), F32),
        pltpu.VMEM((R, LANES), F32), pltpu.VMEM((R, LANES), F32), pltpu.VMEM((R, W), F32),
    ]
    grid_spec = pltpu.PrefetchScalarGridSpec(
        num_scalar_prefetch=1, grid=(nb, n_steps), in_specs=in_specs,
        out_specs=pl.BlockSpec((1, 1, 3 * W), per_seq), scratch_shapes=scratch)
    out = pl.pallas_call(
        functools.partial(_decode_kernel, pages_per_step=P, lambda_init=lambda_init),
        grid_spec=grid_spec,
        out_shape=jax.ShapeDtypeStruct((nb, 1, 3 * W), BF16),
        compiler_params=_cparams(("parallel", "arbitrary")),
        name="decode_attention",
    )(page_table, qkv_ab.reshape(nb, 1, 6 * W), qkv_c.reshape(nb, 1, 3 * W),
      logf_new.reshape(nb, 1, F_PAD), diff_lambda, subln_row, tri_ones,
      *([cache_fox] * P), *([cache_lf_t] * P), *([cache_sb] * P), *([cache_diff] * P))
    return out.reshape(nb, 3 * W)


def _merge_kernel(x_ref, oa_ref, ob_ref, oc_ref, gpre_ref, wg_ref, wa_ref, wb_ref, wc_ref,
                  wout_ref, gpost_ref, y_ref):
    x = x_ref[...]
    d = x.shape[1]
    h = _rms_norm(x, gpre_ref[...], NORM_EPS).astype(BF16)
    merged = None
    for n, (o_ref, w_ref) in enumerate(((oa_ref, wa_ref), (ob_ref, wb_ref), (oc_ref, wc_ref))):
        gate = jax.nn.sigmoid(_dot(h, wg_ref[:, n * d:(n + 1) * d]))
        term = gate * _dot(o_ref[...], w_ref[...])
        merged = term if merged is None else merged + term
    mix = _dot(merged.astype(BF16), wout_ref[...])
    y_ref[...] = x + _rms_norm(mix, gpost_ref[...], NORM_EPS)


def _merge(x, o_a, o_b, o_c, gpre, wg, wa, wb, wc, wout, gpost, tm):
    n, d = x.shape
    row = lambda i: (i, 0)
    fixed = lambda i: (0, 0)
    full = lambda a: pl.BlockSpec(a.shape, fixed)
    return pl.pallas_call(
        _merge_kernel,
        grid=(n // tm,),
        in_specs=[pl.BlockSpec((tm, d), row), pl.BlockSpec((tm, o_a.shape[1]), row),
                  pl.BlockSpec((tm, o_b.shape[1]), row), pl.BlockSpec((tm, o_c.shape[1]), row),
                  full(gpre), full(wg), full(wa), full(wb), full(wc), full(wout), full(gpost)],
        out_specs=pl.BlockSpec((tm, d), row),
        out_shape=jax.ShapeDtypeStruct((n, d), F32),
        compiler_params=_cparams(("parallel",)),
        name="merge",
    )(x, o_a, o_b, o_c, gpre, wg, wa, wb, wc, wout, gpost)


def _ffn_kernel(x_ref, gpre_ref, wgu_ref, wdown_ref, gpost_ref, y_ref, act_ref, *, chunk):
    x = x_ref[...]
    d_ff = wdown_ref.shape[0]
    h = _rms_norm(x, gpre_ref[...], NORM_EPS).astype(BF16)
    for c0 in range(0, d_ff, chunk):
        gate = _dot(h, wgu_ref[:, c0:c0 + chunk])
        up = _dot(h, wgu_ref[:, d_ff + c0:d_ff + c0 + chunk])
        act_ref[:, c0:c0 + chunk] = (gate * jax.nn.sigmoid(gate) * up).astype(BF16)
    ffn = _dot(act_ref[...], wdown_ref[...])
    y_ref[...] = x + _rms_norm(ffn, gpost_ref[...], NORM_EPS)


def _ffn(x, gpre, wgu, wdown, gpost, tm, chunk):
    n, d = x.shape
    d_ff = wdown.shape[0]
    assert d_ff % chunk == 0
    row = lambda i: (i, 0)
    fixed = lambda i: (0, 0)
    full = lambda a: pl.BlockSpec(a.shape, fixed)
    return pl.pallas_call(
        functools.partial(_ffn_kernel, chunk=chunk),
        grid=(n // tm,),
        in_specs=[pl.BlockSpec((tm, d), row), full(gpre), full(wgu), full(wdown), full(gpost)],
        out_specs=pl.BlockSpec((tm, d), row),
        out_shape=jax.ShapeDtypeStruct((n, d), F32),
        scratch_shapes=[pltpu.VMEM((tm, d_ff), BF16)],
        compiler_params=_cparams(("parallel",)),
        name="ffn",
    )(x, gpre, wgu, wdown, gpost)


def _lambda_init(layer):
    return 0.8 - 0.6 * math.exp(-0.3 * layer)


def _rotary_tables(pos):
    half = ROPE_DIM // 2
    inv_freq = ROPE_THETA ** (-jnp.arange(0, ROPE_DIM, 2, dtype=F32) / ROPE_DIM)
    ang = pos.astype(F32)[:, None] * inv_freq[None, :]
    cos, sin = jnp.cos(ang), jnp.sin(ang)
    n = pos.shape[0]
    rest = DIFF_DQ - ROPE_DIM
    ones, zeros = jnp.ones((n, rest), F32), jnp.zeros((n, rest), F32)
    zh = jnp.zeros((n, half), F32)
    reps = BRANCH_W // DIFF_DQ
    cos_t = jnp.tile(jnp.concatenate([cos, cos, ones], axis=1), (1, reps))
    sina_t = jnp.tile(jnp.concatenate([-sin, zh, zeros], axis=1), (1, reps))
    sinb_t = jnp.tile(jnp.concatenate([zh, sin, zeros], axis=1), (1, reps))
    return cos_t, sina_t, sinb_t


def _layer_params(l, norm_mix_pre, norm_mix_post, norm_ffn_pre, norm_ffn_post, w_in, b_forget,
                  diff_lambda, diff_subln, w_branch, w_out, w_gate_up, w_down):
    W = BRANCH_W
    d = w_in.shape[1]
    w = w_in[l]
    w_qkv = jnp.pad(w[:, :9 * W + H_FOX], ((0, 0), (0, F_PAD - H_FOX)))
    w_c = w_branch[l, 2]
    w_c_pad = jnp.pad(w_c.reshape(H_DIFF, DIFF_DV, d), ((0, 0), (0, LANES - DIFF_DV), (0, 0)))
    row = lambda v: v.reshape(1, -1).astype(F32)
    return dict(
        w_qkv=w_qkv.astype(BF16),
        b_f=jnp.pad(b_forget[l], (0, F_PAD - H_FOX)).reshape(1, F_PAD).astype(F32),
        w_gate=w[:, 9 * W + H_FOX:].astype(BF16),
        w_a=w_branch[l, 0].astype(BF16), w_b=w_branch[l, 1].astype(BF16),
        w_c=w_c.astype(BF16), w_c_pad=w_c_pad.reshape(DIFF_PAD_W, d).astype(BF16),
        w_out=w_out[l].astype(BF16), w_gu=w_gate_up[l].astype(BF16), w_down=w_down[l].astype(BF16),
        g_mix_pre=row(norm_mix_pre[l]), g_mix_post=row(norm_mix_post[l]),
        g_ffn_pre=row(norm_ffn_pre[l]), g_ffn_post=row(norm_ffn_post[l]),
        diff_lambda=diff_lambda[l].astype(F32),
        subln=row(jnp.tile(diff_subln[l], H_DIFF)),
        subln_col=jnp.pad(diff_subln[l], (0, LANES - DIFF_DV)).reshape(LANES, 1).astype(F32),
        lambda_init=_lambda_init(l),
    )


def _strict_lower_ones(n):
    j = lax.broadcasted_iota(jnp.int32, (n, n), 0)
    s = lax.broadcasted_iota(jnp.int32, (n, n), 1)
    return (j > s).astype(BF16)


N_PIECES = 3
ONES_LANE = H_FOX * N_PIECES


def _fox_decay_pieces(c):
    b, t, _ = c.shape
    hi = _bf16_prefix(c)
    mid = _bf16_prefix(c - hi)
    lo = c - hi - mid
    pieces = jnp.stack([hi, mid, lo], axis=-1).reshape(b, t, ONES_LANE)
    out = jnp.concatenate([pieces, jnp.ones((b, t, 1), F32),
                           jnp.zeros((b, t, LANES - ONES_LANE - 1), F32)], axis=-1)
    return out.astype(BF16)


def _fox_placements():
    n_in = H_FOX * HEAD_DIM + LANES
    place_q = np.zeros((n_in, H_FOX * LANES), np.float32)
    place_k = np.zeros((n_in, H_FOX * LANES), np.float32)
    pieces0 = H_FOX * HEAD_DIM
    for h in range(H_FOX):
        for d in range(HEAD_DIM):
            place_q[h * HEAD_DIM + d, h * LANES + d] = 1.0
            place_k[h * HEAD_DIM + d, h * LANES + d] = 1.0
        for j in range(N_PIECES):
            place_q[pieces0 + N_PIECES * h + j, h * LANES + HEAD_DIM + j] = 1.0
            place_q[pieces0 + ONES_LANE, h * LANES + HEAD_DIM + N_PIECES + j] = -1.0
            place_k[pieces0 + ONES_LANE, h * LANES + HEAD_DIM + j] = 1.0
            place_k[pieces0 + N_PIECES * h + j, h * LANES + HEAD_DIM + N_PIECES + j] = 1.0
    return jnp.asarray(place_q, BF16), jnp.asarray(place_k, BF16)


def _diff_placement():
    pad_qk = np.zeros((BRANCH_W, DIFF_PAD_W), np.float32)
    for h in range(H_DIFF):
        for c in range(2):
            for d in range(DIFF_DQ):
                pad_qk[h * DIFF_DV + c * DIFF_DQ + d, h * LANES + c * HEAD_DIM + d] = 1.0
    return jnp.asarray(pad_qk, BF16)


def _page_major_cache(cache):
    depth, n_phys, page = cache.shape[:3]
    return jnp.transpose(cache, (0, 1, 3, 4, 5, 2)).reshape(depth, n_phys, -1, page)


def _ffn_chunk(d_ff):
    for chunk in (512, 256, 128):
        if d_ff % chunk == 0:
            return chunk
    return d_ff


def _pages_per_step(n_pages):
    for p in (8, 4, 2):
        if n_pages % p == 0:
            return p
    return 1


def kernel(x_prompt, x_sample, cache_fox_kv, cache_fox_logf, cache_sb_kv, cache_diff_kv, page_table,
           norm_mix_pre, norm_mix_post, norm_ffn_pre, norm_ffn_post, w_in, b_forget, diff_lambda,
           diff_subln, w_branch, w_out, w_gate_up, w_down):
    depth = w_in.shape[0]
    b, t, d = x_prompt.shape
    nb, dec_t, _ = x_sample.shape
    assert dec_t == 1
    page = cache_fox_kv.shape[2]
    n_pages = page_table.shape[1]
    past_len = n_pages * page
    W = BRANCH_W
    d_ff = w_down.shape[1]

    tq = min(256, t)
    tm = min(512, t)
    assert t % tq == 0 and t % tm == 0 and page == LANES
    pages_per_step = _pages_per_step(n_pages)
    chunk = _ffn_chunk(d_ff)

    params = [_layer_params(l, norm_mix_pre, norm_mix_post, norm_ffn_pre, norm_ffn_post, w_in,
                            b_forget, diff_lambda, diff_subln, w_branch, w_out, w_gate_up, w_down)
              for l in range(depth)]
    tab_p = _rotary_tables(jnp.arange(t))
    tab_s = tuple(jnp.broadcast_to(a, (nb, W)) for a in _rotary_tables(jnp.full((1,), past_len)))
    tri_q = _strict_lower_ones(tq).T
    fox_consts = _fox_placements()
    diff_pad = _diff_placement()
    tri_ones_page = jnp.concatenate([_strict_lower_ones(page), jnp.ones((page, page), BF16)], axis=1)

    cache_fox = _page_major_cache(cache_fox_kv)
    cache_sb = _page_major_cache(cache_sb_kv)
    cache_diff = _page_major_cache(cache_diff_kv)
    cache_lf_t = jnp.pad(jnp.swapaxes(cache_fox_logf, 2, 3),
                         ((0, 0), (0, 0), (0, LF_ROWS - H_FOX), (0, 0)))

    def tail(x, o_a, o_b, o_c, w_c, p, rows):
        x = _merge(x, o_a, o_b, o_c, p["g_mix_pre"], p["w_gate"], p["w_a"], p["w_b"], w_c,
                   p["w_out"], p["g_mix_post"], rows)
        return _ffn(x, p["g_ffn_pre"], p["w_gu"], p["w_down"], p["g_ffn_post"], rows, chunk)

    x = x_prompt.reshape(b * t, d)
    rows_p, logf_p = None, []
    for l, p in enumerate(params):
        qkv_ab, rows_p, qkv_c, logf = _inproj(
            x, p["g_mix_pre"], p["w_qkv"], p["b_f"], *tab_p, tm, l, depth, rows_p, seq_len=t)
        logf6 = logf[:, :H_FOX].reshape(b, t, H_FOX)
        logf_p.append(logf6)
        c_t = _cumsum_time(jnp.pad(jnp.swapaxes(logf6, 1, 2), ((0, 0), (0, 8 - H_FOX), (0, 0))))
        c = jnp.swapaxes(c_t, 1, 2)[..., :H_FOX]
        qkv_ab3 = qkv_ab.reshape(b, t, 6 * W)
        qkv_c3 = qkv_c.reshape(b, t, 3 * W)
        fox_t, sb_t, diff_t = rows_p
        o_a = _fox_prompt(qkv_ab3, fox_t, l, _fox_decay_pieces(c), fox_consts, tq)
        o_b = _sb_prompt(qkv_ab3, sb_t, l, tri_q, tq)
        o_c = _diff_prompt(qkv_c3, diff_t, l, diff_pad, p["diff_lambda"], p["subln_col"],
                           p["lambda_init"], tq)
        x = tail(x, o_a.reshape(b * t, W), o_b.reshape(b * t, W), o_c.reshape(b * t, DIFF_PAD_W),
                 p["w_c_pad"], p, tm)
    y_prompt = x.reshape(b, t, d)

    x = x_sample.reshape(nb, d)
    rows_s, logf_s = None, []
    for l, p in enumerate(params):
        qkv_ab, rows_s, qkv_c, logf = _inproj(
            x, p["g_mix_pre"], p["w_qkv"], p["b_f"], *tab_s, nb, l, depth, rows_s)
        logf_s.append(logf[:, :H_FOX].reshape(nb, 1, H_FOX))
        o = _decode_attention(page_table, l, qkv_ab, qkv_c, logf, p["diff_lambda"], p["subln"],
                              tri_ones_page, cache_fox, cache_lf_t, cache_sb, cache_diff,
                              p["lambda_init"], pages_per_step)
        x = tail(x, o[:, 0:W], o[:, W:2 * W], o[:, 2 * W:3 * W], p["w_c"], p, nb)
    y_sample = x.reshape(nb, 1, d)

    def token_major(rows_t, heads, dim):
        return jnp.transpose(rows_t.reshape(depth, b, 2, heads, dim, t), (0, 1, 5, 2, 3, 4))

    fox_p = token_major(rows_p[0], H_FOX, HEAD_DIM)
    sb_p = token_major(rows_p[1], H_SB, HEAD_DIM)
    diff_p = token_major(rows_p[2], H_DIFF, DIFF_DV)
    fox_s = rows_s[0].reshape(depth, nb, 1, 2, H_FOX, HEAD_DIM)
    sb_s = rows_s[1].reshape(depth, nb, 1, 2, H_SB, HEAD_DIM)
    diff_s = rows_s[2].reshape(depth, nb, 1, 2, H_DIFF, DIFF_DV)
    return (y_prompt, y_sample, fox_p, jnp.stack(logf_p), sb_p, diff_p,
            fox_s, jnp.stack(logf_s), sb_s, diff_s)
```

```python
import functools
import math

import jax
import jax.numpy as jnp
import numpy as np
from jax import lax
from jax.experimental import pallas as pl
from jax.experimental.pallas import tpu as pltpu

F32 = jnp.float32
BF16 = jnp.bfloat16

HEAD_DIM = 64
H_FOX = 6
H_SB = 6
H_DIFF = 4
DIFF_DQ = 48
DIFF_DV = 2 * DIFF_DQ
BRANCH_W = 384
N_BRANCH = 3
ROPE_DIM = DIFF_DQ // 4
ROPE_THETA = 500000.0
NORM_EPS = 1e-6
SUBLN_EPS = 1e-5

LANES = 128
HEADS_PER_LANE_BLOCK = LANES // HEAD_DIM
F_PAD = LANES
DIFF_PAD_W = H_DIFF * LANES
NEG_BIG = -1e30
VMEM_LIMIT = 56 * 1024 * 1024
DEC_ROWS = 16
LF_ROWS = 8
STAGE_LAG = 4
PREP_ROWS = 512


def _cparams(sem):
    return pltpu.CompilerParams(dimension_semantics=sem, vmem_limit_bytes=VMEM_LIMIT)


def _rms_norm(x, gain, eps):
    ms = jnp.mean(x * x, axis=-1, keepdims=True)
    return x * lax.rsqrt(ms + eps) * gain


def _softplus_neg_abs(z):
    return jnp.log1p(jnp.exp(-jnp.abs(z)))


def _log_hit_and_fail(z):
    log_fail = jnp.minimum(-z, 0.0) - jnp.log(1.0 + jnp.exp(-jnp.abs(z)))
    return log_fail + z, log_fail


def _bf16_prefix(x):
    bits = lax.bitcast_convert_type(x, jnp.uint32) & jnp.uint32(0xFFFF0000)
    return lax.bitcast_convert_type(bits, F32)


def _split_hi_lo(x):
    hi = _bf16_prefix(x)
    return hi.astype(BF16), (x - hi).astype(BF16)


def _in_group(index, group, width):
    lo = group * width
    return (index >= lo) & (index < lo + width)


def _dot(a, b):
    return jnp.dot(a, b, preferred_element_type=F32)


def _dot_nt(a, b):
    return lax.dot_general(a, b, (((1,), (1,)), ((), ())), preferred_element_type=F32)


def _inproj_kernel(x_ref, gain_ref, w_ref, bf_ref, cos_ref, sina_ref, sinb_ref, *rest, rows_transposed):
    qkv_ab_ref, foxkv_ref, sbkv_ref, diffkv_ref, qkv_c_ref, logf_ref = rest[-6:]
    W = BRANCH_W
    h = _rms_norm(x_ref[...], gain_ref[...], NORM_EPS).astype(BF16)

    def proj(c):
        return _dot(h, w_ref[:, c * W:(c + 1) * W])

    def rotary(r):
        return (r * cos_ref[...] + pltpu.roll(r, W - ROPE_DIM // 2, 1) * sina_ref[...]
                + pltpu.roll(r, ROPE_DIM // 2, 1) * sinb_ref[...])

    def store_rows(kv_ref, j, r):
        if rows_transposed:
            kv_ref[j * W:(j + 1) * W, :] = r.T
        else:
            kv_ref[:, j * W:(j + 1) * W] = r

    for c, kv_ref in ((0, foxkv_ref), (3, sbkv_ref)):
        qkv_ab_ref[:, c * W:(c + 1) * W] = proj(c).astype(BF16)
        for j in range(2):
            r = proj(c + 1 + j)
            qkv_ab_ref[:, (c + 1 + j) * W:(c + 2 + j) * W] = r.astype(BF16)
            store_rows(kv_ref, j, r)

    qc = rotary(proj(6)) * (DIFF_DQ ** -0.5)
    qkv_c_ref[:, 0:W] = qc.astype(BF16)
    kc = rotary(proj(7))
    qkv_c_ref[:, W:2 * W] = kc.astype(BF16)
    store_rows(diffkv_ref, 0, kc)
    vc = proj(8)
    qkv_c_ref[:, 2 * W:3 * W] = vc.astype(BF16)
    store_rows(diffkv_ref, 1, vc)

    f = _dot(h, w_ref[:, 9 * W:9 * W + F_PAD]) + bf_ref[...]
    logf_ref[...] = jnp.minimum(f, 0.0) - _softplus_neg_abs(f)


def _inproj(x, gain, w, bf, cos_t, sina_t, sinb_t, tm, layer, depth, rows_so_far, seq_len=None):
    n, d = x.shape
    W = BRANCH_W
    n_tab = cos_t.shape[0] // tm
    row = lambda i: (i, 0)
    fixed = lambda i: (0, 0)
    tab = lambda i: (i % n_tab, 0)
    if seq_len is None:
        slab = pl.BlockSpec((None, tm, 2 * W), lambda i: (layer, i, 0))
        rows_shape = jax.ShapeDtypeStruct((depth, n, 2 * W), F32)
    else:
        assert seq_len == cos_t.shape[0]
        slab = pl.BlockSpec((None, None, 2 * W, tm), lambda i: (layer, i // n_tab, 0, i % n_tab))
        rows_shape = jax.ShapeDtypeStruct((depth, n // seq_len, 2 * W, seq_len), F32)
    out_shape = (
        jax.ShapeDtypeStruct((n, 6 * W), BF16), rows_shape, rows_shape, rows_shape,
        jax.ShapeDtypeStruct((n, 3 * W), BF16),
        jax.ShapeDtypeStruct((n, F_PAD), F32),
    )
    in_specs = [
        pl.BlockSpec((tm, d), row),
        pl.BlockSpec((1, d), fixed),
        pl.BlockSpec(w.shape, fixed),
        pl.BlockSpec((1, F_PAD), fixed),
        pl.BlockSpec((tm, W), tab),
        pl.BlockSpec((tm, W), tab),
        pl.BlockSpec((tm, W), tab),
    ]
    args = [x, gain, w, bf, cos_t, sina_t, sinb_t]
    aliases = {}
    if rows_so_far is not None:
        for j, buf in enumerate(rows_so_far):
            aliases[len(args)] = 1 + j
            args.append(buf)
            in_specs.append(pl.BlockSpec(memory_space=pl.ANY))
    qkv_ab, fox_rows, sb_rows, diff_rows, qkv_c, logf = pl.pallas_call(
        functools.partial(_inproj_kernel, rows_transposed=seq_len is not None),
        grid=(n // tm,),
        in_specs=in_specs,
        out_specs=(pl.BlockSpec((tm, 6 * W), row), slab, slab, slab,
                   pl.BlockSpec((tm, 3 * W), row), pl.BlockSpec((tm, F_PAD), row)),
        out_shape=out_shape,
        input_output_aliases=aliases,
        compiler_params=_cparams(("parallel",)),
        name="inproj",
    )(*args)
    return qkv_ab, (fox_rows, sb_rows, diff_rows), qkv_c, logf


def _cumsum_kernel(x_ref, o_ref):
    x = x_ref[0]
    t = x.shape[-1]
    lane = lax.broadcasted_iota(jnp.int32, x.shape, 1)
    shift = 1
    while shift < t:
        x = x + jnp.where(lane >= shift, pltpu.roll(x, shift, 1), 0.0)
        shift *= 2
    o_ref[0] = x


def _cumsum_time(logf_t):
    b, r, t = logf_t.shape
    blk = pl.BlockSpec((1, r, t), lambda i: (i, 0, 0))
    return pl.pallas_call(
        _cumsum_kernel, grid=(b,), in_specs=[blk], out_specs=blk,
        out_shape=jax.ShapeDtypeStruct(logf_t.shape, F32),
        compiler_params=_cparams(("parallel",)), name="cumsum_logf",
    )(logf_t)


def _lane_block(ref, rows, blk):
    return ref[0, rows, blk * LANES:(blk + 1) * LANES]


def _softmax_update(s, vt_ext, m_ref, acc_ref):
    m = m_ref[...]
    m_new = jnp.maximum(m, jnp.max(s, axis=0, keepdims=True))
    p = jnp.exp(s - m_new)
    acc_ref[...] = jnp.exp(m - m_new) * acc_ref[...] + _dot(vt_ext, p.astype(BF16))
    m_ref[...] = m_new


def _staggered(n_chains, stages, lag):
    state = [None] * n_chains
    for step in range(n_chains + (len(stages) - 1) * lag):
        for k, stage in enumerate(stages):
            c = step - k * lag
            if 0 <= c < n_chains:
                state[c] = stage(c, state[c])


def _key_query_iota(tq):
    key = lax.broadcasted_iota(jnp.int32, (tq, tq), 0)
    query = lax.broadcasted_iota(jnp.int32, (tq, tq), 1)
    return key, query


def _for_key_blocks(i, tq, block, newest_first):
    def body(step, carry):
        kb = (i - 1 - step) if newest_first else step
        block(pl.multiple_of(kb * tq, tq), False)
        return carry

    q0 = pl.multiple_of(i * tq, tq)
    if newest_first:
        block(q0, True)
        lax.fori_loop(0, i, body, 0)
    else:
        lax.fori_loop(0, i, body, 0)
        block(q0, True)


def _fox_prompt_kernel(q_ref, k_ref, vt_ref, cq_ref, ck_ref, place_q_ref, place_k_ref,
                       o_ref, qx_s, kx_s, vt_s, m_s, acc_s, *, tq):
    i = pl.program_id(1)
    t = k_ref.shape[1]
    key, query = _key_query_iota(tq)
    causal = key <= query

    @pl.when(i == 0)
    def _per_batch_row():
        for r0 in range(0, t, PREP_ROWS):
            rows = slice(r0, min(r0 + PREP_ROWS, t))
            kx = _dot(jnp.concatenate([k_ref[0, rows, :], ck_ref[0, rows, :]], axis=1), place_k_ref[...])
            kx_s[rows, :] = kx.astype(BF16)
        for head in range(H_FOX):
            vt_s[head * LANES:head * LANES + HEAD_DIM, :] = (
                vt_ref[head * HEAD_DIM:(head + 1) * HEAD_DIM, :].astype(BF16))
            vt_s[head * LANES + HEAD_DIM:(head + 1) * LANES, :] = jnp.ones((LANES - HEAD_DIM, t), BF16)

    qx = _dot(jnp.concatenate([q_ref[0], cq_ref[0]], axis=1), place_q_ref[...])
    qx_s[...] = qx.astype(BF16)
    for head in range(H_FOX):
        m_s[head] = jnp.full((1, tq), NEG_BIG, F32)
        acc_s[head] = jnp.zeros((LANES, tq), F32)

    def block(k0, diagonal):
        def scores(head, _):
            k = kx_s[pl.ds(k0, tq), head * LANES:(head + 1) * LANES]
            s = _dot_nt(k, qx_s[:, head * LANES:(head + 1) * LANES])
            return jnp.where(causal, s, NEG_BIG) if diagonal else s

        def update(head, s):
            vt = vt_s[head * LANES:(head + 1) * LANES, pl.ds(k0, tq)]
            _softmax_update(s, vt, m_s.at[head], acc_s.at[head])

        _staggered(H_FOX, (scores, update), STAGE_LAG)

    _for_key_blocks(i, tq, block, newest_first=False)
    for blk in range(H_FOX // HEADS_PER_LANE_BLOCK):
        halves = []
        for sub in range(HEADS_PER_LANE_BLOCK):
            a = acc_s[blk * HEADS_PER_LANE_BLOCK + sub]
            halves.append(a[0:HEAD_DIM] / a[HEAD_DIM:HEAD_DIM + 1])
        o_t = jnp.concatenate(halves, axis=0)
        o_ref[0, :, blk * LANES:(blk + 1) * LANES] = o_t.T.astype(o_ref.dtype)


def _sb_prompt_kernel(q_ref, k_ref, vt_ref, tri_ref, o_ref, qh_s, later_s, acc_s, *, tq):
    i = pl.program_id(1)
    lane = lax.broadcasted_iota(jnp.int32, (tq, LANES), 1)
    key, query = _key_query_iota(tq)
    strict = key < query
    for head in range(H_SB):
        blk, sub = divmod(head, HEADS_PER_LANE_BLOCK)
        q_pair = _lane_block(q_ref, slice(None), blk)
        qh_s[head] = jnp.where(_in_group(lane, sub, HEAD_DIM), q_pair, jnp.zeros_like(q_pair))
        later_s[head] = jnp.zeros((1, tq), F32)
        acc_s[head] = jnp.zeros((HEAD_DIM, tq), F32)

    def block(k0, diagonal):
        def logits(head, _):
            k = _lane_block(k_ref, pl.ds(k0, tq), head // HEADS_PER_LANE_BLOCK)
            log_hit, log_fail = _log_hit_and_fail(_dot_nt(k, qh_s[head]))
            if diagonal:
                log_fail = jnp.where(strict, log_fail, 0.0)
            return log_hit, _split_hi_lo(log_fail), jnp.sum(log_fail, axis=0, keepdims=True)

        def suffix_sums(head, state):
            log_hit, (hi, lo), fail_sum = state
            tri = tri_ref[...]
            return log_hit + (_dot(tri, hi) + _dot(tri, lo)), fail_sum

        def accumulate(head, state):
            log_w, fail_sum = state
            vt = vt_ref[head * HEAD_DIM:(head + 1) * HEAD_DIM, pl.ds(k0, tq)].astype(BF16)
            w = jnp.exp(log_w + later_s[head])
            if diagonal:
                w = jnp.where(strict, w, 0.0)
            acc_s[head] += _dot(vt, w.astype(BF16))
            later_s[head] += fail_sum

        _staggered(H_SB, (logits, suffix_sums, accumulate), STAGE_LAG)

    _for_key_blocks(i, tq, block, newest_first=True)
    for blk in range(H_SB // HEADS_PER_LANE_BLOCK):
        o_t = jnp.concatenate([acc_s[blk * HEADS_PER_LANE_BLOCK + sub]
                               for sub in range(HEADS_PER_LANE_BLOCK)], axis=0)
        o_ref[0, :, blk * LANES:(blk + 1) * LANES] = o_t.T.astype(o_ref.dtype)


def _diff_lambda(dl, lambda_init):
    l1 = jnp.sum(dl[0:1] * dl[1:2], axis=1, keepdims=True)
    l2 = jnp.sum(dl[2:3] * dl[3:4], axis=1, keepdims=True)
    return jnp.exp(l1) - jnp.exp(l2) + lambda_init


def _diff_prompt_kernel(q_ref, k_ref, vt_ref, pad_qk_ref, dl_ref, subln_ref, o_ref,
                        kp_s, vt_s, qh_s, m_s, acc_s, *, tq, lambda_init):
    i = pl.program_id(1)
    t = k_ref.shape[1]
    lane = lax.broadcasted_iota(jnp.int32, (tq, LANES), 1)
    key, query = _key_query_iota(tq)
    causal = key <= query

    @pl.when(i == 0)
    def _per_batch_row():
        for r0 in range(0, t, PREP_ROWS):
            rows = slice(r0, min(r0 + PREP_ROWS, t))
            kp_s[rows, :] = _dot(k_ref[0, rows, :], pad_qk_ref[...]).astype(BF16)
        for head in range(H_DIFF):
            vt_s[head * LANES:head * LANES + DIFF_DV, :] = (
                vt_ref[head * DIFF_DV:(head + 1) * DIFF_DV, :].astype(BF16))
            vt_s[head * LANES + DIFF_DV:(head + 1) * LANES, :] = jnp.ones((LANES - DIFF_DV, t), BF16)

    q_pad = _dot(q_ref[0], pad_qk_ref[...]).astype(BF16)
    for head in range(H_DIFF):
        q_head = q_pad[:, head * LANES:(head + 1) * LANES]
        for comp in range(2):
            c = 2 * head + comp
            qh_s[c] = jnp.where(_in_group(lane, comp, HEAD_DIM), q_head, jnp.zeros_like(q_head))
            m_s[c] = jnp.full((1, tq), NEG_BIG, F32)
            acc_s[c] = jnp.zeros((LANES, tq), F32)

    def block(k0, diagonal):
        def scores(c, _):
            head = c // 2
            s = _dot_nt(kp_s[pl.ds(k0, tq), head * LANES:(head + 1) * LANES], qh_s[c])
            return jnp.where(causal, s, NEG_BIG) if diagonal else s

        def update(c, s):
            head = c // 2
            vt = vt_s[head * LANES:(head + 1) * LANES, pl.ds(k0, tq)]
            _softmax_update(s, vt, m_s.at[c], acc_s.at[c])

        _staggered(2 * H_DIFF, (scores, update), STAGE_LAG)

    _for_key_blocks(i, tq, block, newest_first=False)
    lam = _diff_lambda(dl_ref[...], lambda_init)
    gain = subln_ref[0:DIFF_DV, :] * (1.0 - lambda_init)
    for head in range(H_DIFF):
        a0 = acc_s[2 * head]
        a1 = acc_s[2 * head + 1]
        o = (a0[0:DIFF_DV] / a0[DIFF_DV:DIFF_DV + 1]
             - lam * (a1[0:DIFF_DV] / a1[DIFF_DV:DIFF_DV + 1]))
        ms = jnp.sum(o * o, axis=0, keepdims=True) * (1.0 / DIFF_DV)
        o = o * lax.rsqrt(ms + SUBLN_EPS) * gain
        o_t = jnp.concatenate([o, jnp.zeros((LANES - DIFF_DV, tq), F32)], axis=0)
        o_ref[0, :, head * LANES:(head + 1) * LANES] = o_t.T.astype(o_ref.dtype)


def _prompt_attention_call(kernel, name, args, in_specs, out_w, tq, scratch):
    b, t = args[0].shape[0], args[0].shape[1]
    return pl.pallas_call(
        functools.partial(kernel, tq=tq),
        grid=(b, t // tq),
        in_specs=in_specs,
        out_specs=pl.BlockSpec((1, tq, out_w), lambda bi, i: (bi, i, 0)),
        out_shape=jax.ShapeDtypeStruct((b, t, out_w), BF16),
        scratch_shapes=scratch,
        compiler_params=_cparams(("parallel", "arbitrary")),
        name=name,
    )(*args)


def _q_tile_spec(tq, width, col_block=0):
    return pl.BlockSpec((1, tq, width), lambda bi, i: (bi, i, col_block))


def _per_batch_spec(rows, width, col_block=0):
    return pl.BlockSpec((1, rows, width), lambda bi, i: (bi, 0, col_block))


def _const_spec(shape):
    return pl.BlockSpec(shape, lambda bi, i: (0,) * len(shape))


def _values_spec(layer, t):
    return pl.BlockSpec((None, None, BRANCH_W, t), lambda bi, i: (layer, bi, 1, 0))


def _fox_prompt(qkv, rows_t, layer, decay, consts, tq):
    b, t, _ = qkv.shape
    W = BRANCH_W
    wx = H_FOX * LANES
    place_q, place_k = consts
    scratch = [pltpu.VMEM((tq, wx), BF16), pltpu.VMEM((t, wx), BF16), pltpu.VMEM((wx, t), BF16),
               pltpu.VMEM((H_FOX, 1, tq), F32), pltpu.VMEM((H_FOX, LANES, tq), F32)]
    return _prompt_attention_call(
        _fox_prompt_kernel, "fox_prompt", (qkv, qkv, rows_t, decay, decay, place_q, place_k),
        [_q_tile_spec(tq, W, 0), _per_batch_spec(t, W, 1), _values_spec(layer, t),
         _q_tile_spec(tq, LANES), _per_batch_spec(t, LANES),
         _const_spec(place_q.shape), _const_spec(place_k.shape)], W, tq, scratch)


def _sb_prompt(qkv, rows_t, layer, tri, tq):
    b, t, _ = qkv.shape
    W = BRANCH_W
    scratch = [pltpu.VMEM((H_SB, tq, LANES), BF16), pltpu.VMEM((H_SB, 1, tq), F32),
               pltpu.VMEM((H_SB, HEAD_DIM, tq), F32)]
    return _prompt_attention_call(
        _sb_prompt_kernel, "sb_prompt", (qkv, qkv, rows_t, tri),
        [_q_tile_spec(tq, W, 3), _per_batch_spec(t, W, 4), _values_spec(layer, t),
         _const_spec(tri.shape)], W, tq, scratch)


def _diff_prompt(qkv, rows_t, layer, pad_qk, diff_lambda, subln_col, lambda_init, tq):
    b, t, _ = qkv.shape
    W = BRANCH_W
    chains = 2 * H_DIFF
    scratch = [pltpu.VMEM((t, DIFF_PAD_W), BF16), pltpu.VMEM((DIFF_PAD_W, t), BF16),
               pltpu.VMEM((chains, tq, LANES), BF16), pltpu.VMEM((chains, 1, tq), F32),
               pltpu.VMEM((chains, LANES, tq), F32)]
    return _prompt_attention_call(
        functools.partial(_diff_prompt_kernel, lambda_init=lambda_init), "diff_prompt",
        (qkv, qkv, rows_t, pad_qk, diff_lambda, subln_col),
        [_q_tile_spec(tq, W, 0), _per_batch_spec(t, W, 1), _values_spec(layer, t),
         _const_spec(pad_qk.shape), _const_spec(diff_lambda.shape), _const_spec(subln_col.shape)],
        DIFF_PAD_W, tq, scratch)


def _decode_kernel(pt_ref, qab_ref, qc_ref, lfnew_ref, dl_ref, subln_ref, tri_ref, *rest,
                   pages_per_step, lambda_init):
    del pt_ref
    P = pages_per_step
    fox_refs = rest[0:P]
    lf_refs = rest[P:2 * P]
    sb_refs = rest[2 * P:3 * P]
    diff_refs = rest[3 * P:4 * P]
    o_ref = rest[4 * P]
    (qa_s, qb_s, qd_s, m_a, l_a, acc_a, carry_a, acc_b, carry_b, m_c, l_c, acc_c) = rest[4 * P + 1:]
    W = BRANCH_W
    R = DEC_ROWS
    page = LANES
    step = pl.program_id(1)
    n_steps = pl.num_programs(1)
    lane = lax.broadcasted_iota(jnp.int32, (R, W), 1)
    rowi = lax.broadcasted_iota(jnp.int32, (R, W), 0)
    head64 = _in_group(lane, rowi, HEAD_DIM)
    comp48 = _in_group(lane, rowi, DIFF_DQ)
    tri_ones = tri_ref[...]

    def bcast(x):
        return jnp.broadcast_to(x, (R, W))

    @pl.when(step == 0)
    def _init():
        qab = qab_ref[0]
        qc = qc_ref[0]
        zero = jnp.zeros((R, W), F32)
        qa = jnp.where(head64, bcast(qab[:, 0:W].astype(F32)), zero)
        qb = jnp.where(head64, bcast(qab[:, 3 * W:4 * W].astype(F32)), zero)
        qd = jnp.where(comp48, bcast(qc[:, 0:W].astype(F32)), zero)
        qa_s[...] = qa.astype(BF16)
        qb_s[...] = qb.astype(BF16)
        qd_s[...] = qd.astype(BF16)
        ka = bcast(qab[:, W:2 * W].astype(F32))
        m_a[...] = jnp.broadcast_to(jnp.sum(qa * ka, axis=1, keepdims=True), (R, LANES))
        l_a[...] = jnp.ones((R, LANES), F32)
        acc_a[...] = bcast(qab[:, 2 * W:3 * W].astype(F32))
        lane_f = lax.broadcasted_iota(jnp.int32, (R, F_PAD), 1)
        row_f = lax.broadcasted_iota(jnp.int32, (R, F_PAD), 0)
        lf_col = jnp.sum(jnp.where(lane_f == row_f, jnp.broadcast_to(lfnew_ref[0], (R, F_PAD)), 0.0),
                         axis=1, keepdims=True)
        carry_a[...] = jnp.broadcast_to(lf_col, (R, LANES))
        acc_b[...] = jnp.zeros((R, W), F32)
        carry_b[...] = jnp.zeros((R, LANES), F32)
        kd = bcast(qc[:, W:2 * W].astype(F32))
        m_c[...] = jnp.broadcast_to(jnp.sum(qd * kd, axis=1, keepdims=True), (R, LANES))
        l_c[...] = jnp.ones((R, LANES), F32)
        acc_c[...] = bcast(qc[:, 2 * W:3 * W].astype(F32))

    def keys(ref):
        return ref[0:W, :].astype(BF16)

    def values(ref):
        return ref[W:2 * W, :].astype(BF16)

    def rows(x, j):
        return x[j * R:(j + 1) * R]

    def softmax_weights(s_parts, m_ref, l_ref):
        m = m_ref[...]
        block_max = s_parts[0]
        for s in s_parts[1:]:
            block_max = jnp.maximum(block_max, s)
        m_new = jnp.maximum(m, jnp.max(block_max, axis=1, keepdims=True))
        alpha = jnp.exp(m - m_new)
        p_parts = [jnp.exp(s - m_new) for s in s_parts]
        p_sum = p_parts[0]
        for p in p_parts[1:]:
            p_sum = p_sum + p
        l_ref[...] = alpha * l_ref[...] + jnp.sum(p_sum, axis=1, keepdims=True)
        m_ref[...] = m_new
        return alpha, p_parts

    def weighted_values(w_parts, v_refs):
        pv = None
        for j in range(P):
            term = _dot_nt(w_parts[j].astype(BF16), values(v_refs[j]))
            pv = term if pv is None else pv + term
        return pv

    z_a = [_dot(qa_s[...], keys(fox_refs[j])) for j in range(P)]
    z_b = jnp.concatenate([_dot(qb_s[...], keys(sb_refs[j])) for j in range(P)], axis=0)
    z_c = [_dot(qd_s[...], keys(diff_refs[j])) for j in range(P)]

    log_hit, log_fail = _log_hit_and_fail(z_b)
    no_head = jnp.zeros((R - LF_ROWS, page), F32)
    log_forget = jnp.concatenate(
        [part for j in range(P) for part in (lf_refs[j][...], no_head)], axis=0)
    hi, lo = _split_hi_lo(jnp.concatenate([log_fail, log_forget], axis=0))
    sums = _dot(hi, tri_ones) + _dot(lo, tri_ones)
    suffix_b, total_b = sums[0:P * R, 0:page], sums[0:P * R, page:2 * page]
    suffix_a, total_a = sums[P * R:2 * P * R, 0:page], sums[P * R:2 * P * R, page:2 * page]

    later_a = carry_a[...]
    later_b = carry_b[...]
    s_a, w_b = [], []
    for j in range(P):
        s_a.append(z_a[j] + rows(suffix_a, j) + later_a)
        w_b.append(jnp.exp(rows(log_hit, j) + rows(suffix_b, j) + later_b))
        later_a = later_a + rows(total_a, j)
        later_b = later_b + rows(total_b, j)
    carry_a[...] = later_a
    carry_b[...] = later_b

    alpha_a, p_a = softmax_weights(s_a, m_a, l_a)
    alpha_c, p_c = softmax_weights(z_c, m_c, l_c)
    widen = lambda a: jnp.tile(a, (1, W // LANES))
    acc_a[...] = widen(alpha_a) * acc_a[...] + weighted_values(p_a, fox_refs)
    acc_b[...] += weighted_values(w_b, sb_refs)
    acc_c[...] = widen(alpha_c) * acc_c[...] + weighted_values(p_c, diff_refs)

    @pl.when(step == n_steps - 1)
    def _finish():
        zero = jnp.zeros((R, W), F32)
        o_a = jnp.sum(jnp.where(head64, acc_a[...] / l_a[:, 0:1], zero), axis=0, keepdims=True)
        o_b = jnp.sum(jnp.where(head64, acc_b[...], zero), axis=0, keepdims=True)
        n_c = acc_c[...] / l_c[:, 0:1]
        odd_row = (rowi & 1) == 1
        head_lo = (rowi - (rowi & 1)) * DIFF_DQ
        own_head = (lane >= head_lo) & (lane < head_lo + DIFF_DV)
        first = jnp.sum(jnp.where(own_head & ~odd_row, n_c, zero), axis=0, keepdims=True)
        second = jnp.sum(jnp.where(own_head & odd_row, n_c, zero), axis=0, keepdims=True)
        o_c = first - _diff_lambda(dl_ref[...], lambda_init) * second
        lane1 = lax.broadcasted_iota(jnp.int32, (1, W), 1)
        inv = jnp.zeros((1, W), F32)
        for head in range(H_DIFF):
            in_head = _in_group(lane1, head, DIFF_DV)
            ms = jnp.sum(jnp.where(in_head, o_c * o_c, 0.0), axis=1, keepdims=True) * (1.0 / DIFF_DV)
            inv = jnp.where(in_head, lax.rsqrt(ms + SUBLN_EPS), inv)
        o_c = o_c * inv * subln_ref[...] * (1.0 - lambda_init)
        o_ref[0, :, 0:W] = o_a.astype(o_ref.dtype)
        o_ref[0, :, W:2 * W] = o_b.astype(o_ref.dtype)
        o_ref[0, :, 2 * W:3 * W] = o_c.astype(o_ref.dtype)


def _decode_attention(page_table, layer, qkv_ab, qkv_c, logf_new, diff_lambda, subln_row, tri_ones,
                      cache_fox, cache_lf_t, cache_sb, cache_diff, lambda_init, pages_per_step):
    nb, n_pages = page_table.shape
    P = pages_per_step
    assert n_pages % P == 0
    n_steps = n_pages // P
    page = cache_fox.shape[3]
    W = BRANCH_W
    R = DEC_ROWS

    def page_spec(shape, j):
        def index(b, s, pt):
            return (layer, pt[b, n_pages - 1 - (s * P + j)], 0, 0)
        return pl.BlockSpec((None, None) + shape, index)

    per_seq = lambda b, s, pt: (b, 0, 0)
    fixed = lambda b, s, pt: (0, 0)
    in_specs = [
        pl.BlockSpec((1, 1, 6 * W), per_seq),
        pl.BlockSpec((1, 1, 3 * W), per_seq),
        pl.BlockSpec((1, 1, F_PAD), per_seq),
        pl.BlockSpec(diff_lambda.shape, fixed),
        pl.BlockSpec((1, W), fixed),
        pl.BlockSpec(tri_ones.shape, fixed),
    ]
    in_specs += [page_spec((2 * W, page), j) for j in range(P)]
    in_specs += [page_spec((LF_ROWS, page), j) for j in range(P)]
    in_specs += [page_spec((2 * W, page), j) for j in range(P)]
    in_specs += [page_spec((2 * W, page), j) for j in range(P)]
    scratch = [pltpu.VMEM((R, W), BF16)] * 3 + [
        pltpu.VMEM((R, LANES), F32), pltpu.VMEM((R, LANES), F32), pltpu.VMEM((R, W), F32),
        pltpu.VMEM((R, LANES), F32),
        pltpu.VMEM((R, W), F32), pltpu.VMEM((R, LANES), F32),
        pltpu.VMEM((R, LANES), F32), pltpu.VMEM((R, LANES), F32), pltpu.VMEM((R, W), F32),
    ]
    grid_spec = pltpu.PrefetchScalarGridSpec(
        num_scalar_prefetch=1, grid=(nb, n_steps), in_specs=in_specs,
        out_specs=pl.BlockSpec((1, 1, 3 * W), per_seq), scratch_shapes=scratch)
    out = pl.pallas_call(
        functools.partial(_decode_kernel, pages_per_step=P, lambda_init=lambda_init),
        grid_spec=grid_spec,
        out_shape=jax.ShapeDtypeStruct((nb, 1, 3 * W), BF16),
        compiler_params=_cparams(("parallel", "arbitrary")),
        name="decode_attention",
    )(page_table, qkv_ab.reshape(nb, 1, 6 * W), qkv_c.reshape(nb, 1, 3 * W),
      logf_new.reshape(nb, 1, F_PAD), diff_lambda, subln_row, tri_ones,
      *([cache_fox] * P), *([cache_lf_t] * P), *([cache_sb] * P), *([cache_diff] * P))
    return out.reshape(nb, 3 * W)


def _merge_kernel(x_ref, oa_ref, ob_ref, oc_ref, gpre_ref, wg_ref, wa_ref, wb_ref, wc_ref,
                  wout_ref, gpost_ref, y_ref):
    x = x_ref[...]
    d = x.shape[1]
    h = _rms_norm(x, gpre_ref[...], NORM_EPS).astype(BF16)
    merged = None
    for n, (o_ref, w_ref) in enumerate(((oa_ref, wa_ref), (ob_ref, wb_ref), (oc_ref, wc_ref))):
        gate = jax.nn.sigmoid(_dot(h, wg_ref[:, n * d:(n + 1) * d]))
        term = gate * _dot(o_ref[...], w_ref[...])
        merged = term if merged is None else merged + term
    mix = _dot(merged.astype(BF16), wout_ref[...])
    y_ref[...] = x + _rms_norm(mix, gpost_ref[...], NORM_EPS)


def _merge(x, o_a, o_b, o_c, gpre, wg, wa, wb, wc, wout, gpost, tm):
    n, d = x.shape
    row = lambda i: (i, 0)
    fixed = lambda i: (0, 0)
    full = lambda a: pl.BlockSpec(a.shape, fixed)
    return pl.pallas_call(
        _merge_kernel,
        grid=(n // tm,),
        in_specs=[pl.BlockSpec((tm, d), row), pl.BlockSpec((tm, o_a.shape[1]), row),
                  pl.BlockSpec((tm, o_b.shape[1]), row), pl.BlockSpec((tm, o_c.shape[1]), row),
                  full(gpre), full(wg), full(wa), full(wb), full(wc), full(wout), full(gpost)],
        out_specs=pl.BlockSpec((tm, d), row),
        out_shape=jax.ShapeDtypeStruct((n, d), F32),
        compiler_params=_cparams(("parallel",)),
        name="merge",
    )(x, o_a, o_b, o_c, gpre, wg, wa, wb, wc, wout, gpost)


def _ffn_kernel(x_ref, gpre_ref, wgu_ref, wdown_ref, gpost_ref, y_ref, act_ref, *, chunk):
    x = x_ref[...]
    d_ff = wdown_ref.shape[0]
    h = _rms_norm(x, gpre_ref[...], NORM_EPS).astype(BF16)
    for c0 in range(0, d_ff, chunk):
        gate = _dot(h, wgu_ref[:, c0:c0 + chunk])
        up = _dot(h, wgu_ref[:, d_ff + c0:d_ff + c0 + chunk])
        act_ref[:, c0:c0 + chunk] = (gate * jax.nn.sigmoid(gate) * up).astype(BF16)
    ffn = _dot(act_ref[...], wdown_ref[...])
    y_ref[...] = x + _rms_norm(ffn, gpost_ref[...], NORM_EPS)


def _ffn(x, gpre, wgu, wdown, gpost, tm, chunk):
    n, d = x.shape
    d_ff = wdown.shape[0]
    assert d_ff % chunk == 0
    row = lambda i: (i, 0)
    fixed = lambda i: (0, 0)
    full = lambda a: pl.BlockSpec(a.shape, fixed)
    return pl.pallas_call(
        functools.partial(_ffn_kernel, chunk=chunk),
        grid=(n // tm,),
        in_specs=[pl.BlockSpec((tm, d), row), full(gpre), full(wgu), full(wdown), full(gpost)],
        out_specs=pl.BlockSpec((tm, d), row),
        out_shape=jax.ShapeDtypeStruct((n, d), F32),
        scratch_shapes=[pltpu.VMEM((tm, d_ff), BF16)],
        compiler_params=_cparams(("parallel",)),
        name="ffn",
    )(x, gpre, wgu, wdown, gpost)


def _lambda_init(layer):
    return 0.8 - 0.6 * math.exp(-0.3 * layer)


def _rotary_tables(pos):
    half = ROPE_DIM // 2
    inv_freq = ROPE_THETA ** (-jnp.arange(0, ROPE_DIM, 2, dtype=F32) / ROPE_DIM)
    ang = pos.astype(F32)[:, None] * inv_freq[None, :]
    cos, sin = jnp.cos(ang), jnp.sin(ang)
    n = pos.shape[0]
    rest = DIFF_DQ - ROPE_DIM
    ones, zeros = jnp.ones((n, rest), F32), jnp.zeros((n, rest), F32)
    zh = jnp.zeros((n, half), F32)
    reps = BRANCH_W // DIFF_DQ
    cos_t = jnp.tile(jnp.concatenate([cos, cos, ones], axis=1), (1, reps))
    sina_t = jnp.tile(jnp.concatenate([-sin, zh, zeros], axis=1), (1, reps))
    sinb_t = jnp.tile(jnp.concatenate([zh, sin, zeros], axis=1), (1, reps))
    return cos_t, sina_t, sinb_t


def _layer_params(l, norm_mix_pre, norm_mix_post, norm_ffn_pre, norm_ffn_post, w_in, b_forget,
                  diff_lambda, diff_subln, w_branch, w_out, w_gate_up, w_down):
    W = BRANCH_W
    d = w_in.shape[1]
    w = w_in[l]
    q_scale = HEAD_DIM ** -0.5
    cols = [w[:, 0:W] * q_scale, w[:, W:3 * W], w[:, 3 * W:4 * W] * q_scale, w[:, 4 * W:9 * W],
            jnp.pad(w[:, 9 * W:9 * W + H_FOX], ((0, 0), (0, F_PAD - H_FOX)))]
    w_c = w_branch[l, 2]
    w_c_pad = jnp.pad(w_c.reshape(H_DIFF, DIFF_DV, d), ((0, 0), (0, LANES - DIFF_DV), (0, 0)))
    row = lambda v: v.reshape(1, -1).astype(F32)
    return dict(
        w_qkv=jnp.concatenate(cols, axis=1).astype(BF16),
        b_f=jnp.pad(b_forget[l], (0, F_PAD - H_FOX)).reshape(1, F_PAD).astype(F32),
        w_gate=w[:, 9 * W + H_FOX:].astype(BF16),
        w_a=w_branch[l, 0].astype(BF16), w_b=w_branch[l, 1].astype(BF16),
        w_c=w_c.astype(BF16), w_c_pad=w_c_pad.reshape(DIFF_PAD_W, d).astype(BF16),
        w_out=w_out[l].astype(BF16), w_gu=w_gate_up[l].astype(BF16), w_down=w_down[l].astype(BF16),
        g_mix_pre=row(norm_mix_pre[l]), g_mix_post=row(norm_mix_post[l]),
        g_ffn_pre=row(norm_ffn_pre[l]), g_ffn_post=row(norm_ffn_post[l]),
        diff_lambda=diff_lambda[l].astype(F32),
        subln=row(jnp.tile(diff_subln[l], H_DIFF)),
        subln_col=jnp.pad(diff_subln[l], (0, LANES - DIFF_DV)).reshape(LANES, 1).astype(F32),
        lambda_init=_lambda_init(l),
    )


def _strict_lower_ones(n):
    j = lax.broadcasted_iota(jnp.int32, (n, n), 0)
    s = lax.broadcasted_iota(jnp.int32, (n, n), 1)
    return (j > s).astype(BF16)


N_PIECES = 3
ONES_LANE = H_FOX * N_PIECES


def _fox_decay_pieces(c):
    b, t, _ = c.shape
    hi = _bf16_prefix(c)
    mid = _bf16_prefix(c - hi)
    lo = c - hi - mid
    pieces = jnp.stack([hi, mid, lo], axis=-1).reshape(b, t, ONES_LANE)
    out = jnp.concatenate([pieces, jnp.ones((b, t, 1), F32),
                           jnp.zeros((b, t, LANES - ONES_LANE - 1), F32)], axis=-1)
    return out.astype(BF16)


def _fox_placements():
    n_in = H_FOX * HEAD_DIM + LANES
    place_q = np.zeros((n_in, H_FOX * LANES), np.float32)
    place_k = np.zeros((n_in, H_FOX * LANES), np.float32)
    pieces0 = H_FOX * HEAD_DIM
    for h in range(H_FOX):
        for d in range(HEAD_DIM):
            place_q[h * HEAD_DIM + d, h * LANES + d] = 1.0
            place_k[h * HEAD_DIM + d, h * LANES + d] = 1.0
        for j in range(N_PIECES):
            place_q[pieces0 + N_PIECES * h + j, h * LANES + HEAD_DIM + j] = 1.0
            place_q[pieces0 + ONES_LANE, h * LANES + HEAD_DIM + N_PIECES + j] = -1.0
            place_k[pieces0 + ONES_LANE, h * LANES + HEAD_DIM + j] = 1.0
            place_k[pieces0 + N_PIECES * h + j, h * LANES + HEAD_DIM + N_PIECES + j] = 1.0
    return jnp.asarray(place_q, BF16), jnp.asarray(place_k, BF16)


def _diff_placement():
    pad_qk = np.zeros((BRANCH_W, DIFF_PAD_W), np.float32)
    for h in range(H_DIFF):
        for c in range(2):
            for d in range(DIFF_DQ):
                pad_qk[h * DIFF_DV + c * DIFF_DQ + d, h * LANES + c * HEAD_DIM + d] = 1.0
    return jnp.asarray(pad_qk, BF16)


def _page_major_cache(cache):
    depth, n_phys, page = cache.shape[:3]
    return jnp.transpose(cache, (0, 1, 3, 4, 5, 2)).reshape(depth, n_phys, -1, page)


def _ffn_chunk(d_ff):
    for chunk in (512, 256, 128):
        if d_ff % chunk == 0:
            return chunk
    return d_ff


def _pages_per_step(n_pages):
    for p in (8, 4, 2):
        if n_pages % p == 0:
            return p
    return 1


def kernel(x_prompt, x_sample, cache_fox_kv, cache_fox_logf, cache_sb_kv, cache_diff_kv, page_table,
           norm_mix_pre, norm_mix_post, norm_ffn_pre, norm_ffn_post, w_in, b_forget, diff_lambda,
           diff_subln, w_branch, w_out, w_gate_up, w_down):
    depth = w_in.shape[0]
    b, t, d = x_prompt.shape
    nb, dec_t, _ = x_sample.shape
    assert dec_t == 1
    page = cache_fox_kv.shape[2]
    n_pages = page_table.shape[1]
    past_len = n_pages * page
    W = BRANCH_W
    d_ff = w_down.shape[1]

    tq = min(256, t)
    tm = min(512, t)
    assert t % tq == 0 and t % tm == 0 and page == LANES
    pages_per_step = _pages_per_step(n_pages)
    chunk = _ffn_chunk(d_ff)

    params = [_layer_params(l, norm_mix_pre, norm_mix_post, norm_ffn_pre, norm_ffn_post, w_in,
                            b_forget, diff_lambda, diff_subln, w_branch, w_out, w_gate_up, w_down)
              for l in range(depth)]
    tab_p = _rotary_tables(jnp.arange(t))
    tab_s = tuple(jnp.broadcast_to(a, (nb, W)) for a in _rotary_tables(jnp.full((1,), past_len)))
    tri_q = _strict_lower_ones(tq).T
    fox_consts = _fox_placements()
    diff_pad = _diff_placement()
    tri_ones_page = jnp.concatenate([_strict_lower_ones(page), jnp.ones((page, page), BF16)], axis=1)

    cache_fox = _page_major_cache(cache_fox_kv)
    cache_sb = _page_major_cache(cache_sb_kv)
    cache_diff = _page_major_cache(cache_diff_kv)
    cache_lf_t = jnp.pad(jnp.swapaxes(cache_fox_logf, 2, 3),
                         ((0, 0), (0, 0), (0, LF_ROWS - H_FOX), (0, 0)))

    def tail(x, o_a, o_b, o_c, w_c, p, rows):
        x = _merge(x, o_a, o_b, o_c, p["g_mix_pre"], p["w_gate"], p["w_a"], p["w_b"], w_c,
                   p["w_out"], p["g_mix_post"], rows)
        return _ffn(x, p["g_ffn_pre"], p["w_gu"], p["w_down"], p["g_ffn_post"], rows, chunk)

    x = x_prompt.reshape(b * t, d)
    rows_p, logf_p = None, []
    for l, p in enumerate(params):
        qkv_ab, rows_p, qkv_c, logf = _inproj(
            x, p["g_mix_pre"], p["w_qkv"], p["b_f"], *tab_p, tm, l, depth, rows_p, seq_len=t)
        logf6 = logf[:, :H_FOX].reshape(b, t, H_FOX)
        logf_p.append(logf6)
        c_t = _cumsum_time(jnp.pad(jnp.swapaxes(logf6, 1, 2), ((0, 0), (0, 8 - H_FOX), (0, 0))))
        c = jnp.swapaxes(c_t, 1, 2)[..., :H_FOX]
        qkv_ab3 = qkv_ab.reshape(b, t, 6 * W)
        qkv_c3 = qkv_c.reshape(b, t, 3 * W)
        fox_t, sb_t, diff_t = rows_p
        o_a = _fox_prompt(qkv_ab3, fox_t, l, _fox_decay_pieces(c), fox_consts, tq)
        o_b = _sb_prompt(qkv_ab3, sb_t, l, tri_q, tq)
        o_c = _diff_prompt(qkv_c3, diff_t, l, diff_pad, p["diff_lambda"], p["subln_col"],
                           p["lambda_init"], tq)
        x = tail(x, o_a.reshape(b * t, W), o_b.reshape(b * t, W), o_c.reshape(b * t, DIFF_PAD_W),
                 p["w_c_pad"], p, tm)
    y_prompt = x.reshape(b, t, d)

    x = x_sample.reshape(nb, d)
    rows_s, logf_s = None, []
    for l, p in enumerate(params):
        qkv_ab, rows_s, qkv_c, logf = _inproj(
            x, p["g_mix_pre"], p["w_qkv"], p["b_f"], *tab_s, nb, l, depth, rows_s)
        logf_s.append(logf[:, :H_FOX].reshape(nb, 1, H_FOX))
        o = _decode_attention(page_table, l, qkv_ab, qkv_c, logf, p["diff_lambda"], p["subln"],
                              tri_ones_page, cache_fox, cache_lf_t, cache_sb, cache_diff,
                              p["lambda_init"], pages_per_step)
        x = tail(x, o[:, 0:W], o[:, W:2 * W], o[:, 2 * W:3 * W], p["w_c"], p, nb)
    y_sample = x.reshape(nb, 1, d)

    def token_major(rows_t, heads, dim):
        return jnp.transpose(rows_t.reshape(depth, b, 2, heads, dim, t), (0, 1, 5, 2, 3, 4))

    fox_p = token_major(rows_p[0], H_FOX, HEAD_DIM)
    sb_p = token_major(rows_p[1], H_SB, HEAD_DIM)
    diff_p = token_major(rows_p[2], H_DIFF, DIFF_DV)
    fox_s = rows_s[0].reshape(depth, nb, 1, 2, H_FOX, HEAD_DIM)
    sb_s = rows_s[1].reshape(depth, nb, 1, 2, H_SB, HEAD_DIM)
    diff_s = rows_s[2].reshape(depth, nb, 1, 2, H_DIFF, DIFF_DV)
    return (y_prompt, y_sample, fox_p, jnp.stack(logf_p), sb_p, diff_p,
            fox_s, jnp.stack(logf_s), sb_s, diff_s)
```

```python
import functools
import math

import jax
import jax.numpy as jnp
import numpy as np
from jax import lax
from jax.experimental import pallas as pl
from jax.experimental.pallas import tpu as pltpu

F32 = jnp.float32
BF16 = jnp.bfloat16

HEAD_DIM = 64
H_FOX = 6
H_SB = 6
H_DIFF = 4
DIFF_DQ = 48
DIFF_DV = 2 * DIFF_DQ
BRANCH_W = 384
N_BRANCH = 3
ROPE_DIM = DIFF_DQ // 4
ROPE_THETA = 500000.0
NORM_EPS = 1e-6
SUBLN_EPS = 1e-5

LANES = 128
HEADS_PER_LANE_BLOCK = LANES // HEAD_DIM
F_PAD = LANES
DIFF_PAD_W = H_DIFF * LANES
NEG_BIG = -1e30
EXP_IS_ZERO = -104.0
VMEM_LIMIT = 56 * 1024 * 1024
DEC_ROWS = 16
LF_ROWS = 8
STAGE_LAG = 4
PREP_ROWS = 512


def _cparams(sem):
    return pltpu.CompilerParams(dimension_semantics=sem, vmem_limit_bytes=VMEM_LIMIT)


def _rms_norm(x, gain, eps):
    ms = jnp.mean(x * x, axis=-1, keepdims=True)
    return x * lax.rsqrt(ms + eps) * gain


def _softplus_neg_abs(z):
    return jnp.log1p(jnp.exp(-jnp.abs(z)))


def _log_hit_and_fail(z):
    log_fail = jnp.minimum(-z, 0.0) - jnp.log(1.0 + jnp.exp(-jnp.abs(z)))
    return log_fail + z, log_fail


def _bf16_prefix(x):
    bits = lax.bitcast_convert_type(x, jnp.uint32) & jnp.uint32(0xFFFF0000)
    return lax.bitcast_convert_type(bits, F32)


def _split_hi_lo(x):
    hi = _bf16_prefix(x)
    return hi.astype(BF16), (x - hi).astype(BF16)


def _in_group(index, group, width):
    lo = group * width
    return (index >= lo) & (index < lo + width)


def _dot(a, b):
    return jnp.dot(a, b, preferred_element_type=F32)


def _dot_nt(a, b):
    return lax.dot_general(a, b, (((1,), (1,)), ((), ())), preferred_element_type=F32)


def _inproj_kernel(x_ref, gain_ref, w_ref, bf_ref, cos_ref, sina_ref, sinb_ref, *rest, rows_transposed):
    qkv_ab_ref, foxkv_ref, sbkv_ref, diffkv_ref, qkv_c_ref, logf_ref = rest[-6:]
    W = BRANCH_W
    h = _rms_norm(x_ref[...], gain_ref[...], NORM_EPS).astype(BF16)

    def proj(c):
        return _dot(h, w_ref[:, c * W:(c + 1) * W])

    def rotary(r):
        return (r * cos_ref[...] + pltpu.roll(r, W - ROPE_DIM // 2, 1) * sina_ref[...]
                + pltpu.roll(r, ROPE_DIM // 2, 1) * sinb_ref[...])

    def store_rows(kv_ref, j, r):
        if rows_transposed:
            kv_ref[j * W:(j + 1) * W, :] = r.T
        else:
            kv_ref[:, j * W:(j + 1) * W] = r

    for c, kv_ref in ((0, foxkv_ref), (3, sbkv_ref)):
        qkv_ab_ref[:, c * W:(c + 1) * W] = proj(c).astype(BF16)
        for j in range(2):
            r = proj(c + 1 + j)
            qkv_ab_ref[:, (c + 1 + j) * W:(c + 2 + j) * W] = r.astype(BF16)
            store_rows(kv_ref, j, r)

    qc = rotary(proj(6)) * (DIFF_DQ ** -0.5)
    qkv_c_ref[:, 0:W] = qc.astype(BF16)
    kc = rotary(proj(7))
    qkv_c_ref[:, W:2 * W] = kc.astype(BF16)
    store_rows(diffkv_ref, 0, kc)
    vc = proj(8)
    qkv_c_ref[:, 2 * W:3 * W] = vc.astype(BF16)
    store_rows(diffkv_ref, 1, vc)

    f = _dot(h, w_ref[:, 9 * W:9 * W + F_PAD]) + bf_ref[...]
    logf_ref[...] = jnp.minimum(f, 0.0) - _softplus_neg_abs(f)


def _inproj(x, gain, w, bf, cos_t, sina_t, sinb_t, tm, layer, depth, rows_so_far, seq_len=None):
    n, d = x.shape
    W = BRANCH_W
    n_tab = cos_t.shape[0] // tm
    row = lambda i: (i, 0)
    fixed = lambda i: (0, 0)
    tab = lambda i: (i % n_tab, 0)
    if seq_len is None:
        slab = pl.BlockSpec((None, tm, 2 * W), lambda i: (layer, i, 0))
        rows_shape = jax.ShapeDtypeStruct((depth, n, 2 * W), F32)
    else:
        assert seq_len == cos_t.shape[0]
        slab = pl.BlockSpec((None, None, 2 * W, tm), lambda i: (layer, i // n_tab, 0, i % n_tab))
        rows_shape = jax.ShapeDtypeStruct((depth, n // seq_len, 2 * W, seq_len), F32)
    out_shape = (
        jax.ShapeDtypeStruct((n, 6 * W), BF16), rows_shape, rows_shape, rows_shape,
        jax.ShapeDtypeStruct((n, 3 * W), BF16),
        jax.ShapeDtypeStruct((n, F_PAD), F32),
    )
    in_specs = [
        pl.BlockSpec((tm, d), row),
        pl.BlockSpec((1, d), fixed),
        pl.BlockSpec(w.shape, fixed),
        pl.BlockSpec((1, F_PAD), fixed),
        pl.BlockSpec((tm, W), tab),
        pl.BlockSpec((tm, W), tab),
        pl.BlockSpec((tm, W), tab),
    ]
    args = [x, gain, w, bf, cos_t, sina_t, sinb_t]
    aliases = {}
    if rows_so_far is None:
        rows_so_far = tuple(jnp.zeros(rows_shape.shape, F32) for _ in range(3))
    for j, buf in enumerate(rows_so_far):
        aliases[len(args)] = 1 + j
        args.append(buf)
        in_specs.append(pl.BlockSpec(memory_space=pl.ANY))
    qkv_ab, fox_rows, sb_rows, diff_rows, qkv_c, logf = pl.pallas_call(
        functools.partial(_inproj_kernel, rows_transposed=seq_len is not None),
        grid=(n // tm,),
        in_specs=in_specs,
        out_specs=(pl.BlockSpec((tm, 6 * W), row), slab, slab, slab,
                   pl.BlockSpec((tm, 3 * W), row), pl.BlockSpec((tm, F_PAD), row)),
        out_shape=out_shape,
        input_output_aliases=aliases,
        compiler_params=_cparams(("parallel",)),
        name="inproj",
    )(*args)
    return qkv_ab, (fox_rows, sb_rows, diff_rows), qkv_c, logf


def _cumsum_kernel(x_ref, o_ref):
    x = x_ref[0]
    t = x.shape[-1]
    lane = lax.broadcasted_iota(jnp.int32, x.shape, 1)
    shift = 1
    while shift < t:
        x = x + jnp.where(lane >= shift, pltpu.roll(x, shift, 1), 0.0)
        shift *= 2
    o_ref[0] = x


def _cumsum_time(logf_t):
    b, r, t = logf_t.shape
    blk = pl.BlockSpec((1, r, t), lambda i: (i, 0, 0))
    return pl.pallas_call(
        _cumsum_kernel, grid=(b,), in_specs=[blk], out_specs=blk,
        out_shape=jax.ShapeDtypeStruct(logf_t.shape, F32),
        compiler_params=_cparams(("parallel",)), name="cumsum_logf",
    )(logf_t)


def _lane_block(ref, rows, blk):
    return ref[0, rows, blk * LANES:(blk + 1) * LANES]


def _softmax_update(s, vt_ext, m_ref, acc_ref):
    m = m_ref[...]
    m_new = jnp.maximum(m, jnp.max(s, axis=0, keepdims=True))
    p = jnp.exp(s - m_new)
    acc_ref[...] = jnp.exp(m - m_new) * acc_ref[...] + _dot(vt_ext, p.astype(BF16))
    m_ref[...] = m_new


def _staggered(n_chains, stages, lag):
    state = [None] * n_chains
    for step in range(n_chains + (len(stages) - 1) * lag):
        for k, stage in enumerate(stages):
            c = step - k * lag
            if 0 <= c < n_chains:
                state[c] = stage(c, state[c])


def _key_query_iota(tq):
    key = lax.broadcasted_iota(jnp.int32, (tq, tq), 0)
    query = lax.broadcasted_iota(jnp.int32, (tq, tq), 1)
    return key, query


def _for_key_blocks(i, tq, block, newest_first, still_live=None):
    def visit(step):
        kb = (i - 1 - step) if newest_first else step
        block(pl.multiple_of(kb * tq, tq), False)

    def body(step, carry):
        visit(step)
        return carry

    q0 = pl.multiple_of(i * tq, tq)
    if not newest_first:
        lax.fori_loop(0, i, body, 0)
        block(q0, True)
    elif still_live is None:
        block(q0, True)
        lax.fori_loop(0, i, body, 0)
    else:
        block(q0, True)

        def live_body(carry):
            visit(carry[0])
            return carry[0] + 1, still_live()

        lax.while_loop(lambda carry: (carry[0] < i) & carry[1], live_body, (0, still_live()))


def _fox_prompt_kernel(q_ref, k_ref, vt_ref, cq_ref, ck_ref, place_q_ref, place_k_ref,
                       o_ref, qx_s, kx_s, vt_s, m_s, acc_s, *, tq):
    i = pl.program_id(1)
    t = k_ref.shape[1]
    key, query = _key_query_iota(tq)
    causal = key <= query

    @pl.when(i == 0)
    def _per_batch_row():
        for r0 in range(0, t, PREP_ROWS):
            rows = slice(r0, min(r0 + PREP_ROWS, t))
            kx = _dot(jnp.concatenate([k_ref[0, rows, :], ck_ref[0, rows, :]], axis=1), place_k_ref[...])
            kx_s[rows, :] = kx.astype(BF16)
        for head in range(H_FOX):
            vt_s[head * LANES:head * LANES + HEAD_DIM, :] = (
                vt_ref[head * HEAD_DIM:(head + 1) * HEAD_DIM, :].astype(BF16))
            vt_s[head * LANES + HEAD_DIM:(head + 1) * LANES, :] = jnp.ones((LANES - HEAD_DIM, t), BF16)

    qx = _dot(jnp.concatenate([q_ref[0], cq_ref[0]], axis=1), place_q_ref[...])
    qx_s[...] = qx.astype(BF16)
    for head in range(H_FOX):
        m_s[head] = jnp.full((1, tq), NEG_BIG, F32)
        acc_s[head] = jnp.zeros((LANES, tq), F32)

    def block(k0, diagonal):
        def scores(head, _):
            k = kx_s[pl.ds(k0, tq), head * LANES:(head + 1) * LANES]
            s = _dot_nt(k, qx_s[:, head * LANES:(head + 1) * LANES])
            return jnp.where(causal, s, NEG_BIG) if diagonal else s

        def update(head, s):
            vt = vt_s[head * LANES:(head + 1) * LANES, pl.ds(k0, tq)]
            _softmax_update(s, vt, m_s.at[head], acc_s.at[head])

        _staggered(H_FOX, (scores, update), STAGE_LAG)

    _for_key_blocks(i, tq, block, newest_first=False)
    for blk in range(H_FOX // HEADS_PER_LANE_BLOCK):
        halves = []
        for sub in range(HEADS_PER_LANE_BLOCK):
            a = acc_s[blk * HEADS_PER_LANE_BLOCK + sub]
            halves.append(a[0:HEAD_DIM] / a[HEAD_DIM:HEAD_DIM + 1])
        o_t = jnp.concatenate(halves, axis=0)
        o_ref[0, :, blk * LANES:(blk + 1) * LANES] = o_t.T.astype(o_ref.dtype)


def _sb_prompt_kernel(q_ref, k_ref, vt_ref, tri_ref, o_ref, qh_s, later_s, acc_s, *, tq):
    i = pl.program_id(1)
    lane = lax.broadcasted_iota(jnp.int32, (tq, LANES), 1)
    key, query = _key_query_iota(tq)
    strict = key < query
    for head in range(H_SB):
        blk, sub = divmod(head, HEADS_PER_LANE_BLOCK)
        q_pair = _lane_block(q_ref, slice(None), blk)
        qh_s[head] = jnp.where(_in_group(lane, sub, HEAD_DIM), q_pair, jnp.zeros_like(q_pair))
        later_s[head] = jnp.zeros((1, tq), F32)
        acc_s[head] = jnp.zeros((HEAD_DIM, tq), F32)

    def block(k0, diagonal):
        def logits(head, _):
            k = _lane_block(k_ref, pl.ds(k0, tq), head // HEADS_PER_LANE_BLOCK)
            log_hit, log_fail = _log_hit_and_fail(_dot_nt(k, qh_s[head]))
            if diagonal:
                log_fail = jnp.where(strict, log_fail, 0.0)
            return log_hit, _split_hi_lo(log_fail), jnp.sum(log_fail, axis=0, keepdims=True)

        def suffix_sums(head, state):
            log_hit, (hi, lo), fail_sum = state
            tri = tri_ref[...]
            return log_hit + (_dot(tri, hi) + _dot(tri, lo)), fail_sum

        def accumulate(head, state):
            log_w, fail_sum = state
            vt = vt_ref[head * HEAD_DIM:(head + 1) * HEAD_DIM, pl.ds(k0, tq)].astype(BF16)
            w = jnp.exp(log_w + later_s[head])
            if diagonal:
                w = jnp.where(strict, w, 0.0)
            acc_s[head] += _dot(vt, w.astype(BF16))
            later_s[head] += fail_sum

        _staggered(H_SB, (logits, suffix_sums, accumulate), STAGE_LAG)

    def still_live():
        return jnp.max(later_s[...]) > EXP_IS_ZERO

    _for_key_blocks(i, tq, block, newest_first=True, still_live=still_live)
    for blk in range(H_SB // HEADS_PER_LANE_BLOCK):
        o_t = jnp.concatenate([acc_s[blk * HEADS_PER_LANE_BLOCK + sub]
                               for sub in range(HEADS_PER_LANE_BLOCK)], axis=0)
        o_ref[0, :, blk * LANES:(blk + 1) * LANES] = o_t.T.astype(o_ref.dtype)


def _diff_lambda(dl, lambda_init):
    l1 = jnp.sum(dl[0:1] * dl[1:2], axis=1, keepdims=True)
    l2 = jnp.sum(dl[2:3] * dl[3:4], axis=1, keepdims=True)
    return jnp.exp(l1) - jnp.exp(l2) + lambda_init


def _diff_prompt_kernel(q_ref, k_ref, vt_ref, pad_qk_ref, dl_ref, subln_ref, o_ref,
                        kp_s, vt_s, qh_s, m_s, acc_s, *, tq, lambda_init):
    i = pl.program_id(1)
    t = k_ref.shape[1]
    lane = lax.broadcasted_iota(jnp.int32, (tq, LANES), 1)
    key, query = _key_query_iota(tq)
    causal = key <= query

    @pl.when(i == 0)
    def _per_batch_row():
        for r0 in range(0, t, PREP_ROWS):
            rows = slice(r0, min(r0 + PREP_ROWS, t))
            kp_s[rows, :] = _dot(k_ref[0, rows, :], pad_qk_ref[...]).astype(BF16)
        for head in range(H_DIFF):
            vt_s[head * LANES:head * LANES + DIFF_DV, :] = (
                vt_ref[head * DIFF_DV:(head + 1) * DIFF_DV, :].astype(BF16))
            vt_s[head * LANES + DIFF_DV:(head + 1) * LANES, :] = jnp.ones((LANES - DIFF_DV, t), BF16)

    q_pad = _dot(q_ref[0], pad_qk_ref[...]).astype(BF16)
    for head in range(H_DIFF):
        q_head = q_pad[:, head * LANES:(head + 1) * LANES]
        for comp in range(2):
            c = 2 * head + comp
            qh_s[c] = jnp.where(_in_group(lane, comp, HEAD_DIM), q_head, jnp.zeros_like(q_head))
            m_s[c] = jnp.full((1, tq), NEG_BIG, F32)
            acc_s[c] = jnp.zeros((LANES, tq), F32)

    def block(k0, diagonal):
        def scores(c, _):
            head = c // 2
            s = _dot_nt(kp_s[pl.ds(k0, tq), head * LANES:(head + 1) * LANES], qh_s[c])
            return jnp.where(causal, s, NEG_BIG) if diagonal else s

        def update(c, s):
            head = c // 2
            vt = vt_s[head * LANES:(head + 1) * LANES, pl.ds(k0, tq)]
            _softmax_update(s, vt, m_s.at[c], acc_s.at[c])

        _staggered(2 * H_DIFF, (scores, update), STAGE_LAG)

    _for_key_blocks(i, tq, block, newest_first=False)
    lam = _diff_lambda(dl_ref[...], lambda_init)
    gain = subln_ref[0:DIFF_DV, :] * (1.0 - lambda_init)
    for head in range(H_DIFF):
        a0 = acc_s[2 * head]
        a1 = acc_s[2 * head + 1]
        o = (a0[0:DIFF_DV] / a0[DIFF_DV:DIFF_DV + 1]
             - lam * (a1[0:DIFF_DV] / a1[DIFF_DV:DIFF_DV + 1]))
        ms = jnp.sum(o * o, axis=0, keepdims=True) * (1.0 / DIFF_DV)
        o = o * lax.rsqrt(ms + SUBLN_EPS) * gain
        o_t = jnp.concatenate([o, jnp.zeros((LANES - DIFF_DV, tq), F32)], axis=0)
        o_ref[0, :, head * LANES:(head + 1) * LANES] = o_t.T.astype(o_ref.dtype)


def _prompt_attention_call(kernel, name, args, in_specs, out_w, tq, scratch):
    b, t = args[0].shape[0], args[0].shape[1]
    return pl.pallas_call(
        functools.partial(kernel, tq=tq),
        grid=(b, t // tq),
        in_specs=in_specs,
        out_specs=pl.BlockSpec((1, tq, out_w), lambda bi, i: (bi, i, 0)),
        out_shape=jax.ShapeDtypeStruct((b, t, out_w), BF16),
        scratch_shapes=scratch,
        compiler_params=_cparams(("parallel", "arbitrary")),
        name=name,
    )(*args)


def _q_tile_spec(tq, width, col_block=0):
    return pl.BlockSpec((1, tq, width), lambda bi, i: (bi, i, col_block))


def _per_batch_spec(rows, width, col_block=0):
    return pl.BlockSpec((1, rows, width), lambda bi, i: (bi, 0, col_block))


def _const_spec(shape):
    return pl.BlockSpec(shape, lambda bi, i: (0,) * len(shape))


def _values_spec(layer, t):
    return pl.BlockSpec((None, None, BRANCH_W, t), lambda bi, i: (layer, bi, 1, 0))


def _fox_prompt(qkv, rows_t, layer, decay, consts, tq):
    b, t, _ = qkv.shape
    W = BRANCH_W
    wx = H_FOX * LANES
    place_q, place_k = consts
    scratch = [pltpu.VMEM((tq, wx), BF16), pltpu.VMEM((t, wx), BF16), pltpu.VMEM((wx, t), BF16),
               pltpu.VMEM((H_FOX, 1, tq), F32), pltpu.VMEM((H_FOX, LANES, tq), F32)]
    return _prompt_attention_call(
        _fox_prompt_kernel, "fox_prompt", (qkv, qkv, rows_t, decay, decay, place_q, place_k),
        [_q_tile_spec(tq, W, 0), _per_batch_spec(t, W, 1), _values_spec(layer, t),
         _q_tile_spec(tq, LANES), _per_batch_spec(t, LANES),
         _const_spec(place_q.shape), _const_spec(place_k.shape)], W, tq, scratch)


def _sb_prompt(qkv, rows_t, layer, tri, tq):
    b, t, _ = qkv.shape
    W = BRANCH_W
    scratch = [pltpu.VMEM((H_SB, tq, LANES), BF16), pltpu.VMEM((H_SB, 1, tq), F32),
               pltpu.VMEM((H_SB, HEAD_DIM, tq), F32)]
    return _prompt_attention_call(
        _sb_prompt_kernel, "sb_prompt", (qkv, qkv, rows_t, tri),
        [_q_tile_spec(tq, W, 3), _per_batch_spec(t, W, 4), _values_spec(layer, t),
         _const_spec(tri.shape)], W, tq, scratch)


def _diff_prompt(qkv, rows_t, layer, pad_qk, diff_lambda, subln_col, lambda_init, tq):
    b, t, _ = qkv.shape
    W = BRANCH_W
    chains = 2 * H_DIFF
    scratch = [pltpu.VMEM((t, DIFF_PAD_W), BF16), pltpu.VMEM((DIFF_PAD_W, t), BF16),
               pltpu.VMEM((chains, tq, LANES), BF16), pltpu.VMEM((chains, 1, tq), F32),
               pltpu.VMEM((chains, LANES, tq), F32)]
    return _prompt_attention_call(
        functools.partial(_diff_prompt_kernel, lambda_init=lambda_init), "diff_prompt",
        (qkv, qkv, rows_t, pad_qk, diff_lambda, subln_col),
        [_q_tile_spec(tq, W, 0), _per_batch_spec(t, W, 1), _values_spec(layer, t),
         _const_spec(pad_qk.shape), _const_spec(diff_lambda.shape), _const_spec(subln_col.shape)],
        DIFF_PAD_W, tq, scratch)


def _decode_kernel(pt_ref, qab_ref, qc_ref, lfnew_ref, dl_ref, subln_ref, tri_ref, *rest,
                   pages_per_step, lambda_init):
    del pt_ref
    P = pages_per_step
    fox_refs = rest[0:P]
    lf_refs = rest[P:2 * P]
    sb_refs = rest[2 * P:3 * P]
    diff_refs = rest[3 * P:4 * P]
    o_ref = rest[4 * P]
    (qa_s, qb_s, qd_s, m_a, l_a, acc_a, carry_a, acc_b, carry_b, m_c, l_c, acc_c) = rest[4 * P + 1:]
    W = BRANCH_W
    R = DEC_ROWS
    page = LANES
    step = pl.program_id(1)
    n_steps = pl.num_programs(1)
    lane = lax.broadcasted_iota(jnp.int32, (R, W), 1)
    rowi = lax.broadcasted_iota(jnp.int32, (R, W), 0)
    head64 = _in_group(lane, rowi, HEAD_DIM)
    comp48 = _in_group(lane, rowi, DIFF_DQ)
    tri_ones = tri_ref[...]

    def bcast(x):
        return jnp.broadcast_to(x, (R, W))

    @pl.when(step == 0)
    def _init():
        qab = qab_ref[0]
        qc = qc_ref[0]
        zero = jnp.zeros((R, W), F32)
        qa = jnp.where(head64, bcast(qab[:, 0:W].astype(F32)), zero)
        qb = jnp.where(head64, bcast(qab[:, 3 * W:4 * W].astype(F32)), zero)
        qd = jnp.where(comp48, bcast(qc[:, 0:W].astype(F32)), zero)
        qa_s[...] = qa.astype(BF16)
        qb_s[...] = qb.astype(BF16)
        qd_s[...] = qd.astype(BF16)
        ka = bcast(qab[:, W:2 * W].astype(F32))
        m_a[...] = jnp.broadcast_to(jnp.sum(qa * ka, axis=1, keepdims=True), (R, LANES))
        l_a[...] = jnp.ones((R, LANES), F32)
        acc_a[...] = bcast(qab[:, 2 * W:3 * W].astype(F32))
        lane_f = lax.broadcasted_iota(jnp.int32, (R, F_PAD), 1)
        row_f = lax.broadcasted_iota(jnp.int32, (R, F_PAD), 0)
        lf_col = jnp.sum(jnp.where(lane_f == row_f, jnp.broadcast_to(lfnew_ref[0], (R, F_PAD)), 0.0),
                         axis=1, keepdims=True)
        carry_a[...] = jnp.broadcast_to(lf_col, (R, LANES))
        acc_b[...] = jnp.zeros((R, W), F32)
        carry_b[...] = jnp.zeros((R, LANES), F32)
        kd = bcast(qc[:, W:2 * W].astype(F32))
        m_c[...] = jnp.broadcast_to(jnp.sum(qd * kd, axis=1, keepdims=True), (R, LANES))
        l_c[...] = jnp.ones((R, LANES), F32)
        acc_c[...] = bcast(qc[:, 2 * W:3 * W].astype(F32))

    def keys(ref):
        return ref[0:W, :].astype(BF16)

    def values(ref):
        return ref[W:2 * W, :].astype(BF16)

    def rows(x, j):
        return x[j * R:(j + 1) * R]

    def softmax_weights(s_parts, m_ref, l_ref):
        m = m_ref[...]
        block_max = s_parts[0]
        for s in s_parts[1:]:
            block_max = jnp.maximum(block_max, s)
        m_new = jnp.maximum(m, jnp.max(block_max, axis=1, keepdims=True))
        alpha = jnp.exp(m - m_new)
        p_parts = [jnp.exp(s - m_new) for s in s_parts]
        p_sum = p_parts[0]
        for p in p_parts[1:]:
            p_sum = p_sum + p
        l_ref[...] = alpha * l_ref[...] + jnp.sum(p_sum, axis=1, keepdims=True)
        m_ref[...] = m_new
        return alpha, p_parts

    def weighted_values(w_parts, v_refs):
        pv = None
        for j in range(P):
            term = _dot_nt(w_parts[j].astype(BF16), values(v_refs[j]))
            pv = term if pv is None else pv + term
        return pv

    z_a = [_dot(qa_s[...], keys(fox_refs[j])) for j in range(P)]
    z_b = jnp.concatenate([_dot(qb_s[...], keys(sb_refs[j])) for j in range(P)], axis=0)
    z_c = [_dot(qd_s[...], keys(diff_refs[j])) for j in range(P)]

    log_hit, log_fail = _log_hit_and_fail(z_b)
    no_head = jnp.zeros((R - LF_ROWS, page), F32)
    log_forget = jnp.concatenate(
        [part for j in range(P) for part in (lf_refs[j][...], no_head)], axis=0)
    hi, lo = _split_hi_lo(jnp.concatenate([log_fail, log_forget], axis=0))
    sums = _dot(hi, tri_ones) + _dot(lo, tri_ones)
    suffix_b, total_b = sums[0:P * R, 0:page], sums[0:P * R, page:2 * page]
    suffix_a, total_a = sums[P * R:2 * P * R, 0:page], sums[P * R:2 * P * R, page:2 * page]

    later_a = carry_a[...]
    later_b = carry_b[...]
    s_a, w_b = [], []
    for j in range(P):
        s_a.append(z_a[j] + rows(suffix_a, j) + later_a)
        w_b.append(jnp.exp(rows(log_hit, j) + rows(suffix_b, j) + later_b))
        later_a = later_a + rows(total_a, j)
        later_b = later_b + rows(total_b, j)
    carry_a[...] = later_a
    carry_b[...] = later_b

    alpha_a, p_a = softmax_weights(s_a, m_a, l_a)
    alpha_c, p_c = softmax_weights(z_c, m_c, l_c)
    widen = lambda a: jnp.tile(a, (1, W // LANES))
    acc_a[...] = widen(alpha_a) * acc_a[...] + weighted_values(p_a, fox_refs)
    acc_b[...] += weighted_values(w_b, sb_refs)
    acc_c[...] = widen(alpha_c) * acc_c[...] + weighted_values(p_c, diff_refs)

    @pl.when(step == n_steps - 1)
    def _finish():
        zero = jnp.zeros((R, W), F32)
        o_a = jnp.sum(jnp.where(head64, acc_a[...] / l_a[:, 0:1], zero), axis=0, keepdims=True)
        o_b = jnp.sum(jnp.where(head64, acc_b[...], zero), axis=0, keepdims=True)
        n_c = acc_c[...] / l_c[:, 0:1]
        odd_row = (rowi & 1) == 1
        head_lo = (rowi - (rowi & 1)) * DIFF_DQ
        own_head = (lane >= head_lo) & (lane < head_lo + DIFF_DV)
        first = jnp.sum(jnp.where(own_head & ~odd_row, n_c, zero), axis=0, keepdims=True)
        second = jnp.sum(jnp.where(own_head & odd_row, n_c, zero), axis=0, keepdims=True)
        o_c = first - _diff_lambda(dl_ref[...], lambda_init) * second
        lane1 = lax.broadcasted_iota(jnp.int32, (1, W), 1)
        inv = jnp.zeros((1, W), F32)
        for head in range(H_DIFF):
            in_head = _in_group(lane1, head, DIFF_DV)
            ms = jnp.sum(jnp.where(in_head, o_c * o_c, 0.0), axis=1, keepdims=True) * (1.0 / DIFF_DV)
            inv = jnp.where(in_head, lax.rsqrt(ms + SUBLN_EPS), inv)
        o_c = o_c * inv * subln_ref[...] * (1.0 - lambda_init)
        o_ref[0, :, 0:W] = o_a.astype(o_ref.dtype)
        o_ref[0, :, W:2 * W] = o_b.astype(o_ref.dtype)
        o_ref[0, :, 2 * W:3 * W] = o_c.astype(o_ref.dtype)


def _decode_attention(page_table, layer, qkv_ab, qkv_c, logf_new, diff_lambda, subln_row, tri_ones,
                      cache_fox, cache_lf_t, cache_sb, cache_diff, lambda_init, pages_per_step):
    nb, n_pages = page_table.shape
    P = pages_per_step
    assert n_pages % P == 0
    n_steps = n_pages // P
    page = cache_fox.shape[3]
    W = BRANCH_W
    R = DEC_ROWS

    def page_spec(shape, j):
        def index(b, s, pt):
            return (layer, pt[b, n_pages - 1 - (s * P + j)], 0, 0)
        return pl.BlockSpec((None, None) + shape, index)

    per_seq = lambda b, s, pt: (b, 0, 0)
    fixed = lambda b, s, pt: (0, 0)
    in_specs = [
        pl.BlockSpec((1, 1, 6 * W), per_seq),
        pl.BlockSpec((1, 1, 3 * W), per_seq),
        pl.BlockSpec((1, 1, F_PAD), per_seq),
        pl.BlockSpec(diff_lambda.shape, fixed),
        pl.BlockSpec((1, W), fixed),
        pl.BlockSpec(tri_ones.shape, fixed),
    ]
    in_specs += [page_spec((2 * W, page), j) for j in range(P)]
    in_specs += [page_spec((LF_ROWS, page), j) for j in range(P)]
    in_specs += [page_spec((2 * W, page), j) for j in range(P)]
    in_specs += [page_spec((2 * W, page), j) for j in range(P)]
    scratch = [pltpu.VMEM((R, W), BF16)] * 3 + [
        pltpu.VMEM((R, LANES), F32), pltpu.VMEM((R, LANES), F32), pltpu.VMEM((R, W), F32),
        pltpu.VMEM((R, LANES), F32),
        pltpu.VMEM((R, W), F32), pltpu.VMEM((R, LANES), F32),
        pltpu.VMEM((R, LANES), F32), pltpu.VMEM((R, LANES), F32), pltpu.VMEM((R, W), F32),
    ]
    grid_spec = pltpu.PrefetchScalarGridSpec(
        num_scalar_prefetch=1, grid=(nb, n_steps), in_specs=in_specs,
        out_specs=pl.BlockSpec((1, 1, 3 * W), per_seq), scratch_shapes=scratch)
    out = pl.pallas_call(
        functools.partial(_decode_kernel, pages_per_step=P, lambda_init=lambda_init),
        grid_spec=grid_spec,
        out_shape=jax.ShapeDtypeStruct((nb, 1, 3 * W), BF16),
        compiler_params=_cparams(("parallel", "arbitrary")),
        name="decode_attention",
    )(page_table, qkv_ab.reshape(nb, 1, 6 * W), qkv_c.reshape(nb, 1, 3 * W),
      logf_new.reshape(nb, 1, F_PAD), diff_lambda, subln_row, tri_ones,
      *([cache_fox] * P), *([cache_lf_t] * P), *([cache_sb] * P), *([cache_diff] * P))
    return out.reshape(nb, 3 * W)


def _merge_kernel(x_ref, oa_ref, ob_ref, oc_ref, gpre_ref, wg_ref, wa_ref, wb_ref, wc_ref,
                  wout_ref, gpost_ref, y_ref):
    x = x_ref[...]
    d = x.shape[1]
    h = _rms_norm(x, gpre_ref[...], NORM_EPS).astype(BF16)
    merged = None
    for n, (o_ref, w_ref) in enumerate(((oa_ref, wa_ref), (ob_ref, wb_ref), (oc_ref, wc_ref))):
        gate = jax.nn.sigmoid(_dot(h, wg_ref[:, n * d:(n + 1) * d]))
        term = gate * _dot(o_ref[...], w_ref[...])
        merged = term if merged is None else merged + term
    mix = _dot(merged.astype(BF16), wout_ref[...])
    y_ref[...] = x + _rms_norm(mix, gpost_ref[...], NORM_EPS)


def _merge(x, o_a, o_b, o_c, gpre, wg, wa, wb, wc, wout, gpost, tm):
    n, d = x.shape
    row = lambda i: (i, 0)
    fixed = lambda i: (0, 0)
    full = lambda a: pl.BlockSpec(a.shape, fixed)
    return pl.pallas_call(
        _merge_kernel,
        grid=(n // tm,),
        in_specs=[pl.BlockSpec((tm, d), row), pl.BlockSpec((tm, o_a.shape[1]), row),
                  pl.BlockSpec((tm, o_b.shape[1]), row), pl.BlockSpec((tm, o_c.shape[1]), row),
                  full(gpre), full(wg), full(wa), full(wb), full(wc), full(wout), full(gpost)],
        out_specs=pl.BlockSpec((tm, d), row),
        out_shape=jax.ShapeDtypeStruct((n, d), F32),
        compiler_params=_cparams(("parallel",)),
        name="merge",
    )(x, o_a, o_b, o_c, gpre, wg, wa, wb, wc, wout, gpost)


def _ffn_kernel(x_ref, gpre_ref, wgu_ref, wdown_ref, gpost_ref, y_ref, act_ref, *, chunk):
    x = x_ref[...]
    d_ff = wdown_ref.shape[0]
    h = _rms_norm(x, gpre_ref[...], NORM_EPS).astype(BF16)
    for c0 in range(0, d_ff, chunk):
        gate = _dot(h, wgu_ref[:, c0:c0 + chunk])
        up = _dot(h, wgu_ref[:, d_ff + c0:d_ff + c0 + chunk])
        act_ref[:, c0:c0 + chunk] = (gate * jax.nn.sigmoid(gate) * up).astype(BF16)
    ffn = _dot(act_ref[...], wdown_ref[...])
    y_ref[...] = x + _rms_norm(ffn, gpost_ref[...], NORM_EPS)


def _ffn(x, gpre, wgu, wdown, gpost, tm, chunk):
    n, d = x.shape
    d_ff = wdown.shape[0]
    assert d_ff % chunk == 0
    row = lambda i: (i, 0)
    fixed = lambda i: (0, 0)
    full = lambda a: pl.BlockSpec(a.shape, fixed)
    return pl.pallas_call(
        functools.partial(_ffn_kernel, chunk=chunk),
        grid=(n // tm,),
        in_specs=[pl.BlockSpec((tm, d), row), full(gpre), full(wgu), full(wdown), full(gpost)],
        out_specs=pl.BlockSpec((tm, d), row),
        out_shape=jax.ShapeDtypeStruct((n, d), F32),
        scratch_shapes=[pltpu.VMEM((tm, d_ff), BF16)],
        compiler_params=_cparams(("parallel",)),
        name="ffn",
    )(x, gpre, wgu, wdown, gpost)


def _lambda_init(layer):
    return 0.8 - 0.6 * math.exp(-0.3 * layer)


def _rotary_tables(pos):
    half = ROPE_DIM // 2
    inv_freq = ROPE_THETA ** (-jnp.arange(0, ROPE_DIM, 2, dtype=F32) / ROPE_DIM)
    ang = pos.astype(F32)[:, None] * inv_freq[None, :]
    cos, sin = jnp.cos(ang), jnp.sin(ang)
    n = pos.shape[0]
    rest = DIFF_DQ - ROPE_DIM
    ones, zeros = jnp.ones((n, rest), F32), jnp.zeros((n, rest), F32)
    zh = jnp.zeros((n, half), F32)
    reps = BRANCH_W // DIFF_DQ
    cos_t = jnp.tile(jnp.concatenate([cos, cos, ones], axis=1), (1, reps))
    sina_t = jnp.tile(jnp.concatenate([-sin, zh, zeros], axis=1), (1, reps))
    sinb_t = jnp.tile(jnp.concatenate([zh, sin, zeros], axis=1), (1, reps))
    return cos_t, sina_t, sinb_t


def _layer_params(l, norm_mix_pre, norm_mix_post, norm_ffn_pre, norm_ffn_post, w_in, b_forget,
                  diff_lambda, diff_subln, w_branch, w_out, w_gate_up, w_down):
    W = BRANCH_W
    d = w_in.shape[1]
    w = w_in[l]
    q_scale = HEAD_DIM ** -0.5
    cols = [w[:, 0:W] * q_scale, w[:, W:3 * W], w[:, 3 * W:4 * W] * q_scale, w[:, 4 * W:9 * W],
            jnp.pad(w[:, 9 * W:9 * W + H_FOX], ((0, 0), (0, F_PAD - H_FOX)))]
    w_c = w_branch[l, 2]
    w_c_pad = jnp.pad(w_c.reshape(H_DIFF, DIFF_DV, d), ((0, 0), (0, LANES - DIFF_DV), (0, 0)))
    row = lambda v: v.reshape(1, -1).astype(F32)
    return dict(
        w_qkv=jnp.concatenate(cols, axis=1).astype(BF16),
        b_f=jnp.pad(b_forget[l], (0, F_PAD - H_FOX)).reshape(1, F_PAD).astype(F32),
        w_gate=w[:, 9 * W + H_FOX:].astype(BF16),
        w_a=w_branch[l, 0].astype(BF16), w_b=w_branch[l, 1].astype(BF16),
        w_c=w_c.astype(BF16), w_c_pad=w_c_pad.reshape(DIFF_PAD_W, d).astype(BF16),
        w_out=w_out[l].astype(BF16), w_gu=w_gate_up[l].astype(BF16), w_down=w_down[l].astype(BF16),
        g_mix_pre=row(norm_mix_pre[l]), g_mix_post=row(norm_mix_post[l]),
        g_ffn_pre=row(norm_ffn_pre[l]), g_ffn_post=row(norm_ffn_post[l]),
        diff_lambda=diff_lambda[l].astype(F32),
        subln=row(jnp.tile(diff_subln[l], H_DIFF)),
        subln_col=jnp.pad(diff_subln[l], (0, LANES - DIFF_DV)).reshape(LANES, 1).astype(F32),
        lambda_init=_lambda_init(l),
    )


def _strict_lower_ones(n):
    j = lax.broadcasted_iota(jnp.int32, (n, n), 0)
    s = lax.broadcasted_iota(jnp.int32, (n, n), 1)
    return (j > s).astype(BF16)


N_PIECES = 3
ONES_LANE = H_FOX * N_PIECES


def _fox_decay_pieces(c):
    b, t, _ = c.shape
    hi = _bf16_prefix(c)
    mid = _bf16_prefix(c - hi)
    lo = c - hi - mid
    pieces = jnp.stack([hi, mid, lo], axis=-1).reshape(b, t, ONES_LANE)
    out = jnp.concatenate([pieces, jnp.ones((b, t, 1), F32),
                           jnp.zeros((b, t, LANES - ONES_LANE - 1), F32)], axis=-1)
    return out.astype(BF16)


def _fox_placements():
    n_in = H_FOX * HEAD_DIM + LANES
    place_q = np.zeros((n_in, H_FOX * LANES), np.float32)
    place_k = np.zeros((n_in, H_FOX * LANES), np.float32)
    pieces0 = H_FOX * HEAD_DIM
    for h in range(H_FOX):
        for d in range(HEAD_DIM):
            place_q[h * HEAD_DIM + d, h * LANES + d] = 1.0
            place_k[h * HEAD_DIM + d, h * LANES + d] = 1.0
        for j in range(N_PIECES):
            place_q[pieces0 + N_PIECES * h + j, h * LANES + HEAD_DIM + j] = 1.0
            place_q[pieces0 + ONES_LANE, h * LANES + HEAD_DIM + N_PIECES + j] = -1.0
            place_k[pieces0 + ONES_LANE, h * LANES + HEAD_DIM + j] = 1.0
            place_k[pieces0 + N_PIECES * h + j, h * LANES + HEAD_DIM + N_PIECES + j] = 1.0
    return jnp.asarray(place_q, BF16), jnp.asarray(place_k, BF16)


def _diff_placement():
    pad_qk = np.zeros((BRANCH_W, DIFF_PAD_W), np.float32)
    for h in range(H_DIFF):
        for c in range(2):
            for d in range(DIFF_DQ):
                pad_qk[h * DIFF_DV + c * DIFF_DQ + d, h * LANES + c * HEAD_DIM + d] = 1.0
    return jnp.asarray(pad_qk, BF16)


def _page_major_cache(cache):
    depth, n_phys, page = cache.shape[:3]
    return jnp.transpose(cache, (0, 1, 3, 4, 5, 2)).reshape(depth, n_phys, -1, page)


def _ffn_chunk(d_ff):
    for chunk in (512, 256, 128):
        if d_ff % chunk == 0:
            return chunk
    return d_ff


def _pages_per_step(n_pages):
    for p in (8, 4, 2):
        if n_pages % p == 0:
            return p
    return 1


def kernel(x_prompt, x_sample, cache_fox_kv, cache_fox_logf, cache_sb_kv, cache_diff_kv, page_table,
           norm_mix_pre, norm_mix_post, norm_ffn_pre, norm_ffn_post, w_in, b_forget, diff_lambda,
           diff_subln, w_branch, w_out, w_gate_up, w_down):
    depth = w_in.shape[0]
    b, t, d = x_prompt.shape
    nb, dec_t, _ = x_sample.shape
    assert dec_t == 1
    page = cache_fox_kv.shape[2]
    n_pages = page_table.shape[1]
    past_len = n_pages * page
    W = BRANCH_W
    d_ff = w_down.shape[1]

    tq = min(256, t)
    tm = min(512, t)
    assert t % tq == 0 and t % tm == 0 and page == LANES
    pages_per_step = _pages_per_step(n_pages)
    chunk = _ffn_chunk(d_ff)

    params = [_layer_params(l, norm_mix_pre, norm_mix_post, norm_ffn_pre, norm_ffn_post, w_in,
                            b_forget, diff_lambda, diff_subln, w_branch, w_out, w_gate_up, w_down)
              for l in range(depth)]
    tab_p = _rotary_tables(jnp.arange(t))
    tab_s = tuple(jnp.broadcast_to(a, (nb, W)) for a in _rotary_tables(jnp.full((1,), past_len)))
    tri_q = _strict_lower_ones(tq).T
    fox_consts = _fox_placements()
    diff_pad = _diff_placement()
    tri_ones_page = jnp.concatenate([_strict_lower_ones(page), jnp.ones((page, page), BF16)], axis=1)

    cache_fox = _page_major_cache(cache_fox_kv)
    cache_sb = _page_major_cache(cache_sb_kv)
    cache_diff = _page_major_cache(cache_diff_kv)
    cache_lf_t = jnp.pad(jnp.swapaxes(cache_fox_logf, 2, 3),
                         ((0, 0), (0, 0), (0, LF_ROWS - H_FOX), (0, 0)))

    def tail(x, o_a, o_b, o_c, w_c, p, rows):
        x = _merge(x, o_a, o_b, o_c, p["g_mix_pre"], p["w_gate"], p["w_a"], p["w_b"], w_c,
                   p["w_out"], p["g_mix_post"], rows)
        return _ffn(x, p["g_ffn_pre"], p["w_gu"], p["w_down"], p["g_ffn_post"], rows, chunk)

    x = x_prompt.reshape(b * t, d)
    rows_p, logf_p = None, []
    for l, p in enumerate(params):
        qkv_ab, rows_p, qkv_c, logf = _inproj(
            x, p["g_mix_pre"], p["w_qkv"], p["b_f"], *tab_p, tm, l, depth, rows_p, seq_len=t)
        logf6 = logf[:, :H_FOX].reshape(b, t, H_FOX)
        logf_p.append(logf6)
        c_t = _cumsum_time(jnp.pad(jnp.swapaxes(logf6, 1, 2), ((0, 0), (0, 8 - H_FOX), (0, 0))))
        c = jnp.swapaxes(c_t, 1, 2)[..., :H_FOX]
        qkv_ab3 = qkv_ab.reshape(b, t, 6 * W)
        qkv_c3 = qkv_c.reshape(b, t, 3 * W)
        fox_t, sb_t, diff_t = rows_p
        o_a = _fox_prompt(qkv_ab3, fox_t, l, _fox_decay_pieces(c), fox_consts, tq)
        o_b = _sb_prompt(qkv_ab3, sb_t, l, tri_q, tq)
        o_c = _diff_prompt(qkv_c3, diff_t, l, diff_pad, p["diff_lambda"], p["subln_col"],
                           p["lambda_init"], tq)
        x = tail(x, o_a.reshape(b * t, W), o_b.reshape(b * t, W), o_c.reshape(b * t, DIFF_PAD_W),
                 p["w_c_pad"], p, tm)
    y_prompt = x.reshape(b, t, d)

    x = x_sample.reshape(nb, d)
    rows_s, logf_s = None, []
    for l, p in enumerate(params):
        qkv_ab, rows_s, qkv_c, logf = _inproj(
            x, p["g_mix_pre"], p["w_qkv"], p["b_f"], *tab_s, nb, l, depth, rows_s)
        logf_s.append(logf[:, :H_FOX].reshape(nb, 1, H_FOX))
        o = _decode_attention(page_table, l, qkv_ab, qkv_c, logf, p["diff_lambda"], p["subln"],
                              tri_ones_page, cache_fox, cache_lf_t, cache_sb, cache_diff,
                              p["lambda_init"], pages_per_step)
        x = tail(x, o[:, 0:W], o[:, W:2 * W], o[:, 2 * W:3 * W], p["w_c"], p, nb)
    y_sample = x.reshape(nb, 1, d)

    def token_major(rows_t, heads, dim):
        return jnp.transpose(rows_t.reshape(depth, b, 2, heads, dim, t), (0, 1, 5, 2, 3, 4))

    fox_p = token_major(rows_p[0], H_FOX, HEAD_DIM)
    sb_p = token_major(rows_p[1], H_SB, HEAD_DIM)
    diff_p = token_major(rows_p[2], H_DIFF, DIFF_DV)
    fox_s = rows_s[0].reshape(depth, nb, 1, 2, H_FOX, HEAD_DIM)
    sb_s = rows_s[1].reshape(depth, nb, 1, 2, H_SB, HEAD_DIM)
    diff_s = rows_s[2].reshape(depth, nb, 1, 2, H_DIFF, DIFF_DV)
    return (y_prompt, y_sample, fox_p, jnp.stack(logf_p), sb_p, diff_p,
            fox_s, jnp.stack(logf_s), sb_s, diff_s)
```

```python
import functools
import math

import jax
import jax.numpy as jnp
import numpy as np
from jax import lax
from jax.experimental import pallas as pl
from jax.experimental.pallas import tpu as pltpu

F32 = jnp.float32
BF16 = jnp.bfloat16

HEAD_DIM = 64
H_FOX = 6
H_SB = 6
H_DIFF = 4
DIFF_DQ = 48
DIFF_DV = 2 * DIFF_DQ
BRANCH_W = 384
N_BRANCH = 3
ROPE_DIM = DIFF_DQ // 4
ROPE_THETA = 500000.0
NORM_EPS = 1e-6
SUBLN_EPS = 1e-5

LANES = 128
HEADS_PER_LANE_BLOCK = LANES // HEAD_DIM
F_PAD = LANES
DIFF_PAD_W = H_DIFF * LANES
NEG_BIG = -1e30
EXP_IS_ZERO = -104.0
VMEM_LIMIT = 56 * 1024 * 1024
DEC_ROWS = 16
LF_ROWS = 8
STAGE_LAG = 4
PREP_ROWS = 512


def _cparams(sem):
    return pltpu.CompilerParams(dimension_semantics=sem, vmem_limit_bytes=VMEM_LIMIT)


def _rms_norm(x, gain, eps):
    ms = jnp.mean(x * x, axis=-1, keepdims=True)
    return x * lax.rsqrt(ms + eps) * gain


def _softplus_neg_abs(z):
    return jnp.log1p(jnp.exp(-jnp.abs(z)))


def _log_hit_and_fail(z):
    log_fail = jnp.minimum(-z, 0.0) - jnp.log(1.0 + jnp.exp(-jnp.abs(z)))
    return log_fail + z, log_fail


def _bf16_prefix(x):
    bits = lax.bitcast_convert_type(x, jnp.uint32) & jnp.uint32(0xFFFF0000)
    return lax.bitcast_convert_type(bits, F32)


def _split_hi_lo(x):
    hi = _bf16_prefix(x)
    return hi.astype(BF16), (x - hi).astype(BF16)


def _in_group(index, group, width):
    lo = group * width
    return (index >= lo) & (index < lo + width)


def _dot(a, b):
    return jnp.dot(a, b, preferred_element_type=F32)


def _dot_nt(a, b):
    return lax.dot_general(a, b, (((1,), (1,)), ((), ())), preferred_element_type=F32)


def _inproj_kernel(x_ref, gain_ref, w_ref, bf_ref, cos_ref, sina_ref, sinb_ref, *rest, rows_transposed):
    qkv_ab_ref, foxkv_ref, sbkv_ref, diffkv_ref, qkv_c_ref, logf_ref = rest[-6:]
    W = BRANCH_W
    h = _rms_norm(x_ref[...], gain_ref[...], NORM_EPS).astype(BF16)

    pairs = {}

    def proj(c, width=W):
        pair = c // 2
        if pair not in pairs:
            lo = pair * 2 * W
            pairs[pair] = _dot(h, w_ref[:, lo:min(lo + 2 * W, w_ref.shape[1])])
        lo = (c % 2) * W
        return pairs[pair][:, lo:lo + width]

    def rotary(r):
        return (r * cos_ref[...] + pltpu.roll(r, W - ROPE_DIM // 2, 1) * sina_ref[...]
                + pltpu.roll(r, ROPE_DIM // 2, 1) * sinb_ref[...])

    def store_rows(kv_ref, j, r):
        if rows_transposed:
            kv_ref[j * W:(j + 1) * W, :] = r.T
        else:
            kv_ref[:, j * W:(j + 1) * W] = r

    for c, kv_ref in ((0, foxkv_ref), (3, sbkv_ref)):
        qkv_ab_ref[:, c * W:(c + 1) * W] = proj(c).astype(BF16)
        for j in range(2):
            r = proj(c + 1 + j)
            qkv_ab_ref[:, (c + 1 + j) * W:(c + 2 + j) * W] = r.astype(BF16)
            store_rows(kv_ref, j, r)

    qc = rotary(proj(6)) * (DIFF_DQ ** -0.5)
    qkv_c_ref[:, 0:W] = qc.astype(BF16)
    kc = rotary(proj(7))
    qkv_c_ref[:, W:2 * W] = kc.astype(BF16)
    store_rows(diffkv_ref, 0, kc)
    vc = proj(8)
    qkv_c_ref[:, 2 * W:3 * W] = vc.astype(BF16)
    store_rows(diffkv_ref, 1, vc)

    f = proj(9, F_PAD) + bf_ref[...]
    logf_ref[...] = jnp.minimum(f, 0.0) - _softplus_neg_abs(f)


def _inproj(x, gain, w, bf, cos_t, sina_t, sinb_t, tm, layer, depth, rows_so_far, seq_len=None):
    n, d = x.shape
    W = BRANCH_W
    n_tab = cos_t.shape[0] // tm
    row = lambda i: (i, 0)
    fixed = lambda i: (0, 0)
    tab = lambda i: (i % n_tab, 0)
    if seq_len is None:
        slab = pl.BlockSpec((None, tm, 2 * W), lambda i: (layer, i, 0))
        rows_shape = jax.ShapeDtypeStruct((depth, n, 2 * W), F32)
    else:
        assert seq_len == cos_t.shape[0]
        slab = pl.BlockSpec((None, None, 2 * W, tm), lambda i: (layer, i // n_tab, 0, i % n_tab))
        rows_shape = jax.ShapeDtypeStruct((depth, n // seq_len, 2 * W, seq_len), F32)
    out_shape = (
        jax.ShapeDtypeStruct((n, 6 * W), BF16), rows_shape, rows_shape, rows_shape,
        jax.ShapeDtypeStruct((n, 3 * W), BF16),
        jax.ShapeDtypeStruct((n, F_PAD), F32),
    )
    in_specs = [
        pl.BlockSpec((tm, d), row),
        pl.BlockSpec((1, d), fixed),
        pl.BlockSpec(w.shape, fixed),
        pl.BlockSpec((1, F_PAD), fixed),
        pl.BlockSpec((tm, W), tab),
        pl.BlockSpec((tm, W), tab),
        pl.BlockSpec((tm, W), tab),
    ]
    args = [x, gain, w, bf, cos_t, sina_t, sinb_t]
    aliases = {}
    if rows_so_far is None:
        rows_so_far = tuple(jnp.zeros(rows_shape.shape, F32) for _ in range(3))
    for j, buf in enumerate(rows_so_far):
        aliases[len(args)] = 1 + j
        args.append(buf)
        in_specs.append(pl.BlockSpec(memory_space=pl.ANY))
    qkv_ab, fox_rows, sb_rows, diff_rows, qkv_c, logf = pl.pallas_call(
        functools.partial(_inproj_kernel, rows_transposed=seq_len is not None),
        grid=(n // tm,),
        in_specs=in_specs,
        out_specs=(pl.BlockSpec((tm, 6 * W), row), slab, slab, slab,
                   pl.BlockSpec((tm, 3 * W), row), pl.BlockSpec((tm, F_PAD), row)),
        out_shape=out_shape,
        input_output_aliases=aliases,
        compiler_params=_cparams(("parallel",)),
        name="inproj",
    )(*args)
    return qkv_ab, (fox_rows, sb_rows, diff_rows), qkv_c, logf


def _cumsum_kernel(x_ref, o_ref):
    x = x_ref[0]
    t = x.shape[-1]
    lane = lax.broadcasted_iota(jnp.int32, x.shape, 1)
    shift = 1
    while shift < t:
        x = x + jnp.where(lane >= shift, pltpu.roll(x, shift, 1), 0.0)
        shift *= 2
    o_ref[0] = x


def _cumsum_time(logf_t):
    b, r, t = logf_t.shape
    blk = pl.BlockSpec((1, r, t), lambda i: (i, 0, 0))
    return pl.pallas_call(
        _cumsum_kernel, grid=(b,), in_specs=[blk], out_specs=blk,
        out_shape=jax.ShapeDtypeStruct(logf_t.shape, F32),
        compiler_params=_cparams(("parallel",)), name="cumsum_logf",
    )(logf_t)


def _lane_block(ref, rows, blk):
    return ref[0, rows, blk * LANES:(blk + 1) * LANES]


def _softmax_update(s, vt_ext, m_ref, acc_ref):
    m = m_ref[...]
    m_new = jnp.maximum(m, jnp.max(s, axis=0, keepdims=True))
    p = jnp.exp(s - m_new)
    acc_ref[...] = jnp.exp(m - m_new) * acc_ref[...] + _dot(vt_ext, p.astype(BF16))
    m_ref[...] = m_new


def _staggered(n_chains, stages, lag):
    state = [None] * n_chains
    for step in range(n_chains + (len(stages) - 1) * lag):
        for k, stage in enumerate(stages):
            c = step - k * lag
            if 0 <= c < n_chains:
                state[c] = stage(c, state[c])


def _key_query_iota(tq):
    key = lax.broadcasted_iota(jnp.int32, (tq, tq), 0)
    query = lax.broadcasted_iota(jnp.int32, (tq, tq), 1)
    return key, query


def _for_key_blocks(i, tq, block, newest_first, still_live=None):
    def visit(step):
        kb = (i - 1 - step) if newest_first else step
        block(pl.multiple_of(kb * tq, tq), False)

    def body(step, carry):
        visit(step)
        return carry

    q0 = pl.multiple_of(i * tq, tq)
    if not newest_first:
        lax.fori_loop(0, i, body, 0)
        block(q0, True)
    elif still_live is None:
        block(q0, True)
        lax.fori_loop(0, i, body, 0)
    else:
        block(q0, True)

        def live_body(carry):
            visit(carry[0])
            return carry[0] + 1, still_live()

        lax.while_loop(lambda carry: (carry[0] < i) & carry[1], live_body, (0, still_live()))


def _fox_prompt_kernel(q_ref, k_ref, vt_ref, cq_ref, ck_ref, place_q_ref, place_k_ref,
                       o_ref, qx_s, kx_s, vt_s, m_s, acc_s, *, tq):
    i = pl.program_id(1)
    t = k_ref.shape[1]
    key, query = _key_query_iota(tq)
    causal = key <= query

    @pl.when(i == 0)
    def _per_batch_row():
        for r0 in range(0, t, PREP_ROWS):
            rows = slice(r0, min(r0 + PREP_ROWS, t))
            kx = _dot(jnp.concatenate([k_ref[0, rows, :], ck_ref[0, rows, :]], axis=1), place_k_ref[...])
            kx_s[rows, :] = kx.astype(BF16)
        for head in range(H_FOX):
            vt_s[head * LANES:head * LANES + HEAD_DIM, :] = (
                vt_ref[head * HEAD_DIM:(head + 1) * HEAD_DIM, :].astype(BF16))
            vt_s[head * LANES + HEAD_DIM:(head + 1) * LANES, :] = jnp.ones((LANES - HEAD_DIM, t), BF16)

    qx = _dot(jnp.concatenate([q_ref[0], cq_ref[0]], axis=1), place_q_ref[...])
    qx_s[...] = qx.astype(BF16)
    for head in range(H_FOX):
        m_s[head] = jnp.full((1, tq), NEG_BIG, F32)
        acc_s[head] = jnp.zeros((LANES, tq), F32)

    def block(k0, diagonal):
        def scores(head, _):
            k = kx_s[pl.ds(k0, tq), head * LANES:(head + 1) * LANES]
            s = _dot_nt(k, qx_s[:, head * LANES:(head + 1) * LANES])
            return jnp.where(causal, s, NEG_BIG) if diagonal else s

        def update(head, s):
            vt = vt_s[head * LANES:(head + 1) * LANES, pl.ds(k0, tq)]
            _softmax_update(s, vt, m_s.at[head], acc_s.at[head])

        _staggered(H_FOX, (scores, update), STAGE_LAG)

    _for_key_blocks(i, tq, block, newest_first=False)
    for blk in range(H_FOX // HEADS_PER_LANE_BLOCK):
        halves = []
        for sub in range(HEADS_PER_LANE_BLOCK):
            a = acc_s[blk * HEADS_PER_LANE_BLOCK + sub]
            halves.append(a[0:HEAD_DIM] / a[HEAD_DIM:HEAD_DIM + 1])
        o_t = jnp.concatenate(halves, axis=0)
        o_ref[0, :, blk * LANES:(blk + 1) * LANES] = o_t.T.astype(o_ref.dtype)


def _sb_prompt_kernel(q_ref, k_ref, vt_ref, tri_ref, o_ref, qh_s, later_s, acc_s, *, tq):
    i = pl.program_id(1)
    lane = lax.broadcasted_iota(jnp.int32, (tq, LANES), 1)
    key, query = _key_query_iota(tq)
    strict = key < query
    for head in range(H_SB):
        blk, sub = divmod(head, HEADS_PER_LANE_BLOCK)
        q_pair = _lane_block(q_ref, slice(None), blk)
        qh_s[head] = jnp.where(_in_group(lane, sub, HEAD_DIM), q_pair, jnp.zeros_like(q_pair))
        later_s[head] = jnp.zeros((1, tq), F32)
        acc_s[head] = jnp.zeros((HEAD_DIM, tq), F32)

    def block(k0, diagonal):
        def logits(head, _):
            k = _lane_block(k_ref, pl.ds(k0, tq), head // HEADS_PER_LANE_BLOCK)
            log_hit, log_fail = _log_hit_and_fail(_dot_nt(k, qh_s[head]))
            if diagonal:
                log_fail = jnp.where(strict, log_fail, 0.0)
            return log_hit, _split_hi_lo(log_fail), jnp.sum(log_fail, axis=0, keepdims=True)

        def suffix_sums(head, state):
            log_hit, (hi, lo), fail_sum = state
            tri = tri_ref[...]
            return log_hit + (_dot(tri, hi) + _dot(tri, lo)), fail_sum

        def accumulate(head, state):
            log_w, fail_sum = state
            vt = vt_ref[head * HEAD_DIM:(head + 1) * HEAD_DIM, pl.ds(k0, tq)].astype(BF16)
            w = jnp.exp(log_w + later_s[head])
            if diagonal:
                w = jnp.where(strict, w, 0.0)
            acc_s[head] += _dot(vt, w.astype(BF16))
            later_s[head] += fail_sum

        _staggered(H_SB, (logits, suffix_sums, accumulate), STAGE_LAG)

    def still_live():
        return jnp.max(later_s[...]) > EXP_IS_ZERO

    _for_key_blocks(i, tq, block, newest_first=True, still_live=still_live)
    for blk in range(H_SB // HEADS_PER_LANE_BLOCK):
        o_t = jnp.concatenate([acc_s[blk * HEADS_PER_LANE_BLOCK + sub]
                               for sub in range(HEADS_PER_LANE_BLOCK)], axis=0)
        o_ref[0, :, blk * LANES:(blk + 1) * LANES] = o_t.T.astype(o_ref.dtype)


def _diff_lambda(dl, lambda_init):
    l1 = jnp.sum(dl[0:1] * dl[1:2], axis=1, keepdims=True)
    l2 = jnp.sum(dl[2:3] * dl[3:4], axis=1, keepdims=True)
    return jnp.exp(l1) - jnp.exp(l2) + lambda_init


def _diff_prompt_kernel(q_ref, k_ref, vt_ref, pad_qk_ref, dl_ref, subln_ref, o_ref,
                        kp_s, vt_s, qh_s, m_s, acc_s, *, tq, lambda_init):
    i = pl.program_id(1)
    t = k_ref.shape[1]
    lane = lax.broadcasted_iota(jnp.int32, (tq, LANES), 1)
    key, query = _key_query_iota(tq)
    causal = key <= query

    @pl.when(i == 0)
    def _per_batch_row():
        for r0 in range(0, t, PREP_ROWS):
            rows = slice(r0, min(r0 + PREP_ROWS, t))
            kp_s[rows, :] = _dot(k_ref[0, rows, :], pad_qk_ref[...]).astype(BF16)
        for head in range(H_DIFF):
            vt_s[head * LANES:head * LANES + DIFF_DV, :] = (
                vt_ref[head * DIFF_DV:(head + 1) * DIFF_DV, :].astype(BF16))
            vt_s[head * LANES + DIFF_DV:(head + 1) * LANES, :] = jnp.ones((LANES - DIFF_DV, t), BF16)

    q_pad = _dot(q_ref[0], pad_qk_ref[...]).astype(BF16)
    for head in range(H_DIFF):
        q_head = q_pad[:, head * LANES:(head + 1) * LANES]
        for comp in range(2):
            c = 2 * head + comp
            qh_s[c] = jnp.where(_in_group(lane, comp, HEAD_DIM), q_head, jnp.zeros_like(q_head))
            m_s[c] = jnp.full((1, tq), NEG_BIG, F32)
            acc_s[c] = jnp.zeros((LANES, tq), F32)

    def block(k0, diagonal):
        def scores(c, _):
            head = c // 2
            s = _dot_nt(kp_s[pl.ds(k0, tq), head * LANES:(head + 1) * LANES], qh_s[c])
            return jnp.where(causal, s, NEG_BIG) if diagonal else s

        def update(c, s):
            head = c // 2
            vt = vt_s[head * LANES:(head + 1) * LANES, pl.ds(k0, tq)]
            _softmax_update(s, vt, m_s.at[c], acc_s.at[c])

        _staggered(2 * H_DIFF, (scores, update), STAGE_LAG)

    _for_key_blocks(i, tq, block, newest_first=False)
    lam = _diff_lambda(dl_ref[...], lambda_init)
    gain = subln_ref[0:DIFF_DV, :] * (1.0 - lambda_init)
    for head in range(H_DIFF):
        a0 = acc_s[2 * head]
        a1 = acc_s[2 * head + 1]
        o = (a0[0:DIFF_DV] / a0[DIFF_DV:DIFF_DV + 1]
             - lam * (a1[0:DIFF_DV] / a1[DIFF_DV:DIFF_DV + 1]))
        ms = jnp.sum(o * o, axis=0, keepdims=True) * (1.0 / DIFF_DV)
        o = o * lax.rsqrt(ms + SUBLN_EPS) * gain
        o_t = jnp.concatenate([o, jnp.zeros((LANES - DIFF_DV, tq), F32)], axis=0)
        o_ref[0, :, head * LANES:(head + 1) * LANES] = o_t.T.astype(o_ref.dtype)


def _prompt_attention_call(kernel, name, args, in_specs, out_w, tq, scratch):
    b, t = args[0].shape[0], args[0].shape[1]
    return pl.pallas_call(
        functools.partial(kernel, tq=tq),
        grid=(b, t // tq),
        in_specs=in_specs,
        out_specs=pl.BlockSpec((1, tq, out_w), lambda bi, i: (bi, i, 0)),
        out_shape=jax.ShapeDtypeStruct((b, t, out_w), BF16),
        scratch_shapes=scratch,
        compiler_params=_cparams(("parallel", "arbitrary")),
        name=name,
    )(*args)


def _q_tile_spec(tq, width, col_block=0):
    return pl.BlockSpec((1, tq, width), lambda bi, i: (bi, i, col_block))


def _per_batch_spec(rows, width, col_block=0):
    return pl.BlockSpec((1, rows, width), lambda bi, i: (bi, 0, col_block))


def _const_spec(shape):
    return pl.BlockSpec(shape, lambda bi, i: (0,) * len(shape))


def _values_spec(layer, t):
    return pl.BlockSpec((None, None, BRANCH_W, t), lambda bi, i: (layer, bi, 1, 0))


def _fox_prompt(qkv, rows_t, layer, decay, consts, tq):
    b, t, _ = qkv.shape
    W = BRANCH_W
    wx = H_FOX * LANES
    place_q, place_k = consts
    scratch = [pltpu.VMEM((tq, wx), BF16), pltpu.VMEM((t, wx), BF16), pltpu.VMEM((wx, t), BF16),
               pltpu.VMEM((H_FOX, 1, tq), F32), pltpu.VMEM((H_FOX, LANES, tq), F32)]
    return _prompt_attention_call(
        _fox_prompt_kernel, "fox_prompt", (qkv, qkv, rows_t, decay, decay, place_q, place_k),
        [_q_tile_spec(tq, W, 0), _per_batch_spec(t, W, 1), _values_spec(layer, t),
         _q_tile_spec(tq, LANES), _per_batch_spec(t, LANES),
         _const_spec(place_q.shape), _const_spec(place_k.shape)], W, tq, scratch)


def _sb_prompt(qkv, rows_t, layer, tri, tq):
    b, t, _ = qkv.shape
    W = BRANCH_W
    scratch = [pltpu.VMEM((H_SB, tq, LANES), BF16), pltpu.VMEM((H_SB, 1, tq), F32),
               pltpu.VMEM((H_SB, HEAD_DIM, tq), F32)]
    return _prompt_attention_call(
        _sb_prompt_kernel, "sb_prompt", (qkv, qkv, rows_t, tri),
        [_q_tile_spec(tq, W, 3), _per_batch_spec(t, W, 4), _values_spec(layer, t),
         _const_spec(tri.shape)], W, tq, scratch)


def _diff_prompt(qkv, rows_t, layer, pad_qk, diff_lambda, subln_col, lambda_init, tq):
    b, t, _ = qkv.shape
    W = BRANCH_W
    chains = 2 * H_DIFF
    scratch = [pltpu.VMEM((t, DIFF_PAD_W), BF16), pltpu.VMEM((DIFF_PAD_W, t), BF16),
               pltpu.VMEM((chains, tq, LANES), BF16), pltpu.VMEM((chains, 1, tq), F32),
               pltpu.VMEM((chains, LANES, tq), F32)]
    return _prompt_attention_call(
        functools.partial(_diff_prompt_kernel, lambda_init=lambda_init), "diff_prompt",
        (qkv, qkv, rows_t, pad_qk, diff_lambda, subln_col),
        [_q_tile_spec(tq, W, 0), _per_batch_spec(t, W, 1), _values_spec(layer, t),
         _const_spec(pad_qk.shape), _const_spec(diff_lambda.shape), _const_spec(subln_col.shape)],
        DIFF_PAD_W, tq, scratch)


def _decode_kernel(pt_ref, qab_ref, qc_ref, lfnew_ref, dl_ref, subln_ref, tri_ref, *rest,
                   pages_per_step, lambda_init):
    del pt_ref
    P = pages_per_step
    fox_refs = rest[0:P]
    lf_refs = rest[P:2 * P]
    sb_refs = rest[2 * P:3 * P]
    diff_refs = rest[3 * P:4 * P]
    o_ref = rest[4 * P]
    (qa_s, qb_s, qd_s, m_a, l_a, acc_a, carry_a, acc_b, carry_b, m_c, l_c, acc_c) = rest[4 * P + 1:]
    W = BRANCH_W
    R = DEC_ROWS
    page = LANES
    step = pl.program_id(1)
    n_steps = pl.num_programs(1)
    lane = lax.broadcasted_iota(jnp.int32, (R, W), 1)
    rowi = lax.broadcasted_iota(jnp.int32, (R, W), 0)
    head64 = _in_group(lane, rowi, HEAD_DIM)
    comp48 = _in_group(lane, rowi, DIFF_DQ)
    tri_ones = tri_ref[...]

    def bcast(x):
        return jnp.broadcast_to(x, (R, W))

    @pl.when(step == 0)
    def _init():
        qab = qab_ref[0]
        qc = qc_ref[0]
        zero = jnp.zeros((R, W), F32)
        qa = jnp.where(head64, bcast(qab[:, 0:W].astype(F32)), zero)
        qb = jnp.where(head64, bcast(qab[:, 3 * W:4 * W].astype(F32)), zero)
        qd = jnp.where(comp48, bcast(qc[:, 0:W].astype(F32)), zero)
        qa_s[...] = qa.astype(BF16)
        qb_s[...] = qb.astype(BF16)
        qd_s[...] = qd.astype(BF16)
        ka = bcast(qab[:, W:2 * W].astype(F32))
        m_a[...] = jnp.broadcast_to(jnp.sum(qa * ka, axis=1, keepdims=True), (R, LANES))
        l_a[...] = jnp.ones((R, LANES), F32)
        acc_a[...] = bcast(qab[:, 2 * W:3 * W].astype(F32))
        lane_f = lax.broadcasted_iota(jnp.int32, (R, F_PAD), 1)
        row_f = lax.broadcasted_iota(jnp.int32, (R, F_PAD), 0)
        lf_col = jnp.sum(jnp.where(lane_f == row_f, jnp.broadcast_to(lfnew_ref[0], (R, F_PAD)), 0.0),
                         axis=1, keepdims=True)
        carry_a[...] = jnp.broadcast_to(lf_col, (R, LANES))
        acc_b[...] = jnp.zeros((R, W), F32)
        carry_b[...] = jnp.zeros((R, LANES), F32)
        kd = bcast(qc[:, W:2 * W].astype(F32))
        m_c[...] = jnp.broadcast_to(jnp.sum(qd * kd, axis=1, keepdims=True), (R, LANES))
        l_c[...] = jnp.ones((R, LANES), F32)
        acc_c[...] = bcast(qc[:, 2 * W:3 * W].astype(F32))

    def keys(ref):
        return ref[0:W, :].astype(BF16)

    def values(ref):
        return ref[W:2 * W, :].astype(BF16)

    def rows(x, j):
        return x[j * R:(j + 1) * R]

    def softmax_weights(s_parts, m_ref, l_ref):
        m = m_ref[...]
        block_max = s_parts[0]
        for s in s_parts[1:]:
            block_max = jnp.maximum(block_max, s)
        m_new = jnp.maximum(m, jnp.max(block_max, axis=1, keepdims=True))
        alpha = jnp.exp(m - m_new)
        p_parts = [jnp.exp(s - m_new) for s in s_parts]
        p_sum = p_parts[0]
        for p in p_parts[1:]:
            p_sum = p_sum + p
        l_ref[...] = alpha * l_ref[...] + jnp.sum(p_sum, axis=1, keepdims=True)
        m_ref[...] = m_new
        return alpha, p_parts

    def weighted_values(w_parts, v_refs):
        pv = None
        for j in range(P):
            term = _dot_nt(w_parts[j].astype(BF16), values(v_refs[j]))
            pv = term if pv is None else pv + term
        return pv

    z_a = [_dot(qa_s[...], keys(fox_refs[j])) for j in range(P)]
    z_b = jnp.concatenate([_dot(qb_s[...], keys(sb_refs[j])) for j in range(P)], axis=0)
    z_c = [_dot(qd_s[...], keys(diff_refs[j])) for j in range(P)]

    log_hit, log_fail = _log_hit_and_fail(z_b)
    no_head = jnp.zeros((R - LF_ROWS, page), F32)
    log_forget = jnp.concatenate(
        [part for j in range(P) for part in (lf_refs[j][...], no_head)], axis=0)
    hi, lo = _split_hi_lo(jnp.concatenate([log_fail, log_forget], axis=0))
    sums = _dot(hi, tri_ones) + _dot(lo, tri_ones)
    suffix_b, total_b = sums[0:P * R, 0:page], sums[0:P * R, page:2 * page]
    suffix_a, total_a = sums[P * R:2 * P * R, 0:page], sums[P * R:2 * P * R, page:2 * page]

    later_a = carry_a[...]
    later_b = carry_b[...]
    s_a, w_b = [], []
    for j in range(P):
        s_a.append(z_a[j] + rows(suffix_a, j) + later_a)
        w_b.append(jnp.exp(rows(log_hit, j) + rows(suffix_b, j) + later_b))
        later_a = later_a + rows(total_a, j)
        later_b = later_b + rows(total_b, j)
    carry_a[...] = later_a
    carry_b[...] = later_b

    alpha_a, p_a = softmax_weights(s_a, m_a, l_a)
    alpha_c, p_c = softmax_weights(z_c, m_c, l_c)
    widen = lambda a: jnp.tile(a, (1, W // LANES))
    acc_a[...] = widen(alpha_a) * acc_a[...] + weighted_values(p_a, fox_refs)
    acc_b[...] += weighted_values(w_b, sb_refs)
    acc_c[...] = widen(alpha_c) * acc_c[...] + weighted_values(p_c, diff_refs)

    @pl.when(step == n_steps - 1)
    def _finish():
        zero = jnp.zeros((R, W), F32)
        o_a = jnp.sum(jnp.where(head64, acc_a[...] / l_a[:, 0:1], zero), axis=0, keepdims=True)
        o_b = jnp.sum(jnp.where(head64, acc_b[...], zero), axis=0, keepdims=True)
        n_c = acc_c[...] / l_c[:, 0:1]
        odd_row = (rowi & 1) == 1
        head_lo = (rowi - (rowi & 1)) * DIFF_DQ
        own_head = (lane >= head_lo) & (lane < head_lo + DIFF_DV)
        first = jnp.sum(jnp.where(own_head & ~odd_row, n_c, zero), axis=0, keepdims=True)
        second = jnp.sum(jnp.where(own_head & odd_row, n_c, zero), axis=0, keepdims=True)
        o_c = first - _diff_lambda(dl_ref[...], lambda_init) * second
        lane1 = lax.broadcasted_iota(jnp.int32, (1, W), 1)
        inv = jnp.zeros((1, W), F32)
        for head in range(H_DIFF):
            in_head = _in_group(lane1, head, DIFF_DV)
            ms = jnp.sum(jnp.where(in_head, o_c * o_c, 0.0), axis=1, keepdims=True) * (1.0 / DIFF_DV)
            inv = jnp.where(in_head, lax.rsqrt(ms + SUBLN_EPS), inv)
        o_c = o_c * inv * subln_ref[...] * (1.0 - lambda_init)
        o_ref[0, :, 0:W] = o_a.astype(o_ref.dtype)
        o_ref[0, :, W:2 * W] = o_b.astype(o_ref.dtype)
        o_ref[0, :, 2 * W:3 * W] = o_c.astype(o_ref.dtype)


def _decode_attention(page_table, layer, qkv_ab, qkv_c, logf_new, diff_lambda, subln_row, tri_ones,
                      cache_fox, cache_lf_t, cache_sb, cache_diff, lambda_init, pages_per_step):
    nb, n_pages = page_table.shape
    P = pages_per_step
    assert n_pages % P == 0
    n_steps = n_pages // P
    page = cache_fox.shape[3]
    W = BRANCH_W
    R = DEC_ROWS

    def page_spec(shape, j):
        def index(b, s, pt):
            return (layer, pt[b, n_pages - 1 - (s * P + j)], 0, 0)
        return pl.BlockSpec((None, None) + shape, index)

    per_seq = lambda b, s, pt: (b, 0, 0)
    fixed = lambda b, s, pt: (0, 0)
    in_specs = [
        pl.BlockSpec((1, 1, 6 * W), per_seq),
        pl.BlockSpec((1, 1, 3 * W), per_seq),
        pl.BlockSpec((1, 1, F_PAD), per_seq),
        pl.BlockSpec(diff_lambda.shape, fixed),
        pl.BlockSpec((1, W), fixed),
        pl.BlockSpec(tri_ones.shape, fixed),
    ]
    in_specs += [page_spec((2 * W, page), j) for j in range(P)]
    in_specs += [page_spec((LF_ROWS, page), j) for j in range(P)]
    in_specs += [page_spec((2 * W, page), j) for j in range(P)]
    in_specs += [page_spec((2 * W, page), j) for j in range(P)]
    scratch = [pltpu.VMEM((R, W), BF16)] * 3 + [
        pltpu.VMEM((R, LANES), F32), pltpu.VMEM((R, LANES), F32), pltpu.VMEM((R, W), F32),
        pltpu.VMEM((R, LANES), F32),
        pltpu.VMEM((R, W), F32), pltpu.VMEM((R, LANES), F32),
        pltpu.VMEM((R, LANES), F32), pltpu.VMEM((R, LANES), F32), pltpu.VMEM((R, W), F32),
    ]
    grid_spec = pltpu.PrefetchScalarGridSpec(
        num_scalar_prefetch=1, grid=(nb, n_steps), in_specs=in_specs,
        out_specs=pl.BlockSpec((1, 1, 3 * W), per_seq), scratch_shapes=scratch)
    out = pl.pallas_call(
        functools.partial(_decode_kernel, pages_per_step=P, lambda_init=lambda_init),
        grid_spec=grid_spec,
        out_shape=jax.ShapeDtypeStruct((nb, 1, 3 * W), BF16),
        compiler_params=_cparams(("parallel", "arbitrary")),
        name="decode_attention",
    )(page_table, qkv_ab.reshape(nb, 1, 6 * W), qkv_c.reshape(nb, 1, 3 * W),
      logf_new.reshape(nb, 1, F_PAD), diff_lambda, subln_row, tri_ones,
      *([cache_fox] * P), *([cache_lf_t] * P), *([cache_sb] * P), *([cache_diff] * P))
    return out.reshape(nb, 3 * W)


def _merge_kernel(x_ref, oa_ref, ob_ref, oc_ref, gpre_ref, wg_ref, wa_ref, wb_ref, wc_ref,
                  wout_ref, gpost_ref, y_ref):
    x = x_ref[...]
    d = x.shape[1]
    h = _rms_norm(x, gpre_ref[...], NORM_EPS).astype(BF16)
    merged = None
    for n, (o_ref, w_ref) in enumerate(((oa_ref, wa_ref), (ob_ref, wb_ref), (oc_ref, wc_ref))):
        gate = jax.nn.sigmoid(_dot(h, wg_ref[:, n * d:(n + 1) * d]))
        term = gate * _dot(o_ref[...], w_ref[...])
        merged = term if merged is None else merged + term
    mix = _dot(merged.astype(BF16), wout_ref[...])
    y_ref[...] = x + _rms_norm(mix, gpost_ref[...], NORM_EPS)


def _merge(x, o_a, o_b, o_c, gpre, wg, wa, wb, wc, wout, gpost, tm):
    n, d = x.shape
    row = lambda i: (i, 0)
    fixed = lambda i: (0, 0)
    full = lambda a: pl.BlockSpec(a.shape, fixed)
    return pl.pallas_call(
        _merge_kernel,
        grid=(n // tm,),
        in_specs=[pl.BlockSpec((tm, d), row), pl.BlockSpec((tm, o_a.shape[1]), row),
                  pl.BlockSpec((tm, o_b.shape[1]), row), pl.BlockSpec((tm, o_c.shape[1]), row),
                  full(gpre), full(wg), full(wa), full(wb), full(wc), full(wout), full(gpost)],
        out_specs=pl.BlockSpec((tm, d), row),
        out_shape=jax.ShapeDtypeStruct((n, d), F32),
        compiler_params=_cparams(("parallel",)),
        name="merge",
    )(x, o_a, o_b, o_c, gpre, wg, wa, wb, wc, wout, gpost)


def _ffn_kernel(x_ref, gpre_ref, wgu_ref, wdown_ref, gpost_ref, y_ref, act_ref, *, chunk):
    x = x_ref[...]
    d_ff = wdown_ref.shape[0]
    h = _rms_norm(x, gpre_ref[...], NORM_EPS).astype(BF16)
    for c0 in range(0, d_ff, chunk):
        gate = _dot(h, wgu_ref[:, c0:c0 + chunk])
        up = _dot(h, wgu_ref[:, d_ff + c0:d_ff + c0 + chunk])
        act_ref[:, c0:c0 + chunk] = (gate * jax.nn.sigmoid(gate) * up).astype(BF16)
    ffn = _dot(act_ref[...], wdown_ref[...])
    y_ref[...] = x + _rms_norm(ffn, gpost_ref[...], NORM_EPS)


def _ffn(x, gpre, wgu, wdown, gpost, tm, chunk):
    n, d = x.shape
    d_ff = wdown.shape[0]
    assert d_ff % chunk == 0
    row = lambda i: (i, 0)
    fixed = lambda i: (0, 0)
    full = lambda a: pl.BlockSpec(a.shape, fixed)
    return pl.pallas_call(
        functools.partial(_ffn_kernel, chunk=chunk),
        grid=(n // tm,),
        in_specs=[pl.BlockSpec((tm, d), row), full(gpre), full(wgu), full(wdown), full(gpost)],
        out_specs=pl.BlockSpec((tm, d), row),
        out_shape=jax.ShapeDtypeStruct((n, d), F32),
        scratch_shapes=[pltpu.VMEM((tm, d_ff), BF16)],
        compiler_params=_cparams(("parallel",)),
        name="ffn",
    )(x, gpre, wgu, wdown, gpost)


def _lambda_init(layer):
    return 0.8 - 0.6 * math.exp(-0.3 * layer)


def _rotary_tables(pos):
    half = ROPE_DIM // 2
    inv_freq = ROPE_THETA ** (-jnp.arange(0, ROPE_DIM, 2, dtype=F32) / ROPE_DIM)
    ang = pos.astype(F32)[:, None] * inv_freq[None, :]
    cos, sin = jnp.cos(ang), jnp.sin(ang)
    n = pos.shape[0]
    rest = DIFF_DQ - ROPE_DIM
    ones, zeros = jnp.ones((n, rest), F32), jnp.zeros((n, rest), F32)
    zh = jnp.zeros((n, half), F32)
    reps = BRANCH_W // DIFF_DQ
    cos_t = jnp.tile(jnp.concatenate([cos, cos, ones], axis=1), (1, reps))
    sina_t = jnp.tile(jnp.concatenate([-sin, zh, zeros], axis=1), (1, reps))
    sinb_t = jnp.tile(jnp.concatenate([zh, sin, zeros], axis=1), (1, reps))
    return cos_t, sina_t, sinb_t


def _layer_params(l, norm_mix_pre, norm_mix_post, norm_ffn_pre, norm_ffn_post, w_in, b_forget,
                  diff_lambda, diff_subln, w_branch, w_out, w_gate_up, w_down):
    W = BRANCH_W
    d = w_in.shape[1]
    w = w_in[l]
    q_scale = HEAD_DIM ** -0.5
    cols = [w[:, 0:W] * q_scale, w[:, W:3 * W], w[:, 3 * W:4 * W] * q_scale, w[:, 4 * W:9 * W],
            jnp.pad(w[:, 9 * W:9 * W + H_FOX], ((0, 0), (0, F_PAD - H_FOX)))]
    w_c = w_branch[l, 2]
    w_c_pad = jnp.pad(w_c.reshape(H_DIFF, DIFF_DV, d), ((0, 0), (0, LANES - DIFF_DV), (0, 0)))
    row = lambda v: v.reshape(1, -1).astype(F32)
    return dict(
        w_qkv=jnp.concatenate(cols, axis=1).astype(BF16),
        b_f=jnp.pad(b_forget[l], (0, F_PAD - H_FOX)).reshape(1, F_PAD).astype(F32),
        w_gate=w[:, 9 * W + H_FOX:].astype(BF16),
        w_a=w_branch[l, 0].astype(BF16), w_b=w_branch[l, 1].astype(BF16),
        w_c=w_c.astype(BF16), w_c_pad=w_c_pad.reshape(DIFF_PAD_W, d).astype(BF16),
        w_out=w_out[l].astype(BF16), w_gu=w_gate_up[l].astype(BF16), w_down=w_down[l].astype(BF16),
        g_mix_pre=row(norm_mix_pre[l]), g_mix_post=row(norm_mix_post[l]),
        g_ffn_pre=row(norm_ffn_pre[l]), g_ffn_post=row(norm_ffn_post[l]),
        diff_lambda=diff_lambda[l].astype(F32),
        subln=row(jnp.tile(diff_subln[l], H_DIFF)),
        subln_col=jnp.pad(diff_subln[l], (0, LANES - DIFF_DV)).reshape(LANES, 1).astype(F32),
        lambda_init=_lambda_init(l),
    )


def _strict_lower_ones(n):
    j = lax.broadcasted_iota(jnp.int32, (n, n), 0)
    s = lax.broadcasted_iota(jnp.int32, (n, n), 1)
    return (j > s).astype(BF16)


N_PIECES = 3
ONES_LANE = H_FOX * N_PIECES


def _fox_decay_pieces(c):
    b, t, _ = c.shape
    hi = _bf16_prefix(c)
    mid = _bf16_prefix(c - hi)
    lo = c - hi - mid
    pieces = jnp.stack([hi, mid, lo], axis=-1).reshape(b, t, ONES_LANE)
    out = jnp.concatenate([pieces, jnp.ones((b, t, 1), F32),
                           jnp.zeros((b, t, LANES - ONES_LANE - 1), F32)], axis=-1)
    return out.astype(BF16)


def _fox_placements():
    n_in = H_FOX * HEAD_DIM + LANES
    place_q = np.zeros((n_in, H_FOX * LANES), np.float32)
    place_k = np.zeros((n_in, H_FOX * LANES), np.float32)
    pieces0 = H_FOX * HEAD_DIM
    for h in range(H_FOX):
        for d in range(HEAD_DIM):
            place_q[h * HEAD_DIM + d, h * LANES + d] = 1.0
            place_k[h * HEAD_DIM + d, h * LANES + d] = 1.0
        for j in range(N_PIECES):
            place_q[pieces0 + N_PIECES * h + j, h * LANES + HEAD_DIM + j] = 1.0
            place_q[pieces0 + ONES_LANE, h * LANES + HEAD_DIM + N_PIECES + j] = -1.0
            place_k[pieces0 + ONES_LANE, h * LANES + HEAD_DIM + j] = 1.0
            place_k[pieces0 + N_PIECES * h + j, h * LANES + HEAD_DIM + N_PIECES + j] = 1.0
    return jnp.asarray(place_q, BF16), jnp.asarray(place_k, BF16)


def _diff_placement():
    pad_qk = np.zeros((BRANCH_W, DIFF_PAD_W), np.float32)
    for h in range(H_DIFF):
        for c in range(2):
            for d in range(DIFF_DQ):
                pad_qk[h * DIFF_DV + c * DIFF_DQ + d, h * LANES + c * HEAD_DIM + d] = 1.0
    return jnp.asarray(pad_qk, BF16)


def _page_major_cache(cache):
    depth, n_phys, page = cache.shape[:3]
    return jnp.transpose(cache, (0, 1, 3, 4, 5, 2)).reshape(depth, n_phys, -1, page)


def _ffn_chunk(d_ff):
    for chunk in (512, 256, 128):
        if d_ff % chunk == 0:
            return chunk
    return d_ff


def _pages_per_step(n_pages):
    for p in (8, 4, 2):
        if n_pages % p == 0:
            return p
    return 1


def kernel(x_prompt, x_sample, cache_fox_kv, cache_fox_logf, cache_sb_kv, cache_diff_kv, page_table,
           norm_mix_pre, norm_mix_post, norm_ffn_pre, norm_ffn_post, w_in, b_forget, diff_lambda,
           diff_subln, w_branch, w_out, w_gate_up, w_down):
    depth = w_in.shape[0]
    b, t, d = x_prompt.shape
    nb, dec_t, _ = x_sample.shape
    assert dec_t == 1
    page = cache_fox_kv.shape[2]
    n_pages = page_table.shape[1]
    past_len = n_pages * page
    W = BRANCH_W
    d_ff = w_down.shape[1]

    tq = min(256, t)
    tm = min(512, t)
    assert t % tq == 0 and t % tm == 0 and page == LANES
    pages_per_step = _pages_per_step(n_pages)
    chunk = _ffn_chunk(d_ff)

    params = [_layer_params(l, norm_mix_pre, norm_mix_post, norm_ffn_pre, norm_ffn_post, w_in,
                            b_forget, diff_lambda, diff_subln, w_branch, w_out, w_gate_up, w_down)
              for l in range(depth)]
    tab_p = _rotary_tables(jnp.arange(t))
    tab_s = tuple(jnp.broadcast_to(a, (nb, W)) for a in _rotary_tables(jnp.full((1,), past_len)))
    tri_q = _strict_lower_ones(tq).T
    fox_consts = _fox_placements()
    diff_pad = _diff_placement()
    tri_ones_page = jnp.concatenate([_strict_lower_ones(page), jnp.ones((page, page), BF16)], axis=1)

    cache_fox = _page_major_cache(cache_fox_kv)
    cache_sb = _page_major_cache(cache_sb_kv)
    cache_diff = _page_major_cache(cache_diff_kv)
    cache_lf_t = jnp.pad(jnp.swapaxes(cache_fox_logf, 2, 3),
                         ((0, 0), (0, 0), (0, LF_ROWS - H_FOX), (0, 0)))

    def tail(x, o_a, o_b, o_c, w_c, p, rows):
        x = _merge(x, o_a, o_b, o_c, p["g_mix_pre"], p["w_gate"], p["w_a"], p["w_b"], w_c,
                   p["w_out"], p["g_mix_post"], rows)
        return _ffn(x, p["g_ffn_pre"], p["w_gu"], p["w_down"], p["g_ffn_post"], rows, chunk)

    x = x_prompt.reshape(b * t, d)
    rows_p, logf_p = None, []
    for l, p in enumerate(params):
        qkv_ab, rows_p, qkv_c, logf = _inproj(
            x, p["g_mix_pre"], p["w_qkv"], p["b_f"], *tab_p, tm, l, depth, rows_p, seq_len=t)
        logf6 = logf[:, :H_FOX].reshape(b, t, H_FOX)
        logf_p.append(logf6)
        c_t = _cumsum_time(jnp.pad(jnp.swapaxes(logf6, 1, 2), ((0, 0), (0, 8 - H_FOX), (0, 0))))
        c = jnp.swapaxes(c_t, 1, 2)[..., :H_FOX]
        qkv_ab3 = qkv_ab.reshape(b, t, 6 * W)
        qkv_c3 = qkv_c.reshape(b, t, 3 * W)
        fox_t, sb_t, diff_t = rows_p
        o_a = _fox_prompt(qkv_ab3, fox_t, l, _fox_decay_pieces(c), fox_consts, tq)
        o_b = _sb_prompt(qkv_ab3, sb_t, l, tri_q, tq)
        o_c = _diff_prompt(qkv_c3, diff_t, l, diff_pad, p["diff_lambda"], p["subln_col"],
                           p["lambda_init"], tq)
        x = tail(x, o_a.reshape(b * t, W), o_b.reshape(b * t, W), o_c.reshape(b * t, DIFF_PAD_W),
                 p["w_c_pad"], p, tm)
    y_prompt = x.reshape(b, t, d)

    x = x_sample.reshape(nb, d)
    rows_s, logf_s = None, []
    for l, p in enumerate(params):
        qkv_ab, rows_s, qkv_c, logf = _inproj(
            x, p["g_mix_pre"], p["w_qkv"], p["b_f"], *tab_s, nb, l, depth, rows_s)
        logf_s.append(logf[:, :H_FOX].reshape(nb, 1, H_FOX))
        o = _decode_attention(page_table, l, qkv_ab, qkv_c, logf, p["diff_lambda"], p["subln"],
                              tri_ones_page, cache_fox, cache_lf_t, cache_sb, cache_diff,
                              p["lambda_init"], pages_per_step)
        x = tail(x, o[:, 0:W], o[:, W:2 * W], o[:, 2 * W:3 * W], p["w_c"], p, nb)
    y_sample = x.reshape(nb, 1, d)

    def token_major(rows_t, heads, dim):
        return jnp.transpose(rows_t.reshape(depth, b, 2, heads, dim, t), (0, 1, 5, 2, 3, 4))

    fox_p = token_major(rows_p[0], H_FOX, HEAD_DIM)
    sb_p = token_major(rows_p[1], H_SB, HEAD_DIM)
    diff_p = token_major(rows_p[2], H_DIFF, DIFF_DV)
    fox_s = rows_s[0].reshape(depth, nb, 1, 2, H_FOX, HEAD_DIM)
    sb_s = rows_s[1].reshape(depth, nb, 1, 2, H_SB, HEAD_DIM)
    diff_s = rows_s[2].reshape(depth, nb, 1, 2, H_DIFF, DIFF_DV)
    return (y_prompt, y_sample, fox_p, jnp.stack(logf_p), sb_p, diff_p,
            fox_s, jnp.stack(logf_s), sb_s, diff_s)
```

```python
import functools
import math

import jax
import jax.numpy as jnp
import numpy as np
from jax import lax
from jax.experimental import pallas as pl
from jax.experimental.pallas import tpu as pltpu

F32 = jnp.float32
BF16 = jnp.bfloat16

HEAD_DIM = 64
H_FOX = 6
H_SB = 6
H_DIFF = 4
DIFF_DQ = 48
DIFF_DV = 2 * DIFF_DQ
BRANCH_W = 384
N_BRANCH = 3
ROPE_DIM = DIFF_DQ // 4
ROPE_THETA = 500000.0
NORM_EPS = 1e-6
SUBLN_EPS = 1e-5

LANES = 128
HEADS_PER_LANE_BLOCK = LANES // HEAD_DIM
F_PAD = LANES
DIFF_PAD_W = H_DIFF * LANES
NEG_BIG = -1e30
EXP_IS_ZERO = -104.0
VMEM_LIMIT = 56 * 1024 * 1024
DEC_ROWS = 16
LF_ROWS = 8
STAGE_LAG = 4
PREP_ROWS = 512


def _cparams(sem):
    return pltpu.CompilerParams(dimension_semantics=sem, vmem_limit_bytes=VMEM_LIMIT)


def _rms_norm(x, gain, eps):
    ms = jnp.mean(x * x, axis=-1, keepdims=True)
    return x * lax.rsqrt(ms + eps) * gain


def _softplus_neg_abs(z):
    return jnp.log1p(jnp.exp(-jnp.abs(z)))


def _log_hit_and_fail(z):
    log_fail = jnp.minimum(-z, 0.0) - jnp.log(1.0 + jnp.exp(-jnp.abs(z)))
    return log_fail + z, log_fail


def _bf16_prefix(x):
    bits = lax.bitcast_convert_type(x, jnp.uint32) & jnp.uint32(0xFFFF0000)
    return lax.bitcast_convert_type(bits, F32)


def _split_hi_lo(x):
    hi = _bf16_prefix(x)
    return hi.astype(BF16), (x - hi).astype(BF16)


def _in_group(index, group, width):
    lo = group * width
    return (index >= lo) & (index < lo + width)


def _dot(a, b):
    return jnp.dot(a, b, preferred_element_type=F32)


def _dot_nt(a, b):
    return lax.dot_general(a, b, (((1,), (1,)), ((), ())), preferred_element_type=F32)


def _inproj_kernel(x_ref, gain_ref, w_ref, bf_ref, cos_ref, sina_ref, sinb_ref, *rest, rows_transposed):
    qkv_ab_ref, foxkv_ref, sbkv_ref, diffkv_ref, qkv_c_ref, logf_ref = rest[-6:]
    W = BRANCH_W
    h = _rms_norm(x_ref[...], gain_ref[...], NORM_EPS).astype(BF16)

    pairs = {}

    def proj(c, width=W):
        pair = c // 2
        if pair not in pairs:
            lo = pair * 2 * W
            pairs[pair] = _dot(h, w_ref[:, lo:min(lo + 2 * W, w_ref.shape[1])])
        lo = (c % 2) * W
        return pairs[pair][:, lo:lo + width]

    def rotary(r):
        return (r * cos_ref[...] + pltpu.roll(r, W - ROPE_DIM // 2, 1) * sina_ref[...]
                + pltpu.roll(r, ROPE_DIM // 2, 1) * sinb_ref[...])

    def store_rows(kv_ref, j, r):
        if rows_transposed:
            kv_ref[j * W:(j + 1) * W, :] = r.T
        else:
            kv_ref[:, j * W:(j + 1) * W] = r

    for c, kv_ref in ((0, foxkv_ref), (3, sbkv_ref)):
        qkv_ab_ref[:, c * W:(c + 1) * W] = proj(c).astype(BF16)
        for j in range(2):
            r = proj(c + 1 + j)
            qkv_ab_ref[:, (c + 1 + j) * W:(c + 2 + j) * W] = r.astype(BF16)
            store_rows(kv_ref, j, r)

    qc = rotary(proj(6)) * (DIFF_DQ ** -0.5)
    qkv_c_ref[:, 0:W] = qc.astype(BF16)
    kc = rotary(proj(7))
    qkv_c_ref[:, W:2 * W] = kc.astype(BF16)
    store_rows(diffkv_ref, 0, kc)
    vc = proj(8)
    qkv_c_ref[:, 2 * W:3 * W] = vc.astype(BF16)
    store_rows(diffkv_ref, 1, vc)

    f = proj(9, F_PAD) + bf_ref[...]
    logf_ref[...] = jnp.minimum(f, 0.0) - _softplus_neg_abs(f)


def _inproj(x, gain, w, bf, cos_t, sina_t, sinb_t, tm, layer, depth, rows_so_far, seq_len=None):
    n, d = x.shape
    W = BRANCH_W
    n_tab = cos_t.shape[0] // tm
    row = lambda i: (i, 0)
    fixed = lambda i: (0, 0)
    tab = lambda i: (i % n_tab, 0)
    if seq_len is None:
        slab = pl.BlockSpec((None, tm, 2 * W), lambda i: (layer, i, 0))
        rows_shape = jax.ShapeDtypeStruct((depth, n, 2 * W), F32)
    else:
        assert seq_len == cos_t.shape[0]
        slab = pl.BlockSpec((None, None, 2 * W, tm), lambda i: (layer, i // n_tab, 0, i % n_tab))
        rows_shape = jax.ShapeDtypeStruct((depth, n // seq_len, 2 * W, seq_len), F32)
    out_shape = (
        jax.ShapeDtypeStruct((n, 6 * W), BF16), rows_shape, rows_shape, rows_shape,
        jax.ShapeDtypeStruct((n, 3 * W), BF16),
        jax.ShapeDtypeStruct((n, F_PAD), F32),
    )
    in_specs = [
        pl.BlockSpec((tm, d), row),
        pl.BlockSpec((1, d), fixed),
        pl.BlockSpec(w.shape, fixed),
        pl.BlockSpec((1, F_PAD), fixed),
        pl.BlockSpec((tm, W), tab),
        pl.BlockSpec((tm, W), tab),
        pl.BlockSpec((tm, W), tab),
    ]
    args = [x, gain, w, bf, cos_t, sina_t, sinb_t]
    aliases = {}
    if rows_so_far is None:
        rows_so_far = tuple(jnp.zeros(rows_shape.shape, F32) for _ in range(3))
    for j, buf in enumerate(rows_so_far):
        aliases[len(args)] = 1 + j
        args.append(buf)
        in_specs.append(pl.BlockSpec(memory_space=pl.ANY))
    qkv_ab, fox_rows, sb_rows, diff_rows, qkv_c, logf = pl.pallas_call(
        functools.partial(_inproj_kernel, rows_transposed=seq_len is not None),
        grid=(n // tm,),
        in_specs=in_specs,
        out_specs=(pl.BlockSpec((tm, 6 * W), row), slab, slab, slab,
                   pl.BlockSpec((tm, 3 * W), row), pl.BlockSpec((tm, F_PAD), row)),
        out_shape=out_shape,
        input_output_aliases=aliases,
        compiler_params=_cparams(("parallel",)),
        name="inproj",
    )(*args)
    return qkv_ab, (fox_rows, sb_rows, diff_rows), qkv_c, logf


def _cumsum_kernel(x_ref, o_ref):
    x = x_ref[0]
    t = x.shape[-1]
    lane = lax.broadcasted_iota(jnp.int32, x.shape, 1)
    shift = 1
    while shift < t:
        x = x + jnp.where(lane >= shift, pltpu.roll(x, shift, 1), 0.0)
        shift *= 2
    o_ref[0] = x


def _cumsum_time(logf_t):
    b, r, t = logf_t.shape
    blk = pl.BlockSpec((1, r, t), lambda i: (i, 0, 0))
    return pl.pallas_call(
        _cumsum_kernel, grid=(b,), in_specs=[blk], out_specs=blk,
        out_shape=jax.ShapeDtypeStruct(logf_t.shape, F32),
        compiler_params=_cparams(("parallel",)), name="cumsum_logf",
    )(logf_t)


def _lane_block(ref, rows, blk):
    return ref[0, rows, blk * LANES:(blk + 1) * LANES]


def _softmax_update(s, vt_ext, m_ref, acc_ref):
    m = m_ref[...]
    m_new = jnp.maximum(m, jnp.max(s, axis=0, keepdims=True))
    p = jnp.exp(s - m_new)
    acc_ref[...] = jnp.exp(m - m_new) * acc_ref[...] + _dot(vt_ext, p.astype(BF16))
    m_ref[...] = m_new


def _staggered(n_chains, stages, lag):
    state = [None] * n_chains
    for step in range(n_chains + (len(stages) - 1) * lag):
        for k, stage in enumerate(stages):
            c = step - k * lag
            if 0 <= c < n_chains:
                state[c] = stage(c, state[c])


def _key_query_iota(tq):
    key = lax.broadcasted_iota(jnp.int32, (tq, tq), 0)
    query = lax.broadcasted_iota(jnp.int32, (tq, tq), 1)
    return key, query


def _for_key_blocks(i, tq, block, newest_first, still_live=None):
    def visit(step):
        kb = (i - 1 - step) if newest_first else step
        block(pl.multiple_of(kb * tq, tq), False)

    def body(step, carry):
        visit(step)
        return carry

    q0 = pl.multiple_of(i * tq, tq)
    if not newest_first:
        lax.fori_loop(0, i, body, 0)
        block(q0, True)
    elif still_live is None:
        block(q0, True)
        lax.fori_loop(0, i, body, 0)
    else:
        block(q0, True)

        def live_body(carry):
            visit(carry[0])
            return carry[0] + 1, still_live()

        lax.while_loop(lambda carry: (carry[0] < i) & carry[1], live_body, (0, still_live()))


def _fox_prompt_kernel(q_ref, k_ref, vt_ref, cq_ref, ck_ref, place_q_ref, place_k_ref,
                       o_ref, qx_s, kx_s, vt_s, m_s, acc_s, *, tq):
    i = pl.program_id(1)
    t = k_ref.shape[1]
    key, query = _key_query_iota(tq)
    causal = key <= query

    @pl.when(i == 0)
    def _per_batch_row():
        for r0 in range(0, t, PREP_ROWS):
            rows = slice(r0, min(r0 + PREP_ROWS, t))
            kx = _dot(jnp.concatenate([k_ref[0, rows, :], ck_ref[0, rows, :]], axis=1), place_k_ref[...])
            kx_s[rows, :] = kx.astype(BF16)
        for head in range(H_FOX):
            vt_s[head * LANES:head * LANES + HEAD_DIM, :] = (
                vt_ref[head * HEAD_DIM:(head + 1) * HEAD_DIM, :].astype(BF16))
            vt_s[head * LANES + HEAD_DIM:(head + 1) * LANES, :] = jnp.ones((LANES - HEAD_DIM, t), BF16)

    qx = _dot(jnp.concatenate([q_ref[0], cq_ref[0]], axis=1), place_q_ref[...])
    qx_s[...] = qx.astype(BF16)
    for head in range(H_FOX):
        m_s[head] = jnp.full((1, tq), NEG_BIG, F32)
        acc_s[head] = jnp.zeros((LANES, tq), F32)

    def block(k0, diagonal):
        def scores(head, _):
            k = kx_s[pl.ds(k0, tq), head * LANES:(head + 1) * LANES]
            s = _dot_nt(k, qx_s[:, head * LANES:(head + 1) * LANES])
            return jnp.where(causal, s, NEG_BIG) if diagonal else s

        def update(head, s):
            vt = vt_s[head * LANES:(head + 1) * LANES, pl.ds(k0, tq)]
            _softmax_update(s, vt, m_s.at[head], acc_s.at[head])

        _staggered(H_FOX, (scores, update), STAGE_LAG)

    _for_key_blocks(i, tq, block, newest_first=False)
    for blk in range(H_FOX // HEADS_PER_LANE_BLOCK):
        halves = []
        for sub in range(HEADS_PER_LANE_BLOCK):
            a = acc_s[blk * HEADS_PER_LANE_BLOCK + sub]
            halves.append(a[0:HEAD_DIM] / a[HEAD_DIM:HEAD_DIM + 1])
        o_t = jnp.concatenate(halves, axis=0)
        o_ref[0, :, blk * LANES:(blk + 1) * LANES] = o_t.T.astype(o_ref.dtype)


def _sb_prompt_kernel(q_ref, k_ref, vt_ref, tri_ref, o_ref, qh_s, later_s, acc_s, *, tq):
    i = pl.program_id(1)
    lane = lax.broadcasted_iota(jnp.int32, (tq, LANES), 1)
    key, query = _key_query_iota(tq)
    strict = key < query
    for head in range(H_SB):
        blk, sub = divmod(head, HEADS_PER_LANE_BLOCK)
        q_pair = _lane_block(q_ref, slice(None), blk)
        qh_s[head] = jnp.where(_in_group(lane, sub, HEAD_DIM), q_pair, jnp.zeros_like(q_pair))
        later_s[head] = jnp.zeros((1, tq), F32)
        acc_s[head] = jnp.zeros((HEAD_DIM, tq), F32)

    def block(k0, diagonal):
        def logits(head, _):
            k = _lane_block(k_ref, pl.ds(k0, tq), head // HEADS_PER_LANE_BLOCK)
            log_hit, log_fail = _log_hit_and_fail(_dot_nt(k, qh_s[head]))
            if diagonal:
                log_fail = jnp.where(strict, log_fail, 0.0)
            return log_hit, _split_hi_lo(log_fail), jnp.sum(log_fail, axis=0, keepdims=True)

        def suffix_sums(head, state):
            log_hit, (hi, lo), fail_sum = state
            tri = tri_ref[...]
            return log_hit + (_dot(tri, hi) + _dot(tri, lo)), fail_sum

        def accumulate(head, state):
            log_w, fail_sum = state
            vt = vt_ref[head * HEAD_DIM:(head + 1) * HEAD_DIM, pl.ds(k0, tq)].astype(BF16)
            w = jnp.exp(log_w + later_s[head])
            if diagonal:
                w = jnp.where(strict, w, 0.0)
            acc_s[head] += _dot(vt, w.astype(BF16))
            later_s[head] += fail_sum

        _staggered(H_SB, (logits, suffix_sums, accumulate), STAGE_LAG)

    def still_live():
        return jnp.max(later_s[...]) > EXP_IS_ZERO

    _for_key_blocks(i, tq, block, newest_first=True, still_live=still_live)
    for blk in range(H_SB // HEADS_PER_LANE_BLOCK):
        o_t = jnp.concatenate([acc_s[blk * HEADS_PER_LANE_BLOCK + sub]
                               for sub in range(HEADS_PER_LANE_BLOCK)], axis=0)
        o_ref[0, :, blk * LANES:(blk + 1) * LANES] = o_t.T.astype(o_ref.dtype)


def _diff_lambda(dl, lambda_init):
    l1 = jnp.sum(dl[0:1] * dl[1:2], axis=1, keepdims=True)
    l2 = jnp.sum(dl[2:3] * dl[3:4], axis=1, keepdims=True)
    return jnp.exp(l1) - jnp.exp(l2) + lambda_init


def _diff_prompt_kernel(q_ref, k_ref, vt_ref, pad_qk_ref, dl_ref, subln_ref, o_ref,
                        kp_s, vt_s, qh_s, m_s, acc_s, *, tq, lambda_init):
    i = pl.program_id(1)
    t = k_ref.shape[1]
    lane = lax.broadcasted_iota(jnp.int32, (tq, LANES), 1)
    key = lax.broadcasted_iota(jnp.int32, (tq, 2 * tq), 0)
    column = lax.broadcasted_iota(jnp.int32, (tq, 2 * tq), 1)
    causal = key <= jnp.where(column >= tq, column - tq, column)

    @pl.when(i == 0)
    def _per_batch_row():
        for r0 in range(0, t, PREP_ROWS):
            rows = slice(r0, min(r0 + PREP_ROWS, t))
            kp_s[rows, :] = _dot(k_ref[0, rows, :], pad_qk_ref[...]).astype(BF16)
        for head in range(H_DIFF):
            vt_s[head * LANES:head * LANES + DIFF_DV, :] = (
                vt_ref[head * DIFF_DV:(head + 1) * DIFF_DV, :].astype(BF16))
            vt_s[head * LANES + DIFF_DV:(head + 1) * LANES, :] = jnp.ones((LANES - DIFF_DV, t), BF16)

    q_pad = _dot(q_ref[0], pad_qk_ref[...]).astype(BF16)
    for head in range(H_DIFF):
        q_head = q_pad[:, head * LANES:(head + 1) * LANES]
        for comp in range(2):
            qh_s[head, comp * tq:(comp + 1) * tq, :] = jnp.where(
                _in_group(lane, comp, HEAD_DIM), q_head, jnp.zeros_like(q_head))
        m_s[head] = jnp.full((1, 2 * tq), NEG_BIG, F32)
        acc_s[head] = jnp.zeros((LANES, 2 * tq), F32)

    def block(k0, diagonal):
        def scores(head, _):
            s = _dot_nt(kp_s[pl.ds(k0, tq), head * LANES:(head + 1) * LANES], qh_s[head])
            return jnp.where(causal, s, NEG_BIG) if diagonal else s

        def update(head, s):
            vt = vt_s[head * LANES:(head + 1) * LANES, pl.ds(k0, tq)]
            _softmax_update(s, vt, m_s.at[head], acc_s.at[head])

        _staggered(H_DIFF, (scores, update), min(STAGE_LAG, H_DIFF - 1))

    _for_key_blocks(i, tq, block, newest_first=False)
    lam = _diff_lambda(dl_ref[...], lambda_init)
    gain = subln_ref[0:DIFF_DV, :] * (1.0 - lambda_init)
    for head in range(H_DIFF):
        a0 = acc_s[head, :, 0:tq]
        a1 = acc_s[head, :, tq:2 * tq]
        o = (a0[0:DIFF_DV] / a0[DIFF_DV:DIFF_DV + 1]
             - lam * (a1[0:DIFF_DV] / a1[DIFF_DV:DIFF_DV + 1]))
        ms = jnp.sum(o * o, axis=0, keepdims=True) * (1.0 / DIFF_DV)
        o = o * lax.rsqrt(ms + SUBLN_EPS) * gain
        o_t = jnp.concatenate([o, jnp.zeros((LANES - DIFF_DV, tq), F32)], axis=0)
        o_ref[0, :, head * LANES:(head + 1) * LANES] = o_t.T.astype(o_ref.dtype)


def _prompt_attention_call(kernel, name, args, in_specs, out_w, tq, scratch):
    b, t = args[0].shape[0], args[0].shape[1]
    return pl.pallas_call(
        functools.partial(kernel, tq=tq),
        grid=(b, t // tq),
        in_specs=in_specs,
        out_specs=pl.BlockSpec((1, tq, out_w), lambda bi, i: (bi, i, 0)),
        out_shape=jax.ShapeDtypeStruct((b, t, out_w), BF16),
        scratch_shapes=scratch,
        compiler_params=_cparams(("parallel", "arbitrary")),
        name=name,
    )(*args)


def _q_tile_spec(tq, width, col_block=0):
    return pl.BlockSpec((1, tq, width), lambda bi, i: (bi, i, col_block))


def _per_batch_spec(rows, width, col_block=0):
    return pl.BlockSpec((1, rows, width), lambda bi, i: (bi, 0, col_block))


def _const_spec(shape):
    return pl.BlockSpec(shape, lambda bi, i: (0,) * len(shape))


def _values_spec(layer, t):
    return pl.BlockSpec((None, None, BRANCH_W, t), lambda bi, i: (layer, bi, 1, 0))


def _fox_prompt(qkv, rows_t, layer, decay, consts, tq):
    b, t, _ = qkv.shape
    W = BRANCH_W
    wx = H_FOX * LANES
    place_q, place_k = consts
    scratch = [pltpu.VMEM((tq, wx), BF16), pltpu.VMEM((t, wx), BF16), pltpu.VMEM((wx, t), BF16),
               pltpu.VMEM((H_FOX, 1, tq), F32), pltpu.VMEM((H_FOX, LANES, tq), F32)]
    return _prompt_attention_call(
        _fox_prompt_kernel, "fox_prompt", (qkv, qkv, rows_t, decay, decay, place_q, place_k),
        [_q_tile_spec(tq, W, 0), _per_batch_spec(t, W, 1), _values_spec(layer, t),
         _q_tile_spec(tq, LANES), _per_batch_spec(t, LANES),
         _const_spec(place_q.shape), _const_spec(place_k.shape)], W, tq, scratch)


def _sb_prompt(qkv, rows_t, layer, tri, tq):
    b, t, _ = qkv.shape
    W = BRANCH_W
    scratch = [pltpu.VMEM((H_SB, tq, LANES), BF16), pltpu.VMEM((H_SB, 1, tq), F32),
               pltpu.VMEM((H_SB, HEAD_DIM, tq), F32)]
    return _prompt_attention_call(
        _sb_prompt_kernel, "sb_prompt", (qkv, qkv, rows_t, tri),
        [_q_tile_spec(tq, W, 3), _per_batch_spec(t, W, 4), _values_spec(layer, t),
         _const_spec(tri.shape)], W, tq, scratch)


def _diff_prompt(qkv, rows_t, layer, pad_qk, diff_lambda, subln_col, lambda_init, tq):
    b, t, _ = qkv.shape
    W = BRANCH_W
    scratch = [pltpu.VMEM((t, DIFF_PAD_W), BF16), pltpu.VMEM((DIFF_PAD_W, t), BF16),
               pltpu.VMEM((H_DIFF, 2 * tq, LANES), BF16), pltpu.VMEM((H_DIFF, 1, 2 * tq), F32),
               pltpu.VMEM((H_DIFF, LANES, 2 * tq), F32)]
    return _prompt_attention_call(
        functools.partial(_diff_prompt_kernel, lambda_init=lambda_init), "diff_prompt",
        (qkv, qkv, rows_t, pad_qk, diff_lambda, subln_col),
        [_q_tile_spec(tq, W, 0), _per_batch_spec(t, W, 1), _values_spec(layer, t),
         _const_spec(pad_qk.shape), _const_spec(diff_lambda.shape), _const_spec(subln_col.shape)],
        DIFF_PAD_W, tq, scratch)


def _decode_kernel(pt_ref, qab_ref, qc_ref, lfnew_ref, dl_ref, subln_ref, tri_ref, *rest,
                   pages_per_step, lambda_init):
    del pt_ref
    P = pages_per_step
    fox_refs = rest[0:P]
    lf_refs = rest[P:2 * P]
    sb_refs = rest[2 * P:3 * P]
    diff_refs = rest[3 * P:4 * P]
    o_ref = rest[4 * P]
    (qa_s, qb_s, qd_s, m_a, l_a, acc_a, carry_a, acc_b, carry_b, m_c, l_c, acc_c) = rest[4 * P + 1:]
    W = BRANCH_W
    R = DEC_ROWS
    page = LANES
    step = pl.program_id(1)
    n_steps = pl.num_programs(1)
    lane = lax.broadcasted_iota(jnp.int32, (R, W), 1)
    rowi = lax.broadcasted_iota(jnp.int32, (R, W), 0)
    head64 = _in_group(lane, rowi, HEAD_DIM)
    comp48 = _in_group(lane, rowi, DIFF_DQ)
    tri_ones = tri_ref[...]

    def bcast(x):
        return jnp.broadcast_to(x, (R, W))

    @pl.when(step == 0)
    def _init():
        qab = qab_ref[0]
        qc = qc_ref[0]
        zero = jnp.zeros((R, W), F32)
        qa = jnp.where(head64, bcast(qab[:, 0:W].astype(F32)), zero)
        qb = jnp.where(head64, bcast(qab[:, 3 * W:4 * W].astype(F32)), zero)
        qd = jnp.where(comp48, bcast(qc[:, 0:W].astype(F32)), zero)
        qa_s[...] = qa.astype(BF16)
        qb_s[...] = qb.astype(BF16)
        qd_s[...] = qd.astype(BF16)
        ka = bcast(qab[:, W:2 * W].astype(F32))
        m_a[...] = jnp.broadcast_to(jnp.sum(qa * ka, axis=1, keepdims=True), (R, LANES))
        l_a[...] = jnp.ones((R, LANES), F32)
        acc_a[...] = bcast(qab[:, 2 * W:3 * W].astype(F32))
        lane_f = lax.broadcasted_iota(jnp.int32, (R, F_PAD), 1)
        row_f = lax.broadcasted_iota(jnp.int32, (R, F_PAD), 0)
        lf_col = jnp.sum(jnp.where(lane_f == row_f, jnp.broadcast_to(lfnew_ref[0], (R, F_PAD)), 0.0),
                         axis=1, keepdims=True)
        carry_a[...] = jnp.broadcast_to(lf_col, (R, LANES))
        acc_b[...] = jnp.zeros((R, W), F32)
        carry_b[...] = jnp.zeros((R, LANES), F32)
        kd = bcast(qc[:, W:2 * W].astype(F32))
        m_c[...] = jnp.broadcast_to(jnp.sum(qd * kd, axis=1, keepdims=True), (R, LANES))
        l_c[...] = jnp.ones((R, LANES), F32)
        acc_c[...] = bcast(qc[:, 2 * W:3 * W].astype(F32))

    def keys(ref):
        return ref[0:W, :].astype(BF16)

    def values(ref):
        return ref[W:2 * W, :].astype(BF16)

    def rows(x, j):
        return x[j * R:(j + 1) * R]

    def softmax_weights(s_parts, m_ref, l_ref):
        m = m_ref[...]
        block_max = s_parts[0]
        for s in s_parts[1:]:
            block_max = jnp.maximum(block_max, s)
        m_new = jnp.maximum(m, jnp.max(block_max, axis=1, keepdims=True))
        alpha = jnp.exp(m - m_new)
        p_parts = [jnp.exp(s - m_new) for s in s_parts]
        p_sum = p_parts[0]
        for p in p_parts[1:]:
            p_sum = p_sum + p
        l_ref[...] = alpha * l_ref[...] + jnp.sum(p_sum, axis=1, keepdims=True)
        m_ref[...] = m_new
        return alpha, p_parts

    def weighted_values(w_parts, v_refs):
        pv = None
        for j in range(P):
            term = _dot_nt(w_parts[j].astype(BF16), values(v_refs[j]))
            pv = term if pv is None else pv + term
        return pv

    z_a = [_dot(qa_s[...], keys(fox_refs[j])) for j in range(P)]
    z_b = jnp.concatenate([_dot(qb_s[...], keys(sb_refs[j])) for j in range(P)], axis=0)
    z_c = [_dot(qd_s[...], keys(diff_refs[j])) for j in range(P)]

    log_hit, log_fail = _log_hit_and_fail(z_b)
    no_head = jnp.zeros((R - LF_ROWS, page), F32)
    log_forget = jnp.concatenate(
        [part for j in range(P) for part in (lf_refs[j][...], no_head)], axis=0)
    hi, lo = _split_hi_lo(jnp.concatenate([log_fail, log_forget], axis=0))
    sums = _dot(hi, tri_ones) + _dot(lo, tri_ones)
    suffix_b, total_b = sums[0:P * R, 0:page], sums[0:P * R, page:2 * page]
    suffix_a, total_a = sums[P * R:2 * P * R, 0:page], sums[P * R:2 * P * R, page:2 * page]

    later_a = carry_a[...]
    later_b = carry_b[...]
    s_a, w_b = [], []
    for j in range(P):
        s_a.append(z_a[j] + rows(suffix_a, j) + later_a)
        w_b.append(jnp.exp(rows(log_hit, j) + rows(suffix_b, j) + later_b))
        later_a = later_a + rows(total_a, j)
        later_b = later_b + rows(total_b, j)
    carry_a[...] = later_a
    carry_b[...] = later_b

    alpha_a, p_a = softmax_weights(s_a, m_a, l_a)
    alpha_c, p_c = softmax_weights(z_c, m_c, l_c)
    widen = lambda a: jnp.tile(a, (1, W // LANES))
    acc_a[...] = widen(alpha_a) * acc_a[...] + weighted_values(p_a, fox_refs)
    acc_b[...] += weighted_values(w_b, sb_refs)
    acc_c[...] = widen(alpha_c) * acc_c[...] + weighted_values(p_c, diff_refs)

    @pl.when(step == n_steps - 1)
    def _finish():
        zero = jnp.zeros((R, W), F32)
        o_a = jnp.sum(jnp.where(head64, acc_a[...] / l_a[:, 0:1], zero), axis=0, keepdims=True)
        o_b = jnp.sum(jnp.where(head64, acc_b[...], zero), axis=0, keepdims=True)
        n_c = acc_c[...] / l_c[:, 0:1]
        odd_row = (rowi & 1) == 1
        head_lo = (rowi - (rowi & 1)) * DIFF_DQ
        own_head = (lane >= head_lo) & (lane < head_lo + DIFF_DV)
        first = jnp.sum(jnp.where(own_head & ~odd_row, n_c, zero), axis=0, keepdims=True)
        second = jnp.sum(jnp.where(own_head & odd_row, n_c, zero), axis=0, keepdims=True)
        o_c = first - _diff_lambda(dl_ref[...], lambda_init) * second
        lane1 = lax.broadcasted_iota(jnp.int32, (1, W), 1)
        inv = jnp.zeros((1, W), F32)
        for head in range(H_DIFF):
            in_head = _in_group(lane1, head, DIFF_DV)
            ms = jnp.sum(jnp.where(in_head, o_c * o_c, 0.0), axis=1, keepdims=True) * (1.0 / DIFF_DV)
            inv = jnp.where(in_head, lax.rsqrt(ms + SUBLN_EPS), inv)
        o_c = o_c * inv * subln_ref[...] * (1.0 - lambda_init)
        o_ref[0, :, 0:W] = o_a.astype(o_ref.dtype)
        o_ref[0, :, W:2 * W] = o_b.astype(o_ref.dtype)
        o_ref[0, :, 2 * W:3 * W] = o_c.astype(o_ref.dtype)


def _decode_attention(page_table, layer, qkv_ab, qkv_c, logf_new, diff_lambda, subln_row, tri_ones,
                      cache_fox, cache_lf_t, cache_sb, cache_diff, lambda_init, pages_per_step):
    nb, n_pages = page_table.shape
    P = pages_per_step
    assert n_pages % P == 0
    n_steps = n_pages // P
    page = cache_fox.shape[3]
    W = BRANCH_W
    R = DEC_ROWS

    def page_spec(shape, j):
        def index(b, s, pt):
            return (layer, pt[b, n_pages - 1 - (s * P + j)], 0, 0)
        return pl.BlockSpec((None, None) + shape, index)

    per_seq = lambda b, s, pt: (b, 0, 0)
    fixed = lambda b, s, pt: (0, 0)
    in_specs = [
        pl.BlockSpec((1, 1, 6 * W), per_seq),
        pl.BlockSpec((1, 1, 3 * W), per_seq),
        pl.BlockSpec((1, 1, F_PAD), per_seq),
        pl.BlockSpec(diff_lambda.shape, fixed),
        pl.BlockSpec((1, W), fixed),
        pl.BlockSpec(tri_ones.shape, fixed),
    ]
    in_specs += [page_spec((2 * W, page), j) for j in range(P)]
    in_specs += [page_spec((LF_ROWS, page), j) for j in range(P)]
    in_specs += [page_spec((2 * W, page), j) for j in range(P)]
    in_specs += [page_spec((2 * W, page), j) for j in range(P)]
    scratch = [pltpu.VMEM((R, W), BF16)] * 3 + [
        pltpu.VMEM((R, LANES), F32), pltpu.VMEM((R, LANES), F32), pltpu.VMEM((R, W), F32),
        pltpu.VMEM((R, LANES), F32),
        pltpu.VMEM((R, W), F32), pltpu.VMEM((R, LANES), F32),
        pltpu.VMEM((R, LANES), F32), pltpu.VMEM((R, LANES), F32), pltpu.VMEM((R, W), F32),
    ]
    grid_spec = pltpu.PrefetchScalarGridSpec(
        num_scalar_prefetch=1, grid=(nb, n_steps), in_specs=in_specs,
        out_specs=pl.BlockSpec((1, 1, 3 * W), per_seq), scratch_shapes=scratch)
    out = pl.pallas_call(
        functools.partial(_decode_kernel, pages_per_step=P, lambda_init=lambda_init),
        grid_spec=grid_spec,
        out_shape=jax.ShapeDtypeStruct((nb, 1, 3 * W), BF16),
        compiler_params=_cparams(("parallel", "arbitrary")),
        name="decode_attention",
    )(page_table, qkv_ab.reshape(nb, 1, 6 * W), qkv_c.reshape(nb, 1, 3 * W),
      logf_new.reshape(nb, 1, F_PAD), diff_lambda, subln_row, tri_ones,
      *([cache_fox] * P), *([cache_lf_t] * P), *([cache_sb] * P), *([cache_diff] * P))
    return out.reshape(nb, 3 * W)


def _merge_kernel(x_ref, oa_ref, ob_ref, oc_ref, gpre_ref, wg_ref, wa_ref, wb_ref, wc_ref,
                  wout_ref, gpost_ref, y_ref):
    x = x_ref[...]
    d = x.shape[1]
    h = _rms_norm(x, gpre_ref[...], NORM_EPS).astype(BF16)
    merged = None
    for n, (o_ref, w_ref) in enumerate(((oa_ref, wa_ref), (ob_ref, wb_ref), (oc_ref, wc_ref))):
        gate = jax.nn.sigmoid(_dot(h, wg_ref[:, n * d:(n + 1) * d]))
        term = gate * _dot(o_ref[...], w_ref[...])
        merged = term if merged is None else merged + term
    mix = _dot(merged.astype(BF16), wout_ref[...])
    y_ref[...] = x + _rms_norm(mix, gpost_ref[...], NORM_EPS)


def _merge(x, o_a, o_b, o_c, gpre, wg, wa, wb, wc, wout, gpost, tm):
    n, d = x.shape
    row = lambda i: (i, 0)
    fixed = lambda i: (0, 0)
    full = lambda a: pl.BlockSpec(a.shape, fixed)
    return pl.pallas_call(
        _merge_kernel,
        grid=(n // tm,),
        in_specs=[pl.BlockSpec((tm, d), row), pl.BlockSpec((tm, o_a.shape[1]), row),
                  pl.BlockSpec((tm, o_b.shape[1]), row), pl.BlockSpec((tm, o_c.shape[1]), row),
                  full(gpre), full(wg), full(wa), full(wb), full(wc), full(wout), full(gpost)],
        out_specs=pl.BlockSpec((tm, d), row),
        out_shape=jax.ShapeDtypeStruct((n, d), F32),
        compiler_params=_cparams(("parallel",)),
        name="merge",
    )(x, o_a, o_b, o_c, gpre, wg, wa, wb, wc, wout, gpost)


def _ffn_kernel(x_ref, gpre_ref, wgu_ref, wdown_ref, gpost_ref, y_ref, act_ref, *, chunk):
    x = x_ref[...]
    d_ff = wdown_ref.shape[0]
    h = _rms_norm(x, gpre_ref[...], NORM_EPS).astype(BF16)
    for c0 in range(0, d_ff, chunk):
        gate = _dot(h, wgu_ref[:, c0:c0 + chunk])
        up = _dot(h, wgu_ref[:, d_ff + c0:d_ff + c0 + chunk])
        act_ref[:, c0:c0 + chunk] = (gate * jax.nn.sigmoid(gate) * up).astype(BF16)
    ffn = _dot(act_ref[...], wdown_ref[...])
    y_ref[...] = x + _rms_norm(ffn, gpost_ref[...], NORM_EPS)


def _ffn(x, gpre, wgu, wdown, gpost, tm, chunk):
    n, d = x.shape
    d_ff = wdown.shape[0]
    assert d_ff % chunk == 0
    row = lambda i: (i, 0)
    fixed = lambda i: (0, 0)
    full = lambda a: pl.BlockSpec(a.shape, fixed)
    return pl.pallas_call(
        functools.partial(_ffn_kernel, chunk=chunk),
        grid=(n // tm,),
        in_specs=[pl.BlockSpec((tm, d), row), full(gpre), full(wgu), full(wdown), full(gpost)],
        out_specs=pl.BlockSpec((tm, d), row),
        out_shape=jax.ShapeDtypeStruct((n, d), F32),
        scratch_shapes=[pltpu.VMEM((tm, d_ff), BF16)],
        compiler_params=_cparams(("parallel",)),
        name="ffn",
    )(x, gpre, wgu, wdown, gpost)


def _lambda_init(layer):
    return 0.8 - 0.6 * math.exp(-0.3 * layer)


def _rotary_tables(pos):
    half = ROPE_DIM // 2
    inv_freq = ROPE_THETA ** (-jnp.arange(0, ROPE_DIM, 2, dtype=F32) / ROPE_DIM)
    ang = pos.astype(F32)[:, None] * inv_freq[None, :]
    cos, sin = jnp.cos(ang), jnp.sin(ang)
    n = pos.shape[0]
    rest = DIFF_DQ - ROPE_DIM
    ones, zeros = jnp.ones((n, rest), F32), jnp.zeros((n, rest), F32)
    zh = jnp.zeros((n, half), F32)
    reps = BRANCH_W // DIFF_DQ
    cos_t = jnp.tile(jnp.concatenate([cos, cos, ones], axis=1), (1, reps))
    sina_t = jnp.tile(jnp.concatenate([-sin, zh, zeros], axis=1), (1, reps))
    sinb_t = jnp.tile(jnp.concatenate([zh, sin, zeros], axis=1), (1, reps))
    return cos_t, sina_t, sinb_t


def _layer_params(l, norm_mix_pre, norm_mix_post, norm_ffn_pre, norm_ffn_post, w_in, b_forget,
                  diff_lambda, diff_subln, w_branch, w_out, w_gate_up, w_down):
    W = BRANCH_W
    d = w_in.shape[1]
    w = w_in[l]
    q_scale = HEAD_DIM ** -0.5
    cols = [w[:, 0:W] * q_scale, w[:, W:3 * W], w[:, 3 * W:4 * W] * q_scale, w[:, 4 * W:9 * W],
            jnp.pad(w[:, 9 * W:9 * W + H_FOX], ((0, 0), (0, F_PAD - H_FOX)))]
    w_c = w_branch[l, 2]
    w_c_pad = jnp.pad(w_c.reshape(H_DIFF, DIFF_DV, d), ((0, 0), (0, LANES - DIFF_DV), (0, 0)))
    row = lambda v: v.reshape(1, -1).astype(F32)
    return dict(
        w_qkv=jnp.concatenate(cols, axis=1).astype(BF16),
        b_f=jnp.pad(b_forget[l], (0, F_PAD - H_FOX)).reshape(1, F_PAD).astype(F32),
        w_gate=w[:, 9 * W + H_FOX:].astype(BF16),
        w_a=w_branch[l, 0].astype(BF16), w_b=w_branch[l, 1].astype(BF16),
        w_c=w_c.astype(BF16), w_c_pad=w_c_pad.reshape(DIFF_PAD_W, d).astype(BF16),
        w_out=w_out[l].astype(BF16), w_gu=w_gate_up[l].astype(BF16), w_down=w_down[l].astype(BF16),
        g_mix_pre=row(norm_mix_pre[l]), g_mix_post=row(norm_mix_post[l]),
        g_ffn_pre=row(norm_ffn_pre[l]), g_ffn_post=row(norm_ffn_post[l]),
        diff_lambda=diff_lambda[l].astype(F32),
        subln=row(jnp.tile(diff_subln[l], H_DIFF)),
        subln_col=jnp.pad(diff_subln[l], (0, LANES - DIFF_DV)).reshape(LANES, 1).astype(F32),
        lambda_init=_lambda_init(l),
    )


def _strict_lower_ones(n):
    j = lax.broadcasted_iota(jnp.int32, (n, n), 0)
    s = lax.broadcasted_iota(jnp.int32, (n, n), 1)
    return (j > s).astype(BF16)


N_PIECES = 3
ONES_LANE = H_FOX * N_PIECES


def _fox_decay_pieces(c):
    b, t, _ = c.shape
    hi = _bf16_prefix(c)
    mid = _bf16_prefix(c - hi)
    lo = c - hi - mid
    pieces = jnp.stack([hi, mid, lo], axis=-1).reshape(b, t, ONES_LANE)
    out = jnp.concatenate([pieces, jnp.ones((b, t, 1), F32),
                           jnp.zeros((b, t, LANES - ONES_LANE - 1), F32)], axis=-1)
    return out.astype(BF16)


def _fox_placements():
    n_in = H_FOX * HEAD_DIM + LANES
    place_q = np.zeros((n_in, H_FOX * LANES), np.float32)
    place_k = np.zeros((n_in, H_FOX * LANES), np.float32)
    pieces0 = H_FOX * HEAD_DIM
    for h in range(H_FOX):
        for d in range(HEAD_DIM):
            place_q[h * HEAD_DIM + d, h * LANES + d] = 1.0
            place_k[h * HEAD_DIM + d, h * LANES + d] = 1.0
        for j in range(N_PIECES):
            place_q[pieces0 + N_PIECES * h + j, h * LANES + HEAD_DIM + j] = 1.0
            place_q[pieces0 + ONES_LANE, h * LANES + HEAD_DIM + N_PIECES + j] = -1.0
            place_k[pieces0 + ONES_LANE, h * LANES + HEAD_DIM + j] = 1.0
            place_k[pieces0 + N_PIECES * h + j, h * LANES + HEAD_DIM + N_PIECES + j] = 1.0
    return jnp.asarray(place_q, BF16), jnp.asarray(place_k, BF16)


def _diff_placement():
    pad_qk = np.zeros((BRANCH_W, DIFF_PAD_W), np.float32)
    for h in range(H_DIFF):
        for c in range(2):
            for d in range(DIFF_DQ):
                pad_qk[h * DIFF_DV + c * DIFF_DQ + d, h * LANES + c * HEAD_DIM + d] = 1.0
    return jnp.asarray(pad_qk, BF16)


def _page_major_cache(cache):
    depth, n_phys, page = cache.shape[:3]
    return jnp.transpose(cache, (0, 1, 3, 4, 5, 2)).reshape(depth, n_phys, -1, page)


def _ffn_chunk(d_ff):
    for chunk in (512, 256, 128):
        if d_ff % chunk == 0:
            return chunk
    return d_ff


def _pages_per_step(n_pages):
    for p in (8, 4, 2):
        if n_pages % p == 0:
            return p
    return 1


def kernel(x_prompt, x_sample, cache_fox_kv, cache_fox_logf, cache_sb_kv, cache_diff_kv, page_table,
           norm_mix_pre, norm_mix_post, norm_ffn_pre, norm_ffn_post, w_in, b_forget, diff_lambda,
           diff_subln, w_branch, w_out, w_gate_up, w_down):
    depth = w_in.shape[0]
    b, t, d = x_prompt.shape
    nb, dec_t, _ = x_sample.shape
    assert dec_t == 1
    page = cache_fox_kv.shape[2]
    n_pages = page_table.shape[1]
    past_len = n_pages * page
    W = BRANCH_W
    d_ff = w_down.shape[1]

    tq = min(256, t)
    tm = min(512, t)
    assert t % tq == 0 and t % tm == 0 and page == LANES
    pages_per_step = _pages_per_step(n_pages)
    chunk = _ffn_chunk(d_ff)

    params = [_layer_params(l, norm_mix_pre, norm_mix_post, norm_ffn_pre, norm_ffn_post, w_in,
                            b_forget, diff_lambda, diff_subln, w_branch, w_out, w_gate_up, w_down)
              for l in range(depth)]
    tab_p = _rotary_tables(jnp.arange(t))
    tab_s = tuple(jnp.broadcast_to(a, (nb, W)) for a in _rotary_tables(jnp.full((1,), past_len)))
    tri_q = _strict_lower_ones(tq).T
    fox_consts = _fox_placements()
    diff_pad = _diff_placement()
    tri_ones_page = jnp.concatenate([_strict_lower_ones(page), jnp.ones((page, page), BF16)], axis=1)

    cache_fox = _page_major_cache(cache_fox_kv)
    cache_sb = _page_major_cache(cache_sb_kv)
    cache_diff = _page_major_cache(cache_diff_kv)
    cache_lf_t = jnp.pad(jnp.swapaxes(cache_fox_logf, 2, 3),
                         ((0, 0), (0, 0), (0, LF_ROWS - H_FOX), (0, 0)))

    def tail(x, o_a, o_b, o_c, w_c, p, rows):
        x = _merge(x, o_a, o_b, o_c, p["g_mix_pre"], p["w_gate"], p["w_a"], p["w_b"], w_c,
                   p["w_out"], p["g_mix_post"], rows)
        return _ffn(x, p["g_ffn_pre"], p["w_gu"], p["w_down"], p["g_ffn_post"], rows, chunk)

    x = x_prompt.reshape(b * t, d)
    rows_p, logf_p = None, []
    for l, p in enumerate(params):
        qkv_ab, rows_p, qkv_c, logf = _inproj(
            x, p["g_mix_pre"], p["w_qkv"], p["b_f"], *tab_p, tm, l, depth, rows_p, seq_len=t)
        logf6 = logf[:, :H_FOX].reshape(b, t, H_FOX)
        logf_p.append(logf6)
        c_t = _cumsum_time(jnp.pad(jnp.swapaxes(logf6, 1, 2), ((0, 0), (0, 8 - H_FOX), (0, 0))))
        c = jnp.swapaxes(c_t, 1, 2)[..., :H_FOX]
        qkv_ab3 = qkv_ab.reshape(b, t, 6 * W)
        qkv_c3 = qkv_c.reshape(b, t, 3 * W)
        fox_t, sb_t, diff_t = rows_p
        o_a = _fox_prompt(qkv_ab3, fox_t, l, _fox_decay_pieces(c), fox_consts, tq)
        o_b = _sb_prompt(qkv_ab3, sb_t, l, tri_q, tq)
        o_c = _diff_prompt(qkv_c3, diff_t, l, diff_pad, p["diff_lambda"], p["subln_col"],
                           p["lambda_init"], tq)
        x = tail(x, o_a.reshape(b * t, W), o_b.reshape(b * t, W), o_c.reshape(b * t, DIFF_PAD_W),
                 p["w_c_pad"], p, tm)
    y_prompt = x.reshape(b, t, d)

    x = x_sample.reshape(nb, d)
    rows_s, logf_s = None, []
    for l, p in enumerate(params):
        qkv_ab, rows_s, qkv_c, logf = _inproj(
            x, p["g_mix_pre"], p["w_qkv"], p["b_f"], *tab_s, nb, l, depth, rows_s)
        logf_s.append(logf[:, :H_FOX].reshape(nb, 1, H_FOX))
        o = _decode_attention(page_table, l, qkv_ab, qkv_c, logf, p["diff_lambda"], p["subln"],
                              tri_ones_page, cache_fox, cache_lf_t, cache_sb, cache_diff,
                              p["lambda_init"], pages_per_step)
        x = tail(x, o[:, 0:W], o[:, W:2 * W], o[:, 2 * W:3 * W], p["w_c"], p, nb)
    y_sample = x.reshape(nb, 1, d)

    def token_major(rows_t, heads, dim):
        return jnp.transpose(rows_t.reshape(depth, b, 2, heads, dim, t), (0, 1, 5, 2, 3, 4))

    fox_p = token_major(rows_p[0], H_FOX, HEAD_DIM)
    sb_p = token_major(rows_p[1], H_SB, HEAD_DIM)
    diff_p = token_major(rows_p[2], H_DIFF, DIFF_DV)
    fox_s = rows_s[0].reshape(depth, nb, 1, 2, H_FOX, HEAD_DIM)
    sb_s = rows_s[1].reshape(depth, nb, 1, 2, H_SB, HEAD_DIM)
    diff_s = rows_s[2].reshape(depth, nb, 1, 2, H_DIFF, DIFF_DV)
    return (y_prompt, y_sample, fox_p, jnp.stack(logf_p), sb_p, diff_p,
            fox_s, jnp.stack(logf_s), sb_s, diff_s)
```

```python
import functools
import math

import jax
import jax.numpy as jnp
import numpy as np
from jax import lax
from jax.experimental import pallas as pl
from jax.experimental.pallas import tpu as pltpu

F32 = jnp.float32
BF16 = jnp.bfloat16

HEAD_DIM = 64
H_FOX = 6
H_SB = 6
H_DIFF = 4
DIFF_DQ = 48
DIFF_DV = 2 * DIFF_DQ
BRANCH_W = 384
N_BRANCH = 3
ROPE_DIM = DIFF_DQ // 4
ROPE_THETA = 500000.0
NORM_EPS = 1e-6
SUBLN_EPS = 1e-5

LANES = 128
HEADS_PER_LANE_BLOCK = LANES // HEAD_DIM
F_PAD = LANES
DIFF_PAD_W = H_DIFF * LANES
NEG_BIG = -1e30
EXP_IS_ZERO = -104.0
VMEM_LIMIT = 56 * 1024 * 1024
DEC_ROWS = 16
LF_ROWS = 8
STAGE_LAG = 4
PREP_ROWS = 512


def _cparams(sem):
    return pltpu.CompilerParams(dimension_semantics=sem, vmem_limit_bytes=VMEM_LIMIT)


def _rms_norm(x, gain, eps):
    ms = jnp.mean(x * x, axis=-1, keepdims=True)
    return x * lax.rsqrt(ms + eps) * gain


def _softplus_neg_abs(z):
    return jnp.log1p(jnp.exp(-jnp.abs(z)))


def _log_hit_and_fail(z):
    log_fail = jnp.minimum(-z, 0.0) - jnp.log(1.0 + jnp.exp(-jnp.abs(z)))
    return log_fail + z, log_fail


def _bf16_prefix(x):
    bits = lax.bitcast_convert_type(x, jnp.uint32) & jnp.uint32(0xFFFF0000)
    return lax.bitcast_convert_type(bits, F32)


def _split_hi_lo(x):
    hi = _bf16_prefix(x)
    return hi.astype(BF16), (x - hi).astype(BF16)


def _in_group(index, group, width):
    lo = group * width
    return (index >= lo) & (index < lo + width)


def _dot(a, b):
    return jnp.dot(a, b, preferred_element_type=F32)


def _dot_nt(a, b):
    return lax.dot_general(a, b, (((1,), (1,)), ((), ())), preferred_element_type=F32)


def _inproj_kernel(x_ref, gain_ref, w_ref, bf_ref, cos_ref, sina_ref, sinb_ref, *rest, rows_transposed):
    qkv_ab_ref, foxkv_ref, sbkv_ref, diffkv_ref, qkv_c_ref, logf_ref = rest[-6:]
    W = BRANCH_W
    h = _rms_norm(x_ref[...], gain_ref[...], NORM_EPS).astype(BF16)

    pairs = {}

    def proj(c, width=W):
        pair = c // 2
        if pair not in pairs:
            lo = pair * 2 * W
            pairs[pair] = _dot(h, w_ref[:, lo:min(lo + 2 * W, w_ref.shape[1])])
        lo = (c % 2) * W
        return pairs[pair][:, lo:lo + width]

    def rotary(r):
        return (r * cos_ref[...] + pltpu.roll(r, W - ROPE_DIM // 2, 1) * sina_ref[...]
                + pltpu.roll(r, ROPE_DIM // 2, 1) * sinb_ref[...])

    def store_rows(kv_ref, j, r):
        if rows_transposed:
            kv_ref[j * W:(j + 1) * W, :] = r.T
        else:
            kv_ref[:, j * W:(j + 1) * W] = r

    for c, kv_ref in ((0, foxkv_ref), (3, sbkv_ref)):
        qkv_ab_ref[:, c * W:(c + 1) * W] = proj(c).astype(BF16)
        for j in range(2):
            r = proj(c + 1 + j)
            qkv_ab_ref[:, (c + 1 + j) * W:(c + 2 + j) * W] = r.astype(BF16)
            store_rows(kv_ref, j, r)

    qc = rotary(proj(6)) * (DIFF_DQ ** -0.5)
    qkv_c_ref[:, 0:W] = qc.astype(BF16)
    kc = rotary(proj(7))
    qkv_c_ref[:, W:2 * W] = kc.astype(BF16)
    store_rows(diffkv_ref, 0, kc)
    vc = proj(8)
    qkv_c_ref[:, 2 * W:3 * W] = vc.astype(BF16)
    store_rows(diffkv_ref, 1, vc)

    f = proj(9, F_PAD) + bf_ref[...]
    logf_ref[...] = jnp.minimum(f, 0.0) - _softplus_neg_abs(f)


def _inproj(x, gain, w, bf, cos_t, sina_t, sinb_t, tm, layer, depth, rows_so_far, seq_len=None):
    n, d = x.shape
    W = BRANCH_W
    n_tab = cos_t.shape[0] // tm
    row = lambda i: (i, 0)
    fixed = lambda i: (0, 0)
    tab = lambda i: (i % n_tab, 0)
    if seq_len is None:
        slab = pl.BlockSpec((None, tm, 2 * W), lambda i: (layer, i, 0))
        rows_shape = jax.ShapeDtypeStruct((depth, n, 2 * W), F32)
    else:
        assert seq_len == cos_t.shape[0]
        slab = pl.BlockSpec((None, None, 2 * W, tm), lambda i: (layer, i // n_tab, 0, i % n_tab))
        rows_shape = jax.ShapeDtypeStruct((depth, n // seq_len, 2 * W, seq_len), F32)
    out_shape = (
        jax.ShapeDtypeStruct((n, 6 * W), BF16), rows_shape, rows_shape, rows_shape,
        jax.ShapeDtypeStruct((n, 3 * W), BF16),
        jax.ShapeDtypeStruct((n, F_PAD), F32),
    )
    in_specs = [
        pl.BlockSpec((tm, d), row),
        pl.BlockSpec((1, d), fixed),
        pl.BlockSpec(w.shape, fixed),
        pl.BlockSpec((1, F_PAD), fixed),
        pl.BlockSpec((tm, W), tab),
        pl.BlockSpec((tm, W), tab),
        pl.BlockSpec((tm, W), tab),
    ]
    args = [x, gain, w, bf, cos_t, sina_t, sinb_t]
    aliases = {}
    if rows_so_far is None:
        rows_so_far = tuple(jnp.zeros(rows_shape.shape, F32) for _ in range(3))
    for j, buf in enumerate(rows_so_far):
        aliases[len(args)] = 1 + j
        args.append(buf)
        in_specs.append(pl.BlockSpec(memory_space=pl.ANY))
    qkv_ab, fox_rows, sb_rows, diff_rows, qkv_c, logf = pl.pallas_call(
        functools.partial(_inproj_kernel, rows_transposed=seq_len is not None),
        grid=(n // tm,),
        in_specs=in_specs,
        out_specs=(pl.BlockSpec((tm, 6 * W), row), slab, slab, slab,
                   pl.BlockSpec((tm, 3 * W), row), pl.BlockSpec((tm, F_PAD), row)),
        out_shape=out_shape,
        input_output_aliases=aliases,
        compiler_params=_cparams(("parallel",)),
        name="inproj",
    )(*args)
    return qkv_ab, (fox_rows, sb_rows, diff_rows), qkv_c, logf


def _cumsum_kernel(x_ref, o_ref):
    x = x_ref[0]
    t = x.shape[-1]
    lane = lax.broadcasted_iota(jnp.int32, x.shape, 1)
    shift = 1
    while shift < t:
        x = x + jnp.where(lane >= shift, pltpu.roll(x, shift, 1), 0.0)
        shift *= 2
    o_ref[0] = x


def _cumsum_time(logf_t):
    b, r, t = logf_t.shape
    blk = pl.BlockSpec((1, r, t), lambda i: (i, 0, 0))
    return pl.pallas_call(
        _cumsum_kernel, grid=(b,), in_specs=[blk], out_specs=blk,
        out_shape=jax.ShapeDtypeStruct(logf_t.shape, F32),
        compiler_params=_cparams(("parallel",)), name="cumsum_logf",
    )(logf_t)


def _lane_block(ref, rows, blk):
    return ref[0, rows, blk * LANES:(blk + 1) * LANES]


def _softmax_update(s, vt_ext, m_ref, acc_ref):
    m = m_ref[...]
    m_new = jnp.maximum(m, jnp.max(s, axis=0, keepdims=True))
    p = jnp.exp(s - m_new)
    acc_ref[...] = jnp.exp(m - m_new) * acc_ref[...] + _dot(vt_ext, p.astype(BF16))
    m_ref[...] = m_new


def _staggered(n_chains, stages, lag):
    state = [None] * n_chains
    for step in range(n_chains + (len(stages) - 1) * lag):
        for k, stage in enumerate(stages):
            c = step - k * lag
            if 0 <= c < n_chains:
                state[c] = stage(c, state[c])


def _key_query_iota(tq):
    key = lax.broadcasted_iota(jnp.int32, (tq, tq), 0)
    query = lax.broadcasted_iota(jnp.int32, (tq, tq), 1)
    return key, query


def _for_key_blocks(i, tq, block, newest_first, still_live=None):
    def visit(step):
        kb = (i - 1 - step) if newest_first else step
        block(pl.multiple_of(kb * tq, tq), False)

    def body(step, carry):
        visit(step)
        return carry

    q0 = pl.multiple_of(i * tq, tq)
    if not newest_first:
        lax.fori_loop(0, i, body, 0)
        block(q0, True)
    elif still_live is None:
        block(q0, True)
        lax.fori_loop(0, i, body, 0)
    else:
        block(q0, True)

        def live_body(carry):
            visit(carry[0])
            return carry[0] + 1, still_live()

        lax.while_loop(lambda carry: (carry[0] < i) & carry[1], live_body, (0, still_live()))


def _fox_prompt_kernel(q_ref, k_ref, vt_ref, cq_ref, ck_ref, place_q_ref, place_k_ref,
                       o_ref, qx_s, kx_s, vt_s, m_s, acc_s, *, tq):
    i = pl.program_id(1)
    t = k_ref.shape[1]
    key, query = _key_query_iota(tq)
    causal = key <= query

    @pl.when(i == 0)
    def _per_batch_row():
        for r0 in range(0, t, PREP_ROWS):
            rows = slice(r0, min(r0 + PREP_ROWS, t))
            kx = _dot(jnp.concatenate([k_ref[0, rows, :], ck_ref[0, rows, :]], axis=1), place_k_ref[...])
            kx_s[rows, :] = kx.astype(BF16)
        for head in range(H_FOX):
            vt_s[head * LANES:head * LANES + HEAD_DIM, :] = (
                vt_ref[head * HEAD_DIM:(head + 1) * HEAD_DIM, :].astype(BF16))
            vt_s[head * LANES + HEAD_DIM:(head + 1) * LANES, :] = jnp.ones((LANES - HEAD_DIM, t), BF16)

    qx = _dot(jnp.concatenate([q_ref[0], cq_ref[0]], axis=1), place_q_ref[...])
    qx_s[...] = qx.astype(BF16)
    for head in range(H_FOX):
        m_s[head] = jnp.full((1, tq), NEG_BIG, F32)
        acc_s[head] = jnp.zeros((LANES, tq), F32)

    def block(k0, diagonal):
        def scores(head, _):
            k = kx_s[pl.ds(k0, tq), head * LANES:(head + 1) * LANES]
            s = _dot_nt(k, qx_s[:, head * LANES:(head + 1) * LANES])
            return jnp.where(causal, s, NEG_BIG) if diagonal else s

        def update(head, s):
            vt = vt_s[head * LANES:(head + 1) * LANES, pl.ds(k0, tq)]
            _softmax_update(s, vt, m_s.at[head], acc_s.at[head])

        _staggered(H_FOX, (scores, update), STAGE_LAG)

    _for_key_blocks(i, tq, block, newest_first=False)
    for blk in range(H_FOX // HEADS_PER_LANE_BLOCK):
        halves = []
        for sub in range(HEADS_PER_LANE_BLOCK):
            a = acc_s[blk * HEADS_PER_LANE_BLOCK + sub]
            halves.append(a[0:HEAD_DIM] / a[HEAD_DIM:HEAD_DIM + 1])
        o_t = jnp.concatenate(halves, axis=0)
        o_ref[0, :, blk * LANES:(blk + 1) * LANES] = o_t.T.astype(o_ref.dtype)


def _sb_prompt_kernel(q_ref, k_ref, vt_ref, tri_ref, o_ref, qh_s, later_s, acc_s, *, tq):
    i = pl.program_id(1)
    lane = lax.broadcasted_iota(jnp.int32, (tq, LANES), 1)
    key, query = _key_query_iota(tq)
    strict = key < query
    for head in range(H_SB):
        blk, sub = divmod(head, HEADS_PER_LANE_BLOCK)
        q_pair = _lane_block(q_ref, slice(None), blk)
        qh_s[head] = jnp.where(_in_group(lane, sub, HEAD_DIM), q_pair, jnp.zeros_like(q_pair))
        later_s[head] = jnp.zeros((1, tq), F32)
        acc_s[head] = jnp.zeros((HEAD_DIM, tq), F32)

    def block(k0, diagonal):
        def logits(head, _):
            k = _lane_block(k_ref, pl.ds(k0, tq), head // HEADS_PER_LANE_BLOCK)
            log_hit, log_fail = _log_hit_and_fail(_dot_nt(k, qh_s[head]))
            if diagonal:
                log_fail = jnp.where(strict, log_fail, 0.0)
            return log_hit, _split_hi_lo(log_fail), jnp.sum(log_fail, axis=0, keepdims=True)

        def suffix_sums(head, state):
            log_hit, (hi, lo), fail_sum = state
            tri = tri_ref[...]
            return log_hit + (_dot(tri, hi) + _dot(tri, lo)), fail_sum

        def accumulate(head, state):
            log_w, fail_sum = state
            vt = vt_ref[head * HEAD_DIM:(head + 1) * HEAD_DIM, pl.ds(k0, tq)].astype(BF16)
            w = jnp.exp(log_w + later_s[head])
            if diagonal:
                w = jnp.where(strict, w, 0.0)
            acc_s[head] += _dot(vt, w.astype(BF16))
            later_s[head] += fail_sum

        _staggered(H_SB, (logits, suffix_sums, accumulate), STAGE_LAG)

    def still_live():
        return jnp.max(later_s[...]) > EXP_IS_ZERO

    _for_key_blocks(i, tq, block, newest_first=True, still_live=still_live)
    for blk in range(H_SB // HEADS_PER_LANE_BLOCK):
        o_t = jnp.concatenate([acc_s[blk * HEADS_PER_LANE_BLOCK + sub]
                               for sub in range(HEADS_PER_LANE_BLOCK)], axis=0)
        o_ref[0, :, blk * LANES:(blk + 1) * LANES] = o_t.T.astype(o_ref.dtype)


def _diff_lambda(dl, lambda_init):
    l1 = jnp.sum(dl[0:1] * dl[1:2], axis=1, keepdims=True)
    l2 = jnp.sum(dl[2:3] * dl[3:4], axis=1, keepdims=True)
    return jnp.exp(l1) - jnp.exp(l2) + lambda_init


def _diff_prompt_kernel(q_ref, k_ref, vt_ref, pad_qk_ref, dl_ref, subln_ref, o_ref,
                        kp_s, vt_s, qh_s, m_s, acc_s, *, tq, lambda_init):
    i = pl.program_id(1)
    t = k_ref.shape[1]
    lane = lax.broadcasted_iota(jnp.int32, (tq, LANES), 1)
    key = lax.broadcasted_iota(jnp.int32, (tq, 2 * tq), 0)
    column = lax.broadcasted_iota(jnp.int32, (tq, 2 * tq), 1)
    causal = key <= jnp.where(column >= tq, column - tq, column)

    @pl.when(i == 0)
    def _per_batch_row():
        for r0 in range(0, t, PREP_ROWS):
            rows = slice(r0, min(r0 + PREP_ROWS, t))
            kp_s[rows, :] = _dot(k_ref[0, rows, :], pad_qk_ref[...]).astype(BF16)
        for head in range(H_DIFF):
            vt_s[head * LANES:head * LANES + DIFF_DV, :] = (
                vt_ref[head * DIFF_DV:(head + 1) * DIFF_DV, :].astype(BF16))
            vt_s[head * LANES + DIFF_DV:(head + 1) * LANES, :] = jnp.ones((LANES - DIFF_DV, t), BF16)

    q_pad = _dot(q_ref[0], pad_qk_ref[...]).astype(BF16)
    for head in range(H_DIFF):
        q_head = q_pad[:, head * LANES:(head + 1) * LANES]
        for comp in range(2):
            qh_s[head, comp * tq:(comp + 1) * tq, :] = jnp.where(
                _in_group(lane, comp, HEAD_DIM), q_head, jnp.zeros_like(q_head))
        m_s[head] = jnp.full((1, 2 * tq), NEG_BIG, F32)
        acc_s[head] = jnp.zeros((LANES, 2 * tq), F32)

    def block(k0, diagonal):
        def scores(head, _):
            s = _dot_nt(kp_s[pl.ds(k0, tq), head * LANES:(head + 1) * LANES], qh_s[head])
            return jnp.where(causal, s, NEG_BIG) if diagonal else s

        def update(head, s):
            vt = vt_s[head * LANES:(head + 1) * LANES, pl.ds(k0, tq)]
            _softmax_update(s, vt, m_s.at[head], acc_s.at[head])

        _staggered(H_DIFF, (scores, update), min(STAGE_LAG, H_DIFF - 1))

    _for_key_blocks(i, tq, block, newest_first=False)
    lam = _diff_lambda(dl_ref[...], lambda_init)
    gain = subln_ref[0:DIFF_DV, :] * (1.0 - lambda_init)
    for head in range(H_DIFF):
        a0 = acc_s[head, :, 0:tq]
        a1 = acc_s[head, :, tq:2 * tq]
        o = (a0[0:DIFF_DV] / a0[DIFF_DV:DIFF_DV + 1]
             - lam * (a1[0:DIFF_DV] / a1[DIFF_DV:DIFF_DV + 1]))
        ms = jnp.sum(o * o, axis=0, keepdims=True) * (1.0 / DIFF_DV)
        o = o * lax.rsqrt(ms + SUBLN_EPS) * gain
        o_t = jnp.concatenate([o, jnp.zeros((LANES - DIFF_DV, tq), F32)], axis=0)
        o_ref[0, :, head * LANES:(head + 1) * LANES] = o_t.T.astype(o_ref.dtype)


def _prompt_attention_call(kernel, name, args, in_specs, out_w, tq, scratch):
    b, t = args[0].shape[0], args[0].shape[1]
    return pl.pallas_call(
        functools.partial(kernel, tq=tq),
        grid=(b, t // tq),
        in_specs=in_specs,
        out_specs=pl.BlockSpec((1, tq, out_w), lambda bi, i: (bi, i, 0)),
        out_shape=jax.ShapeDtypeStruct((b, t, out_w), BF16),
        scratch_shapes=scratch,
        compiler_params=_cparams(("parallel", "arbitrary")),
        name=name,
    )(*args)


def _q_tile_spec(tq, width, col_block=0):
    return pl.BlockSpec((1, tq, width), lambda bi, i: (bi, i, col_block))


def _per_batch_spec(rows, width, col_block=0):
    return pl.BlockSpec((1, rows, width), lambda bi, i: (bi, 0, col_block))


def _const_spec(shape):
    return pl.BlockSpec(shape, lambda bi, i: (0,) * len(shape))


def _values_spec(layer, t):
    return pl.BlockSpec((None, None, BRANCH_W, t), lambda bi, i: (layer, bi, 1, 0))


def _fox_prompt(qkv, rows_t, layer, decay, consts, tq):
    b, t, _ = qkv.shape
    W = BRANCH_W
    wx = H_FOX * LANES
    place_q, place_k = consts
    scratch = [pltpu.VMEM((tq, wx), BF16), pltpu.VMEM((t, wx), BF16), pltpu.VMEM((wx, t), BF16),
               pltpu.VMEM((H_FOX, 1, tq), F32), pltpu.VMEM((H_FOX, LANES, tq), F32)]
    return _prompt_attention_call(
        _fox_prompt_kernel, "fox_prompt", (qkv, qkv, rows_t, decay, decay, place_q, place_k),
        [_q_tile_spec(tq, W, 0), _per_batch_spec(t, W, 1), _values_spec(layer, t),
         _q_tile_spec(tq, LANES), _per_batch_spec(t, LANES),
         _const_spec(place_q.shape), _const_spec(place_k.shape)], W, tq, scratch)


def _sb_prompt(qkv, rows_t, layer, tri, tq):
    b, t, _ = qkv.shape
    W = BRANCH_W
    scratch = [pltpu.VMEM((H_SB, tq, LANES), BF16), pltpu.VMEM((H_SB, 1, tq), F32),
               pltpu.VMEM((H_SB, HEAD_DIM, tq), F32)]
    return _prompt_attention_call(
        _sb_prompt_kernel, "sb_prompt", (qkv, qkv, rows_t, tri),
        [_q_tile_spec(tq, W, 3), _per_batch_spec(t, W, 4), _values_spec(layer, t),
         _const_spec(tri.shape)], W, tq, scratch)


def _diff_prompt(qkv, rows_t, layer, pad_qk, diff_lambda, subln_col, lambda_init, tq):
    b, t, _ = qkv.shape
    W = BRANCH_W
    scratch = [pltpu.VMEM((t, DIFF_PAD_W), BF16), pltpu.VMEM((DIFF_PAD_W, t), BF16),
               pltpu.VMEM((H_DIFF, 2 * tq, LANES), BF16), pltpu.VMEM((H_DIFF, 1, 2 * tq), F32),
               pltpu.VMEM((H_DIFF, LANES, 2 * tq), F32)]
    return _prompt_attention_call(
        functools.partial(_diff_prompt_kernel, lambda_init=lambda_init), "diff_prompt",
        (qkv, qkv, rows_t, pad_qk, diff_lambda, subln_col),
        [_q_tile_spec(tq, W, 0), _per_batch_spec(t, W, 1), _values_spec(layer, t),
         _const_spec(pad_qk.shape), _const_spec(diff_lambda.shape), _const_spec(subln_col.shape)],
        DIFF_PAD_W, tq, scratch)


def _decode_kernel(pt_ref, qab_ref, qc_ref, lfnew_ref, dl_ref, subln_ref, tri_ref, *rest,
                   pages_per_step, lambda_init):
    del pt_ref
    P = pages_per_step
    fox_refs = rest[0:P]
    lf_refs = rest[P:2 * P]
    sb_refs = rest[2 * P:3 * P]
    diff_refs = rest[3 * P:4 * P]
    o_ref = rest[4 * P]
    (qa_s, qb_s, qd_s, m_a, l_a, acc_a, carry_a, acc_b, carry_b, m_c, l_c, acc_c) = rest[4 * P + 1:]
    W = BRANCH_W
    R = DEC_ROWS
    page = LANES
    step = pl.program_id(1)
    n_steps = pl.num_programs(1)
    lane = lax.broadcasted_iota(jnp.int32, (R, W), 1)
    rowi = lax.broadcasted_iota(jnp.int32, (R, W), 0)
    head64 = _in_group(lane, rowi, HEAD_DIM)
    comp48 = _in_group(lane, rowi, DIFF_DQ)
    tri_ones = tri_ref[...]

    def bcast(x):
        return jnp.broadcast_to(x, (R, W))

    @pl.when(step == 0)
    def _init():
        qab = qab_ref[0]
        qc = qc_ref[0]
        zero = jnp.zeros((R, W), F32)
        qa = jnp.where(head64, bcast(qab[:, 0:W].astype(F32)), zero)
        qb = jnp.where(head64, bcast(qab[:, 3 * W:4 * W].astype(F32)), zero)
        qd = jnp.where(comp48, bcast(qc[:, 0:W].astype(F32)), zero)
        qa_s[...] = qa.astype(BF16)
        qb_s[...] = qb.astype(BF16)
        qd_s[...] = qd.astype(BF16)
        ka = bcast(qab[:, W:2 * W].astype(F32))
        m_a[...] = jnp.broadcast_to(jnp.sum(qa * ka, axis=1, keepdims=True), (R, LANES))
        l_a[...] = jnp.ones((R, LANES), F32)
        acc_a[...] = bcast(qab[:, 2 * W:3 * W].astype(F32))
        lane_f = lax.broadcasted_iota(jnp.int32, (R, F_PAD), 1)
        row_f = lax.broadcasted_iota(jnp.int32, (R, F_PAD), 0)
        lf_col = jnp.sum(jnp.where(lane_f == row_f, jnp.broadcast_to(lfnew_ref[0], (R, F_PAD)), 0.0),
                         axis=1, keepdims=True)
        carry_a[...] = jnp.broadcast_to(lf_col, (R, LANES))
        acc_b[...] = jnp.zeros((R, W), F32)
        carry_b[...] = jnp.zeros((R, LANES), F32)
        kd = bcast(qc[:, W:2 * W].astype(F32))
        m_c[...] = jnp.broadcast_to(jnp.sum(qd * kd, axis=1, keepdims=True), (R, LANES))
        l_c[...] = jnp.ones((R, LANES), F32)
        acc_c[...] = bcast(qc[:, 2 * W:3 * W].astype(F32))

    def keys(ref):
        return ref[0:W, :].astype(BF16)

    def values(ref):
        return ref[W:2 * W, :].astype(BF16)

    def rows(x, j):
        return x[j * R:(j + 1) * R]

    def softmax_weights(s_parts, m_ref, l_ref):
        m = m_ref[...]
        block_max = s_parts[0]
        for s in s_parts[1:]:
            block_max = jnp.maximum(block_max, s)
        m_new = jnp.maximum(m, jnp.max(block_max, axis=1, keepdims=True))
        alpha = jnp.exp(m - m_new)
        p_parts = [jnp.exp(s - m_new) for s in s_parts]
        p_sum = p_parts[0]
        for p in p_parts[1:]:
            p_sum = p_sum + p
        l_ref[...] = alpha * l_ref[...] + jnp.sum(p_sum, axis=1, keepdims=True)
        m_ref[...] = m_new
        return alpha, p_parts

    def weighted_values(w_parts, v_refs):
        pv = None
        for j in range(P):
            term = _dot_nt(w_parts[j].astype(BF16), values(v_refs[j]))
            pv = term if pv is None else pv + term
        return pv

    z_a = [_dot(qa_s[...], keys(fox_refs[j])) for j in range(P)]
    z_b = jnp.concatenate([_dot(qb_s[...], keys(sb_refs[j])) for j in range(P)], axis=0)
    z_c = [_dot(qd_s[...], keys(diff_refs[j])) for j in range(P)]

    log_hit, log_fail = _log_hit_and_fail(z_b)
    no_head = jnp.zeros((R - LF_ROWS, page), F32)
    log_forget = jnp.concatenate(
        [part for j in range(P) for part in (lf_refs[j][...], no_head)], axis=0)
    hi, lo = _split_hi_lo(jnp.concatenate([log_fail, log_forget], axis=0))
    sums = _dot(hi, tri_ones) + _dot(lo, tri_ones)
    suffix_b, total_b = sums[0:P * R, 0:page], sums[0:P * R, page:2 * page]
    suffix_a, total_a = sums[P * R:2 * P * R, 0:page], sums[P * R:2 * P * R, page:2 * page]

    later_a = carry_a[...]
    later_b = carry_b[...]
    s_a, w_b = [], []
    for j in range(P):
        s_a.append(z_a[j] + rows(suffix_a, j) + later_a)
        w_b.append(jnp.exp(rows(log_hit, j) + rows(suffix_b, j) + later_b))
        later_a = later_a + rows(total_a, j)
        later_b = later_b + rows(total_b, j)
    carry_a[...] = later_a
    carry_b[...] = later_b

    alpha_a, p_a = softmax_weights(s_a, m_a, l_a)
    alpha_c, p_c = softmax_weights(z_c, m_c, l_c)
    widen = lambda a: jnp.tile(a, (1, W // LANES))
    acc_a[...] = widen(alpha_a) * acc_a[...] + weighted_values(p_a, fox_refs)
    acc_b[...] += weighted_values(w_b, sb_refs)
    acc_c[...] = widen(alpha_c) * acc_c[...] + weighted_values(p_c, diff_refs)

    @pl.when(step == n_steps - 1)
    def _finish():
        zero = jnp.zeros((R, W), F32)
        o_a = jnp.sum(jnp.where(head64, acc_a[...] / l_a[:, 0:1], zero), axis=0, keepdims=True)
        o_b = jnp.sum(jnp.where(head64, acc_b[...], zero), axis=0, keepdims=True)
        n_c = acc_c[...] / l_c[:, 0:1]
        odd_row = (rowi & 1) == 1
        head_lo = (rowi - (rowi & 1)) * DIFF_DQ
        own_head = (lane >= head_lo) & (lane < head_lo + DIFF_DV)
        first = jnp.sum(jnp.where(own_head & ~odd_row, n_c, zero), axis=0, keepdims=True)
        second = jnp.sum(jnp.where(own_head & odd_row, n_c, zero), axis=0, keepdims=True)
        o_c = first - _diff_lambda(dl_ref[...], lambda_init) * second
        lane1 = lax.broadcasted_iota(jnp.int32, (1, W), 1)
        inv = jnp.zeros((1, W), F32)
        for head in range(H_DIFF):
            in_head = _in_group(lane1, head, DIFF_DV)
            ms = jnp.sum(jnp.where(in_head, o_c * o_c, 0.0), axis=1, keepdims=True) * (1.0 / DIFF_DV)
            inv = jnp.where(in_head, lax.rsqrt(ms + SUBLN_EPS), inv)
        o_c = o_c * inv * subln_ref[...] * (1.0 - lambda_init)
        o_ref[0, :, 0:W] = o_a.astype(o_ref.dtype)
        o_ref[0, :, W:2 * W] = o_b.astype(o_ref.dtype)
        o_ref[0, :, 2 * W:3 * W] = o_c.astype(o_ref.dtype)


def _decode_attention(page_table, layer, qkv_ab, qkv_c, logf_new, diff_lambda, subln_row, tri_ones,
                      cache_fox, cache_lf_t, cache_sb, cache_diff, lambda_init, pages_per_step):
    nb, n_pages = page_table.shape
    P = pages_per_step
    assert n_pages % P == 0
    n_steps = n_pages // P
    page = cache_fox.shape[3]
    W = BRANCH_W
    R = DEC_ROWS

    def page_spec(shape, j):
        def index(b, s, pt):
            return (layer, pt[b, n_pages - 1 - (s * P + j)], 0, 0)
        return pl.BlockSpec((None, None) + shape, index)

    per_seq = lambda b, s, pt: (b, 0, 0)
    fixed = lambda b, s, pt: (0, 0)
    in_specs = [
        pl.BlockSpec((1, 1, 6 * W), per_seq),
        pl.BlockSpec((1, 1, 3 * W), per_seq),
        pl.BlockSpec((1, 1, F_PAD), per_seq),
        pl.BlockSpec(diff_lambda.shape, fixed),
        pl.BlockSpec((1, W), fixed),
        pl.BlockSpec(tri_ones.shape, fixed),
    ]
    in_specs += [page_spec((2 * W, page), j) for j in range(P)]
    in_specs += [page_spec((LF_ROWS, page), j) for j in range(P)]
    in_specs += [page_spec((2 * W, page), j) for j in range(P)]
    in_specs += [page_spec((2 * W, page), j) for j in range(P)]
    scratch = [pltpu.VMEM((R, W), BF16)] * 3 + [
        pltpu.VMEM((R, LANES), F32), pltpu.VMEM((R, LANES), F32), pltpu.VMEM((R, W), F32),
        pltpu.VMEM((R, LANES), F32),
        pltpu.VMEM((R, W), F32), pltpu.VMEM((R, LANES), F32),
        pltpu.VMEM((R, LANES), F32), pltpu.VMEM((R, LANES), F32), pltpu.VMEM((R, W), F32),
    ]
    grid_spec = pltpu.PrefetchScalarGridSpec(
        num_scalar_prefetch=1, grid=(nb, n_steps), in_specs=in_specs,
        out_specs=pl.BlockSpec((1, 1, 3 * W), per_seq), scratch_shapes=scratch)
    out = pl.pallas_call(
        functools.partial(_decode_kernel, pages_per_step=P, lambda_init=lambda_init),
        grid_spec=grid_spec,
        out_shape=jax.ShapeDtypeStruct((nb, 1, 3 * W), BF16),
        compiler_params=_cparams(("parallel", "arbitrary")),
        name="decode_attention",
    )(page_table, qkv_ab.reshape(nb, 1, 6 * W), qkv_c.reshape(nb, 1, 3 * W),
      logf_new.reshape(nb, 1, F_PAD), diff_lambda, subln_row, tri_ones,
      *([cache_fox] * P), *([cache_lf_t] * P), *([cache_sb] * P), *([cache_diff] * P))
    return out.reshape(nb, 3 * W)


def _merge_kernel(x_ref, oa_ref, ob_ref, oc_ref, gpre_ref, wg_ref, wa_ref, wb_ref, wc_ref,
                  wout_ref, gpost_ref, y_ref):
    x = x_ref[...]
    d = x.shape[1]
    h = _rms_norm(x, gpre_ref[...], NORM_EPS).astype(BF16)
    merged = None
    for n, (o_ref, w_ref) in enumerate(((oa_ref, wa_ref), (ob_ref, wb_ref), (oc_ref, wc_ref))):
        gate = jax.nn.sigmoid(_dot(h, wg_ref[:, n * d:(n + 1) * d]))
        term = gate * _dot(o_ref[...], w_ref[...])
        merged = term if merged is None else merged + term
    mix = _dot(merged.astype(BF16), wout_ref[...])
    y_ref[...] = x + _rms_norm(mix, gpost_ref[...], NORM_EPS)


def _merge(x, o_a, o_b, o_c, gpre, wg, wa, wb, wc, wout, gpost, tm):
    n, d = x.shape
    row = lambda i: (i, 0)
    fixed = lambda i: (0, 0)
    full = lambda a: pl.BlockSpec(a.shape, fixed)
    return pl.pallas_call(
        _merge_kernel,
        grid=(n // tm,),
        in_specs=[pl.BlockSpec((tm, d), row), pl.BlockSpec((tm, o_a.shape[1]), row),
                  pl.BlockSpec((tm, o_b.shape[1]), row), pl.BlockSpec((tm, o_c.shape[1]), row),
                  full(gpre), full(wg), full(wa), full(wb), full(wc), full(wout), full(gpost)],
        out_specs=pl.BlockSpec((tm, d), row),
        out_shape=jax.ShapeDtypeStruct((n, d), F32),
        compiler_params=_cparams(("parallel",)),
        name="merge",
    )(x, o_a, o_b, o_c, gpre, wg, wa, wb, wc, wout, gpost)


def _ffn_kernel(x_ref, gpre_ref, wgu_ref, wdown_ref, gpost_ref, y_ref, act_ref, *, chunk):
    x = x_ref[...]
    d_ff = wdown_ref.shape[0]
    h = _rms_norm(x, gpre_ref[...], NORM_EPS).astype(BF16)
    for c0 in range(0, d_ff, chunk):
        gate = _dot(h, wgu_ref[:, c0:c0 + chunk])
        up = _dot(h, wgu_ref[:, d_ff + c0:d_ff + c0 + chunk])
        act_ref[:, c0:c0 + chunk] = (gate * jax.nn.sigmoid(gate) * up).astype(BF16)
    ffn = _dot(act_ref[...], wdown_ref[...])
    y_ref[...] = x + _rms_norm(ffn, gpost_ref[...], NORM_EPS)


def _ffn(x, gpre, wgu, wdown, gpost, tm, chunk):
    n, d = x.shape
    d_ff = wdown.shape[0]
    assert d_ff % chunk == 0
    row = lambda i: (i, 0)
    fixed = lambda i: (0, 0)
    full = lambda a: pl.BlockSpec(a.shape, fixed)
    return pl.pallas_call(
        functools.partial(_ffn_kernel, chunk=chunk),
        grid=(n // tm,),
        in_specs=[pl.BlockSpec((tm, d), row), full(gpre), full(wgu), full(wdown), full(gpost)],
        out_specs=pl.BlockSpec((tm, d), row),
        out_shape=jax.ShapeDtypeStruct((n, d), F32),
        scratch_shapes=[pltpu.VMEM((tm, d_ff), BF16)],
        compiler_params=_cparams(("parallel",)),
        name="ffn",
    )(x, gpre, wgu, wdown, gpost)


def _lambda_init(layer):
    return 0.8 - 0.6 * math.exp(-0.3 * layer)


def _rotary_tables(pos):
    half = ROPE_DIM // 2
    inv_freq = ROPE_THETA ** (-jnp.arange(0, ROPE_DIM, 2, dtype=F32) / ROPE_DIM)
    ang = pos.astype(F32)[:, None] * inv_freq[None, :]
    cos, sin = jnp.cos(ang), jnp.sin(ang)
    n = pos.shape[0]
    rest = DIFF_DQ - ROPE_DIM
    ones, zeros = jnp.ones((n, rest), F32), jnp.zeros((n, rest), F32)
    zh = jnp.zeros((n, half), F32)
    reps = BRANCH_W // DIFF_DQ
    cos_t = jnp.tile(jnp.concatenate([cos, cos, ones], axis=1), (1, reps))
    sina_t = jnp.tile(jnp.concatenate([-sin, zh, zeros], axis=1), (1, reps))
    sinb_t = jnp.tile(jnp.concatenate([zh, sin, zeros], axis=1), (1, reps))
    return cos_t, sina_t, sinb_t


def _layer_params(l, norm_mix_pre, norm_mix_post, norm_ffn_pre, norm_ffn_post, w_in, b_forget,
                  diff_lambda, diff_subln, w_branch, w_out, w_gate_up, w_down):
    W = BRANCH_W
    d = w_in.shape[1]
    w = w_in[l]
    q_scale = HEAD_DIM ** -0.5
    cols = [w[:, 0:W] * q_scale, w[:, W:3 * W], w[:, 3 * W:4 * W] * q_scale, w[:, 4 * W:9 * W],
            jnp.pad(w[:, 9 * W:9 * W + H_FOX], ((0, 0), (0, F_PAD - H_FOX)))]
    w_c = w_branch[l, 2]
    w_c_pad = jnp.pad(w_c.reshape(H_DIFF, DIFF_DV, d), ((0, 0), (0, LANES - DIFF_DV), (0, 0)))
    row = lambda v: v.reshape(1, -1).astype(F32)
    return dict(
        w_qkv=jnp.concatenate(cols, axis=1).astype(BF16),
        b_f=jnp.pad(b_forget[l], (0, F_PAD - H_FOX)).reshape(1, F_PAD).astype(F32),
        w_gate=w[:, 9 * W + H_FOX:].astype(BF16),
        w_a=w_branch[l, 0].astype(BF16), w_b=w_branch[l, 1].astype(BF16),
        w_c=w_c.astype(BF16), w_c_pad=w_c_pad.reshape(DIFF_PAD_W, d).astype(BF16),
        w_out=w_out[l].astype(BF16), w_gu=w_gate_up[l].astype(BF16), w_down=w_down[l].astype(BF16),
        g_mix_pre=row(norm_mix_pre[l]), g_mix_post=row(norm_mix_post[l]),
        g_ffn_pre=row(norm_ffn_pre[l]), g_ffn_post=row(norm_ffn_post[l]),
        diff_lambda=diff_lambda[l].astype(F32),
        subln=row(jnp.tile(diff_subln[l], H_DIFF)),
        subln_col=jnp.pad(diff_subln[l], (0, LANES - DIFF_DV)).reshape(LANES, 1).astype(F32),
        lambda_init=_lambda_init(l),
    )


def _strict_lower_ones(n):
    j = lax.broadcasted_iota(jnp.int32, (n, n), 0)
    s = lax.broadcasted_iota(jnp.int32, (n, n), 1)
    return (j > s).astype(BF16)


N_PIECES = 3
ONES_LANE = H_FOX * N_PIECES


def _fox_decay_pieces(c):
    b, t, _ = c.shape
    hi = _bf16_prefix(c)
    mid = _bf16_prefix(c - hi)
    lo = c - hi - mid
    pieces = jnp.stack([hi, mid, lo], axis=-1).reshape(b, t, ONES_LANE)
    out = jnp.concatenate([pieces, jnp.ones((b, t, 1), F32),
                           jnp.zeros((b, t, LANES - ONES_LANE - 1), F32)], axis=-1)
    return out.astype(BF16)


def _fox_placements():
    n_in = H_FOX * HEAD_DIM + LANES
    place_q = np.zeros((n_in, H_FOX * LANES), np.float32)
    place_k = np.zeros((n_in, H_FOX * LANES), np.float32)
    pieces0 = H_FOX * HEAD_DIM
    for h in range(H_FOX):
        for d in range(HEAD_DIM):
            place_q[h * HEAD_DIM + d, h * LANES + d] = 1.0
            place_k[h * HEAD_DIM + d, h * LANES + d] = 1.0
        for j in range(N_PIECES):
            place_q[pieces0 + N_PIECES * h + j, h * LANES + HEAD_DIM + j] = 1.0
            place_q[pieces0 + ONES_LANE, h * LANES + HEAD_DIM + N_PIECES + j] = -1.0
            place_k[pieces0 + ONES_LANE, h * LANES + HEAD_DIM + j] = 1.0
            place_k[pieces0 + N_PIECES * h + j, h * LANES + HEAD_DIM + N_PIECES + j] = 1.0
    return jnp.asarray(place_q, BF16), jnp.asarray(place_k, BF16)


def _diff_placement():
    pad_qk = np.zeros((BRANCH_W, DIFF_PAD_W), np.float32)
    for h in range(H_DIFF):
        for c in range(2):
            for d in range(DIFF_DQ):
                pad_qk[h * DIFF_DV + c * DIFF_DQ + d, h * LANES + c * HEAD_DIM + d] = 1.0
    return jnp.asarray(pad_qk, BF16)


def _page_major_cache(cache):
    depth, n_phys, page = cache.shape[:3]
    return jnp.transpose(cache, (0, 1, 3, 4, 5, 2)).reshape(depth, n_phys, -1, page)


def _ffn_chunk(d_ff):
    for chunk in (512, 256, 128):
        if d_ff % chunk == 0:
            return chunk
    return d_ff


def _pages_per_step(n_pages):
    for p in (16, 8, 4, 2):
        if n_pages % p == 0:
            return p
    return 1


def kernel(x_prompt, x_sample, cache_fox_kv, cache_fox_logf, cache_sb_kv, cache_diff_kv, page_table,
           norm_mix_pre, norm_mix_post, norm_ffn_pre, norm_ffn_post, w_in, b_forget, diff_lambda,
           diff_subln, w_branch, w_out, w_gate_up, w_down):
    depth = w_in.shape[0]
    b, t, d = x_prompt.shape
    nb, dec_t, _ = x_sample.shape
    assert dec_t == 1
    page = cache_fox_kv.shape[2]
    n_pages = page_table.shape[1]
    past_len = n_pages * page
    W = BRANCH_W
    d_ff = w_down.shape[1]

    tq = min(256, t)
    tm = min(512, t)
    assert t % tq == 0 and t % tm == 0 and page == LANES
    pages_per_step = _pages_per_step(n_pages)
    chunk = _ffn_chunk(d_ff)

    params = [_layer_params(l, norm_mix_pre, norm_mix_post, norm_ffn_pre, norm_ffn_post, w_in,
                            b_forget, diff_lambda, diff_subln, w_branch, w_out, w_gate_up, w_down)
              for l in range(depth)]
    tab_p = _rotary_tables(jnp.arange(t))
    tab_s = tuple(jnp.broadcast_to(a, (nb, W)) for a in _rotary_tables(jnp.full((1,), past_len)))
    tri_q = _strict_lower_ones(tq).T
    fox_consts = _fox_placements()
    diff_pad = _diff_placement()
    tri_ones_page = jnp.concatenate([_strict_lower_ones(page), jnp.ones((page, page), BF16)], axis=1)

    cache_fox = _page_major_cache(cache_fox_kv)
    cache_sb = _page_major_cache(cache_sb_kv)
    cache_diff = _page_major_cache(cache_diff_kv)
    cache_lf_t = jnp.pad(jnp.swapaxes(cache_fox_logf, 2, 3),
                         ((0, 0), (0, 0), (0, LF_ROWS - H_FOX), (0, 0)))

    def tail(x, o_a, o_b, o_c, w_c, p, rows):
        x = _merge(x, o_a, o_b, o_c, p["g_mix_pre"], p["w_gate"], p["w_a"], p["w_b"], w_c,
                   p["w_out"], p["g_mix_post"], rows)
        return _ffn(x, p["g_ffn_pre"], p["w_gu"], p["w_down"], p["g_ffn_post"], rows, chunk)

    x = x_prompt.reshape(b * t, d)
    rows_p, logf_p = None, []
    for l, p in enumerate(params):
        qkv_ab, rows_p, qkv_c, logf = _inproj(
            x, p["g_mix_pre"], p["w_qkv"], p["b_f"], *tab_p, tm, l, depth, rows_p, seq_len=t)
        logf6 = logf[:, :H_FOX].reshape(b, t, H_FOX)
        logf_p.append(logf6)
        c_t = _cumsum_time(jnp.pad(jnp.swapaxes(logf6, 1, 2), ((0, 0), (0, 8 - H_FOX), (0, 0))))
        c = jnp.swapaxes(c_t, 1, 2)[..., :H_FOX]
        qkv_ab3 = qkv_ab.reshape(b, t, 6 * W)
        qkv_c3 = qkv_c.reshape(b, t, 3 * W)
        fox_t, sb_t, diff_t = rows_p
        o_a = _fox_prompt(qkv_ab3, fox_t, l, _fox_decay_pieces(c), fox_consts, tq)
        o_b = _sb_prompt(qkv_ab3, sb_t, l, tri_q, tq)
        o_c = _diff_prompt(qkv_c3, diff_t, l, diff_pad, p["diff_lambda"], p["subln_col"],
                           p["lambda_init"], tq)
        x = tail(x, o_a.reshape(b * t, W), o_b.reshape(b * t, W), o_c.reshape(b * t, DIFF_PAD_W),
                 p["w_c_pad"], p, tm)
    y_prompt = x.reshape(b, t, d)

    x = x_sample.reshape(nb, d)
    rows_s, logf_s = None, []
    for l, p in enumerate(params):
        qkv_ab, rows_s, qkv_c, logf = _inproj(
            x, p["g_mix_pre"], p["w_qkv"], p["b_f"], *tab_s, nb, l, depth, rows_s)
        logf_s.append(logf[:, :H_FOX].reshape(nb, 1, H_FOX))
        o = _decode_attention(page_table, l, qkv_ab, qkv_c, logf, p["diff_lambda"], p["subln"],
                              tri_ones_page, cache_fox, cache_lf_t, cache_sb, cache_diff,
                              p["lambda_init"], pages_per_step)
        x = tail(x, o[:, 0:W], o[:, W:2 * W], o[:, 2 * W:3 * W], p["w_c"], p, nb)
    y_sample = x.reshape(nb, 1, d)

    def token_major(rows_t, heads, dim):
        return jnp.transpose(rows_t.reshape(depth, b, 2, heads, dim, t), (0, 1, 5, 2, 3, 4))

    fox_p = token_major(rows_p[0], H_FOX, HEAD_DIM)
    sb_p = token_major(rows_p[1], H_SB, HEAD_DIM)
    diff_p = token_major(rows_p[2], H_DIFF, DIFF_DV)
    fox_s = rows_s[0].reshape(depth, nb, 1, 2, H_FOX, HEAD_DIM)
    sb_s = rows_s[1].reshape(depth, nb, 1, 2, H_SB, HEAD_DIM)
    diff_s = rows_s[2].reshape(depth, nb, 1, 2, H_DIFF, DIFF_DV)
    return (y_prompt, y_sample, fox_p, jnp.stack(logf_p), sb_p, diff_p,
            fox_s, jnp.stack(logf_s), sb_s, diff_s)
```
